```python
import jax, jax.numpy as jnp
from jax import lax
import numpy as np

D_MODEL = 4096
BATCH = 8
SEQ = 4096
DEPTH = 2

CHUNK = 64
N_META = 16
Q_BLOCK = 128
N_MIXERS = 2
N_DELTA = (DEPTH + N_MIXERS - 1) // N_MIXERS
N_SB = DEPTH // N_MIXERS
EPS = 1e-6

DN_HEAD_K = 128
DN_HEAD_V = 128
DN_HK = D_MODEL // DN_HEAD_K
DN_HV = 2 * DN_HK
DN_KEY = DN_HK * DN_HEAD_K
DN_VAL = DN_HV * DN_HEAD_V
DN_CONV_CH = 2 * DN_KEY + DN_VAL
DN_IN = DN_CONV_CH + DN_VAL + 2 * DN_HV
CONV_K = 4

SB_HEAD = 128
SB_HEADS = D_MODEL // SB_HEAD
SB_WIDTH = SB_HEADS * SB_HEAD

kernel_name = "hybrid_deltanet_stickbreaking_meta"


def rms_norm(x, w):
    xf = x.astype(jnp.float32)
    y = xf * lax.rsqrt(jnp.mean(xf * xf, axis=-1, keepdims=True) + EPS)
    return (y * w.astype(jnp.float32)).astype(x.dtype)


def l2_norm(x):
    xf = x.astype(jnp.float32)
    return xf * lax.rsqrt(jnp.sum(xf * xf, axis=-1, keepdims=True) + EPS)


def causal_depthwise_conv(x, w):
    c = x.shape[-1]
    return lax.conv_general_dilated(
        x, w[:, None, :].astype(x.dtype), window_strides=(1,),
        padding=((CONV_K - 1, 0),), dimension_numbers=("NWC", "WIO", "NWC"),
        feature_group_count=c)


def chunk_gated_delta_rule(q, k, v, g, beta):
    b, h, lc, dk = q.shape
    dv = v.shape[-1]
    n = lc // CHUNK
    f32 = jnp.float32
    q = q.astype(f32) * (dk ** -0.5)
    k = k.astype(f32)
    v = v.astype(f32)
    rs = lambda t: t.reshape(b, h, n, CHUNK, *t.shape[3:])
    q, k, v, g, beta = rs(q), rs(k), rs(v), rs(g.astype(f32)), rs(beta.astype(f32))
    g = jnp.cumsum(g, axis=-1)
    idx = jnp.arange(CHUNK)
    lower_incl = idx[:, None] >= idx[None, :]
    strict = idx[:, None] > idx[None, :]
    decay = jnp.exp(jnp.where(lower_incl, g[..., :, None] - g[..., None, :], -jnp.inf))
    k_beta = k * beta[..., None]
    v_beta = v * beta[..., None]
    l_mat = jnp.where(strict, jnp.einsum("bhnid,bhnjd->bhnij", k_beta, k) * decay, 0.0)
    eye = jnp.eye(CHUNK, dtype=f32)
    t_mat = lax.linalg.triangular_solve(eye + l_mat, jnp.broadcast_to(eye, l_mat.shape),
                                        left_side=True, lower=True)
    u = jnp.einsum("bhnij,bhnjd->bhnid", t_mat, v_beta)
    w = jnp.einsum("bhnij,bhnjd->bhnid", t_mat, k_beta * jnp.exp(g)[..., None])
    attn = jnp.where(lower_incl, jnp.einsum("bhnid,bhnjd->bhnij", q, k) * decay, 0.0)
    q_dec = q * jnp.exp(g)[..., None]
    g_last = g[..., -1]
    k_dec = k * jnp.exp(g_last[..., None] - g)[..., None]

    def step(state, xs):
        u_c, w_c, qd_c, kd_c, a_c, gl_c = xs
        v_new = u_c - jnp.einsum("bhck,bhkv->bhcv", w_c, state)
        out = jnp.einsum("bhck,bhkv->bhcv", qd_c, state) + jnp.einsum("bhij,bhjv->bhiv", a_c, v_new)
        state = state * jnp.exp(gl_c)[..., None, None] + jnp.einsum("bhck,bhcv->bhkv", kd_c, v_new)
        return state, out

    mv = lambda t: jnp.moveaxis(t, 2, 0)
    state0 = jnp.zeros((b, h, dk, dv), f32)
    _, out = lax.scan(step, state0, (mv(u), mv(w), mv(q_dec), mv(k_dec), mv(attn), mv(g_last)))
    return jnp.moveaxis(out, 0, 2).reshape(b, h, lc, dv)


def gated_deltanet_mixer(hn, w_in, conv_w, a_log, dt_bias, out_norm_w, w_out):
    bsz, seq_len, _ = hn.shape
    proj = hn @ w_in
    qkv, z, b_logit, a_logit = jnp.split(
        proj, [DN_CONV_CH, DN_CONV_CH + DN_VAL, DN_CONV_CH + DN_VAL + DN_HV], axis=-1)
    qkv = jax.nn.silu(causal_depthwise_conv(qkv, conv_w))
    q, k, v = jnp.split(qkv, [DN_KEY, 2 * DN_KEY], axis=-1)
    to_heads = lambda t, nh, hd: t.reshape(bsz, seq_len, nh, hd).transpose(0, 2, 1, 3)
    q = jnp.repeat(l2_norm(to_heads(q, DN_HK, DN_HEAD_K)), DN_HV // DN_HK, axis=1)
    k = jnp.repeat(l2_norm(to_heads(k, DN_HK, DN_HEAD_K)), DN_HV // DN_HK, axis=1)
    v = to_heads(v, DN_HV, DN_HEAD_V)
    beta = jax.nn.sigmoid(b_logit.astype(jnp.float32)).transpose(0, 2, 1)
    g = (-jnp.exp(a_log.astype(jnp.float32))
         * jax.nn.softplus(a_logit.astype(jnp.float32) + dt_bias.astype(jnp.float32))).transpose(0, 2, 1)
    pad = CHUNK - N_META
    p4 = lambda t: jnp.pad(t, ((0, 0), (0, 0), (pad, 0), (0, 0)))
    p3 = lambda t: jnp.pad(t, ((0, 0), (0, 0), (pad, 0)))
    o = chunk_gated_delta_rule(p4(q), p4(k), p4(v), p3(g), p3(beta))[:, :, pad:]
    o = rms_norm(o, out_norm_w) * jax.nn.silu(to_heads(z, DN_HV, DN_HEAD_V).astype(jnp.float32))
    o = o.transpose(0, 2, 1, 3).reshape(bsz, seq_len, DN_VAL).astype(hn.dtype)
    return o @ w_out


def stick_breaking_attention(q, k, v):
    lp, d = q.shape[2], q.shape[3]
    scale = d ** -0.5
    outs = []
    for i in range(lp // Q_BLOCK):
        t0, t1 = i * Q_BLOCK, (i + 1) * Q_BLOCK
        z = jnp.einsum("bhtd,bhsd->bhts", q[:, :, t0:t1], k[:, :, :t1]).astype(jnp.float32) * scale
        causal = jnp.arange(t1)[None, :] < jnp.arange(t0, t1)[:, None]
        log_keep = jnp.where(causal, jax.nn.log_sigmoid(-z), 0.0)
        tail = lax.cumsum(log_keep, axis=3, reverse=True) - log_keep
        a = jnp.where(causal, jnp.exp(jax.nn.log_sigmoid(z) + tail), 0.0)
        outs.append(jnp.einsum("bhts,bhsd->bhtd", a.astype(v.dtype), v[:, :, :t1]))
    return jnp.concatenate(outs, axis=2)


def stick_breaking_mixer(hn, w_in, q_norm_w, k_norm_w, w_out):
    bsz, seq_len, _ = hn.shape
    q, k, v, gate = jnp.split(hn @ w_in, 4, axis=-1)
    to_heads = lambda t: t.reshape(bsz, seq_len, SB_HEADS, SB_HEAD).transpose(0, 2, 1, 3)
    q = rms_norm(to_heads(q), q_norm_w)
    k = rms_norm(to_heads(k), k_norm_w)
    v = to_heads(v)
    lp = -(-seq_len // Q_BLOCK) * Q_BLOCK
    padr = lambda t: jnp.pad(t, ((0, 0), (0, 0), (0, lp - seq_len), (0, 0)))
    o = stick_breaking_attention(padr(q), padr(k), padr(v))[:, :, :seq_len]
    o = o.transpose(0, 2, 1, 3).reshape(bsz, seq_len, SB_WIDTH) * jax.nn.silu(gate)
    return o @ w_out


def _fwd_setup_inputs(seed: int = 0) -> dict:
    key = jax.random.key(seed)
    ks = jax.random.split(key, 16)
    f32 = jnp.float32
    nrm = lambda k, shape, scale: jax.random.normal(k, shape, f32) * scale
    dt = jnp.exp(jax.random.uniform(ks[6], (N_DELTA, DN_HV), f32, np.log(1e-3), np.log(1e-1)))
    return {
        "x": nrm(ks[0], (BATCH, SEQ, D_MODEL), 1.0),
        "meta_tokens": nrm(ks[1], (N_META, D_MODEL), 1.0),
        "dn_norm_w": 1.0 + nrm(ks[2], (N_DELTA, D_MODEL), 0.02),
        "dn_w_in": nrm(ks[3], (N_DELTA, D_MODEL, DN_IN), D_MODEL ** -0.5),
        "dn_conv_w": nrm(ks[4], (N_DELTA, CONV_K, DN_CONV_CH), CONV_K ** -0.5),
        "dn_a_log": jnp.log(jax.random.uniform(ks[5], (N_DELTA, DN_HV), f32, 1.0, 16.0)),
        "dn_dt_bias": dt + jnp.log(-jnp.expm1(-dt)),
        "dn_out_norm_w": 1.0 + nrm(ks[7], (N_DELTA, DN_HEAD_V), 0.02),
        "dn_w_out": nrm(ks[8], (N_DELTA, DN_VAL, D_MODEL), DN_VAL ** -0.5),
        "sb_norm_w": 1.0 + nrm(ks[9], (N_SB, D_MODEL), 0.02),
        "sb_w_in": nrm(ks[10], (N_SB, D_MODEL, 4 * SB_WIDTH), D_MODEL ** -0.5),
        "sb_q_norm_w": 1.0 + nrm(ks[11], (N_SB, SB_HEAD), 0.02),
        "sb_k_norm_w": 1.0 + nrm(ks[12], (N_SB, SB_HEAD), 0.02),
        "sb_w_out": nrm(ks[13], (N_SB, SB_WIDTH, D_MODEL), SB_WIDTH ** -0.5),
    }


def _fwd_reference(x, meta_tokens, dn_norm_w, dn_w_in, dn_conv_w, dn_a_log, dn_dt_bias,
              dn_out_norm_w, dn_w_out, sb_norm_w, sb_w_in, sb_q_norm_w, sb_k_norm_w, sb_w_out):
    bsz = x.shape[0]
    meta = jnp.broadcast_to(meta_tokens[None].astype(x.dtype), (bsz, N_META, D_MODEL))
    h = jnp.concatenate([meta, x], axis=1)
    for i in range(DEPTH):
        j = i // N_MIXERS
        if i % N_MIXERS == 0:
            h = h + gated_deltanet_mixer(rms_norm(h, dn_norm_w[j]), dn_w_in[j], dn_conv_w[j],
                                         dn_a_log[j], dn_dt_bias[j], dn_out_norm_w[j], dn_w_out[j])
        else:
            h = h + stick_breaking_mixer(rms_norm(h, sb_norm_w[j]), sb_w_in[j],
                                         sb_q_norm_w[j], sb_k_norm_w[j], sb_w_out[j])
    return h[:, N_META:]


import jax as _jax
import jax.numpy as _jnp

TWIN_FORMAT = 'train_step'
FWD_PARAMS = ['x', 'meta_tokens', 'dn_norm_w', 'dn_w_in', 'dn_conv_w', 'dn_a_log', 'dn_dt_bias', 'dn_out_norm_w', 'dn_w_out', 'sb_norm_w', 'sb_w_in', 'sb_q_norm_w', 'sb_k_norm_w', 'sb_w_out']
TWIN_WEIGHTS = ['meta_tokens', 'dn_norm_w', 'dn_w_in', 'dn_conv_w', 'dn_a_log', 'dn_dt_bias', 'dn_out_norm_w', 'dn_w_out', 'sb_norm_w', 'sb_w_in', 'sb_q_norm_w', 'sb_k_norm_w', 'sb_w_out']
TWIN_DIFF_INPUT = 'x'
TWIN_INPUTS = ['x', 'meta_tokens', 'dn_norm_w', 'dn_w_in', 'dn_conv_w', 'dn_a_log', 'dn_dt_bias', 'dn_out_norm_w', 'dn_w_out', 'sb_norm_w', 'sb_w_in', 'sb_q_norm_w', 'sb_k_norm_w', 'sb_w_out', 'loss_target', 'm_meta_tokens', 'm_dn_norm_w', 'm_dn_w_in', 'm_dn_conv_w', 'm_dn_a_log', 'm_dn_dt_bias', 'm_dn_out_norm_w', 'm_dn_w_out', 'm_sb_norm_w', 'm_sb_w_in', 'm_sb_q_norm_w', 'm_sb_k_norm_w', 'm_sb_w_out', 'v_meta_tokens', 'v_dn_norm_w', 'v_dn_w_in', 'v_dn_conv_w', 'v_dn_a_log', 'v_dn_dt_bias', 'v_dn_out_norm_w', 'v_dn_w_out', 'v_sb_norm_w', 'v_sb_w_in', 'v_sb_q_norm_w', 'v_sb_k_norm_w', 'v_sb_w_out']
TWIN_OUTPUTS = ['loss', 'grad_x', 'grad_meta_tokens', 'grad_dn_norm_w', 'grad_dn_w_in', 'grad_dn_conv_w', 'grad_dn_a_log', 'grad_dn_dt_bias', 'grad_dn_out_norm_w', 'grad_dn_w_out', 'grad_sb_norm_w', 'grad_sb_w_in', 'grad_sb_q_norm_w', 'grad_sb_k_norm_w', 'grad_sb_w_out', 'delta_meta_tokens', 'delta_dn_norm_w', 'delta_dn_w_in', 'delta_dn_conv_w', 'delta_dn_a_log', 'delta_dn_dt_bias', 'delta_dn_out_norm_w', 'delta_dn_w_out', 'delta_sb_norm_w', 'delta_sb_w_in', 'delta_sb_q_norm_w', 'delta_sb_k_norm_w', 'delta_sb_w_out', 'new_m_meta_tokens', 'new_m_dn_norm_w', 'new_m_dn_w_in', 'new_m_dn_conv_w', 'new_m_dn_a_log', 'new_m_dn_dt_bias', 'new_m_dn_out_norm_w', 'new_m_dn_w_out', 'new_m_sb_norm_w', 'new_m_sb_w_in', 'new_m_sb_q_norm_w', 'new_m_sb_k_norm_w', 'new_m_sb_w_out', 'new_v_meta_tokens', 'new_v_dn_norm_w', 'new_v_dn_w_in', 'new_v_dn_conv_w', 'new_v_dn_a_log', 'new_v_dn_dt_bias', 'new_v_dn_out_norm_w', 'new_v_dn_w_out', 'new_v_sb_norm_w', 'new_v_sb_w_in', 'new_v_sb_q_norm_w', 'new_v_sb_k_norm_w', 'new_v_sb_w_out']
TWIN_LEAF_KINDS = {'loss': 'loss', 'grad_x': 'grad_x', 'grad_meta_tokens': 'grad_w', 'grad_dn_norm_w': 'grad_w', 'grad_dn_w_in': 'grad_w', 'grad_dn_conv_w': 'grad_w', 'grad_dn_a_log': 'grad_w', 'grad_dn_dt_bias': 'grad_w', 'grad_dn_out_norm_w': 'grad_w', 'grad_dn_w_out': 'grad_w', 'grad_sb_norm_w': 'grad_w', 'grad_sb_w_in': 'grad_w', 'grad_sb_q_norm_w': 'grad_w', 'grad_sb_k_norm_w': 'grad_w', 'grad_sb_w_out': 'grad_w', 'delta_meta_tokens': 'delta_w', 'delta_dn_norm_w': 'delta_w', 'delta_dn_w_in': 'delta_w', 'delta_dn_conv_w': 'delta_w', 'delta_dn_a_log': 'delta_w', 'delta_dn_dt_bias': 'delta_w', 'delta_dn_out_norm_w': 'delta_w', 'delta_dn_w_out': 'delta_w', 'delta_sb_norm_w': 'delta_w', 'delta_sb_w_in': 'delta_w', 'delta_sb_q_norm_w': 'delta_w', 'delta_sb_k_norm_w': 'delta_w', 'delta_sb_w_out': 'delta_w', 'new_m_meta_tokens': 'new_m', 'new_m_dn_norm_w': 'new_m', 'new_m_dn_w_in': 'new_m', 'new_m_dn_conv_w': 'new_m', 'new_m_dn_a_log': 'new_m', 'new_m_dn_dt_bias': 'new_m', 'new_m_dn_out_norm_w': 'new_m', 'new_m_dn_w_out': 'new_m', 'new_m_sb_norm_w': 'new_m', 'new_m_sb_w_in': 'new_m', 'new_m_sb_q_norm_w': 'new_m', 'new_m_sb_k_norm_w': 'new_m', 'new_m_sb_w_out': 'new_m', 'new_v_meta_tokens': 'new_v', 'new_v_dn_norm_w': 'new_v', 'new_v_dn_w_in': 'new_v', 'new_v_dn_conv_w': 'new_v', 'new_v_dn_a_log': 'new_v', 'new_v_dn_dt_bias': 'new_v', 'new_v_dn_out_norm_w': 'new_v', 'new_v_dn_w_out': 'new_v', 'new_v_sb_norm_w': 'new_v', 'new_v_sb_w_in': 'new_v', 'new_v_sb_q_norm_w': 'new_v', 'new_v_sb_k_norm_w': 'new_v', 'new_v_sb_w_out': 'new_v'}


def _forward(args):
    return _fwd_reference(*[args[k] for k in FWD_PARAMS])


def _output_shape():
    out = _jax.eval_shape(lambda: _forward(_fwd_setup_inputs(0)))
    return out.shape, out.dtype

N_MICROBATCH = 1
ADAM_LR = 0.001
ADAM_B1 = 0.9
ADAM_B2 = 0.999
ADAM_EPS = 1e-08
ADAM_WD = 0.01
ADAM_STEP = 10
PER_EXAMPLE_BATCH_AXIS = {'x': 0, 'loss_target': 0}
SHARED_INPUTS = []
_WEIGHT_DTYPES = {'meta_tokens': _jnp.float32, 'dn_norm_w': _jnp.float32, 'dn_w_in': _jnp.float32, 'dn_conv_w': _jnp.float32, 'dn_a_log': _jnp.float32, 'dn_dt_bias': _jnp.float32, 'dn_out_norm_w': _jnp.float32, 'dn_w_out': _jnp.float32, 'sb_norm_w': _jnp.float32, 'sb_w_in': _jnp.float32, 'sb_q_norm_w': _jnp.float32, 'sb_k_norm_w': _jnp.float32, 'sb_w_out': _jnp.float32}
MOMENT_SCALE = {'meta_tokens': 3.297350e-03, 'dn_norm_w': 3.330975e+00, 'dn_w_in': 4.796522e-02, 'dn_conv_w': 7.113962e-02, 'dn_a_log': 4.639519e+00, 'dn_dt_bias': 4.455926e+00, 'dn_out_norm_w': 8.759694e+01, 'dn_w_out': 1.426113e-01, 'sb_norm_w': 2.608091e+00, 'sb_w_in': 4.683911e-02, 'sb_q_norm_w': 5.498666e+00, 'sb_k_norm_w': 5.479097e+00, 'sb_w_out': 3.205888e-02}


def _to_microbatches(a, axis):
    t = _jnp.moveaxis(a, axis, 0)
    t = t.reshape((N_MICROBATCH, t.shape[0] // N_MICROBATCH) + t.shape[1:])
    return _jnp.moveaxis(t, 1, axis + 1)


def setup_inputs(seed: int = 0) -> dict:
    inp = _fwd_setup_inputs(seed)
    key = _jax.random.fold_in(_jax.random.key(seed), 7919)
    shape, _ = _output_shape()
    out = dict(inp)
    out["loss_target"] = _jax.random.normal(_jax.random.fold_in(key, 0), shape, _jnp.float32)
    for i, name in enumerate(TWIN_WEIGHTS):
        w = inp[name].astype(_jnp.float32)
        if MOMENT_SCALE is None:
            s = _jnp.sqrt(_jnp.mean(_jnp.square(w)) + 1e-30)
        else:
            s = MOMENT_SCALE[name]
        km, kv = _jax.random.split(_jax.random.fold_in(key, i + 1))
        out[name] = w
        out["m_" + name] = s * _jax.random.normal(km, w.shape, _jnp.float32)
        out["v_" + name] = (s * s) * _jax.random.uniform(kv, w.shape, _jnp.float32, 0.5, 1.5)
    if N_MICROBATCH > 1:
        for name, axis in PER_EXAMPLE_BATCH_AXIS.items():
            out[name] = _to_microbatches(out[name], axis)
    return {'x': out['x'], 'meta_tokens': out['meta_tokens'], 'dn_norm_w': out['dn_norm_w'], 'dn_w_in': out['dn_w_in'], 'dn_conv_w': out['dn_conv_w'], 'dn_a_log': out['dn_a_log'], 'dn_dt_bias': out['dn_dt_bias'], 'dn_out_norm_w': out['dn_out_norm_w'], 'dn_w_out': out['dn_w_out'], 'sb_norm_w': out['sb_norm_w'], 'sb_w_in': out['sb_w_in'], 'sb_q_norm_w': out['sb_q_norm_w'], 'sb_k_norm_w': out['sb_k_norm_w'], 'sb_w_out': out['sb_w_out'], 'loss_target': out['loss_target'], 'm_meta_tokens': out['m_meta_tokens'], 'm_dn_norm_w': out['m_dn_norm_w'], 'm_dn_w_in': out['m_dn_w_in'], 'm_dn_conv_w': out['m_dn_conv_w'], 'm_dn_a_log': out['m_dn_a_log'], 'm_dn_dt_bias': out['m_dn_dt_bias'], 'm_dn_out_norm_w': out['m_dn_out_norm_w'], 'm_dn_w_out': out['m_dn_w_out'], 'm_sb_norm_w': out['m_sb_norm_w'], 'm_sb_w_in': out['m_sb_w_in'], 'm_sb_q_norm_w': out['m_sb_q_norm_w'], 'm_sb_k_norm_w': out['m_sb_k_norm_w'], 'm_sb_w_out': out['m_sb_w_out'], 'v_meta_tokens': out['v_meta_tokens'], 'v_dn_norm_w': out['v_dn_norm_w'], 'v_dn_w_in': out['v_dn_w_in'], 'v_dn_conv_w': out['v_dn_conv_w'], 'v_dn_a_log': out['v_dn_a_log'], 'v_dn_dt_bias': out['v_dn_dt_bias'], 'v_dn_out_norm_w': out['v_dn_out_norm_w'], 'v_dn_w_out': out['v_dn_w_out'], 'v_sb_norm_w': out['v_sb_norm_w'], 'v_sb_w_in': out['v_sb_w_in'], 'v_sb_q_norm_w': out['v_sb_q_norm_w'], 'v_sb_k_norm_w': out['v_sb_k_norm_w'], 'v_sb_w_out': out['v_sb_w_out']}


def _loss(weights, diff, rest, loss_target):
    with _jax.named_scope("forward"):
        args = {**rest, TWIN_DIFF_INPUT: diff, **{k: w.astype(_WEIGHT_DTYPES[k]) for k, w in weights.items()}}
        y = _forward(args)
    with _jax.named_scope("loss_head"):
        err = _jnp.square(y.astype(_jnp.float32) - loss_target)
        return 0.5 * _jnp.sum(_jnp.mean(err, axis=-1)) if err.ndim else 0.5 * err


def _adamw(w, g, m, v):
    m = ADAM_B1 * m + (1.0 - ADAM_B1) * g
    v = ADAM_B2 * v + (1.0 - ADAM_B2) * _jnp.square(g)
    m_hat = m / (1.0 - ADAM_B1 ** ADAM_STEP)
    v_hat = v / (1.0 - ADAM_B2 ** ADAM_STEP)
    delta = -ADAM_LR * (m_hat / (_jnp.sqrt(v_hat) + ADAM_EPS) + ADAM_WD * w)
    return delta, m, v


def reference(x, meta_tokens, dn_norm_w, dn_w_in, dn_conv_w, dn_a_log, dn_dt_bias, dn_out_norm_w, dn_w_out, sb_norm_w, sb_w_in, sb_q_norm_w, sb_k_norm_w, sb_w_out, loss_target, m_meta_tokens, m_dn_norm_w, m_dn_w_in, m_dn_conv_w, m_dn_a_log, m_dn_dt_bias, m_dn_out_norm_w, m_dn_w_out, m_sb_norm_w, m_sb_w_in, m_sb_q_norm_w, m_sb_k_norm_w, m_sb_w_out, v_meta_tokens, v_dn_norm_w, v_dn_w_in, v_dn_conv_w, v_dn_a_log, v_dn_dt_bias, v_dn_out_norm_w, v_dn_w_out, v_sb_norm_w, v_sb_w_in, v_sb_q_norm_w, v_sb_k_norm_w, v_sb_w_out):
    given = dict(x=x, meta_tokens=meta_tokens, dn_norm_w=dn_norm_w, dn_w_in=dn_w_in, dn_conv_w=dn_conv_w, dn_a_log=dn_a_log, dn_dt_bias=dn_dt_bias, dn_out_norm_w=dn_out_norm_w, dn_w_out=dn_w_out, sb_norm_w=sb_norm_w, sb_w_in=sb_w_in, sb_q_norm_w=sb_q_norm_w, sb_k_norm_w=sb_k_norm_w, sb_w_out=sb_w_out, loss_target=loss_target, m_meta_tokens=m_meta_tokens, m_dn_norm_w=m_dn_norm_w, m_dn_w_in=m_dn_w_in, m_dn_conv_w=m_dn_conv_w, m_dn_a_log=m_dn_a_log, m_dn_dt_bias=m_dn_dt_bias, m_dn_out_norm_w=m_dn_out_norm_w, m_dn_w_out=m_dn_w_out, m_sb_norm_w=m_sb_norm_w, m_sb_w_in=m_sb_w_in, m_sb_q_norm_w=m_sb_q_norm_w, m_sb_k_norm_w=m_sb_k_norm_w, m_sb_w_out=m_sb_w_out, v_meta_tokens=v_meta_tokens, v_dn_norm_w=v_dn_norm_w, v_dn_w_in=v_dn_w_in, v_dn_conv_w=v_dn_conv_w, v_dn_a_log=v_dn_a_log, v_dn_dt_bias=v_dn_dt_bias, v_dn_out_norm_w=v_dn_out_norm_w, v_dn_w_out=v_dn_w_out, v_sb_norm_w=v_sb_norm_w, v_sb_w_in=v_sb_w_in, v_sb_q_norm_w=v_sb_q_norm_w, v_sb_k_norm_w=v_sb_k_norm_w, v_sb_w_out=v_sb_w_out)
    weights = {n: given[n] for n in TWIN_WEIGHTS}
    shared = {n: given[n] for n in SHARED_INPUTS}
    per_example = {n: given[n] for n in ['x']}
    grad_fn = _jax.value_and_grad(_loss, argnums=(0, 1))

    def one_microbatch(ex, loss_target):
        ex = dict(ex)
        diff = ex.pop(TWIN_DIFF_INPUT)
        return grad_fn(weights, diff, {**shared, **ex}, loss_target)

    if N_MICROBATCH == 1:
        loss, (grad_w, grad_x) = one_microbatch(per_example, given["loss_target"])
    else:
        def body(carry, xs):
            loss_sum, grad_sum = carry
            l_k, (gw_k, gx_k) = one_microbatch(xs[0], xs[1])
            with _jax.named_scope("update"):
                return (loss_sum + l_k, _jax.tree.map(_jnp.add, grad_sum, gw_k)), gx_k

        init = (_jnp.zeros((), _jnp.float32), _jax.tree.map(_jnp.zeros_like, weights))
        (loss, grad_w), grad_x = _jax.lax.scan(body, init, (per_example, given["loss_target"]))
    with _jax.named_scope("update"):
        delta_w, new_m, new_v = {}, {}, {}
        for n in TWIN_WEIGHTS:
            delta_w[n], new_m[n], new_v[n] = _adamw(weights[n], grad_w[n], given["m_" + n], given["v_" + n])
    return (loss, grad_x, *[grad_w[n] for n in TWIN_WEIGHTS], *[delta_w[n] for n in TWIN_WEIGHTS],
            *[new_m[n] for n in TWIN_WEIGHTS], *[new_v[n] for n in TWIN_WEIGHTS])
```

```python
import functools

import jax
import jax.numpy as jnp
from jax import lax
from jax.experimental import pallas as pl
from jax.experimental.pallas import tpu as pltpu

F32 = jnp.float32
BF16 = jnp.bfloat16
HD = 128
CH = 64
QB = 128
N_META = 16
PAD = 128
INERT = PAD - N_META
NDEV = 8
CONV_K = 4
EPS = 1e-6
VMEM_LIMIT = 56 * 1024 * 1024

ADAM_LR, ADAM_B1, ADAM_B2, ADAM_EPS, ADAM_WD, ADAM_STEP = 0.001, 0.9, 0.999, 1e-08, 0.01, 10
MESH = pl.DeviceIdType.MESH


def _cp(*sem):
    return pltpu.CompilerParams(dimension_semantics=sem, vmem_limit_bytes=VMEM_LIMIT)


def _tile(n, pref, mult=8):
    if n <= pref:
        return n
    for t in range(pref - pref % mult, 0, -mult):
        if n % t == 0:
            return t
    return n


def _silu(x):
    return x * jax.nn.sigmoid(x)


def _dsilu(x):
    s = jax.nn.sigmoid(x)
    return s * (1.0 + x * (1.0 - s))


def _dot(a, b, dims=((1,), (0,))):
    return lax.dot_general(a.astype(BF16), b.astype(BF16), (dims, ((), ())), preferred_element_type=F32)


def _dot_nt(a, b):
    return _dot(a, b, ((1,), (1,)))


def _dot_tn(a, b):
    return _dot(a, b, ((0,), (0,)))


def _dot_f32(a, b):
    return lax.dot_general(a, b, (((1,), (0,)), ((), ())), preferred_element_type=F32,
                           precision=lax.Precision.HIGHEST)


def _dot_split(a, m):
    hi = a.astype(BF16)
    lo = (a - hi.astype(F32)).astype(BF16)
    dn = (((1,), (0,)), ((), ()))
    return (lax.dot_general(hi, m, dn, preferred_element_type=F32)
            + lax.dot_general(lo, m, dn, preferred_element_type=F32))


def _iota(shape, dim):
    return lax.broadcasted_iota(jnp.int32, shape, dim)


def _col_to_row(col):
    n = col.shape[0]
    eye = _iota((n, n), 0) == _iota((n, n), 1)
    return jnp.sum(jnp.where(eye, col, 0.0), axis=0, keepdims=True)


def _row_to_col(row):
    n = row.shape[1]
    eye = _iota((n, n), 0) == _iota((n, n), 1)
    return jnp.sum(jnp.where(eye, row, 0.0), axis=1, keepdims=True)


def _exchange(arrs, gather, name):
    n = len(arrs)

    def body(*refs):
        ins, outs = refs[:n], refs[n:2 * n]
        send_sems, recv_sems, local_sems = refs[2 * n:]
        x, y, c = lax.axis_index("x"), lax.axis_index("y"), lax.axis_index("c")
        me = 4 * x + 2 * y + c
        sends = []
        for i in range(n):
            mine = pltpu.make_async_copy(ins[i] if gather else ins[i].at[me], outs[i].at[me], local_sems.at[i])
            mine.start()
            sends.append(mine)
        for k in range(1, NDEV):
            px, py, pc = x ^ (k >> 2), y ^ ((k >> 1) & 1), c ^ (k & 1)
            peer = 4 * px + 2 * py + pc
            for i in range(n):
                cp = pltpu.make_async_remote_copy(
                    src_ref=ins[i] if gather else ins[i].at[peer], dst_ref=outs[i].at[me],
                    send_sem=send_sems.at[i * NDEV + k], recv_sem=recv_sems.at[i * NDEV + k],
                    device_id=(px, py, pc), device_id_type=MESH)
                cp.start()
                sends.append(cp)
        for k in range(1, NDEV):
            px, py, pc = x ^ (k >> 2), y ^ ((k >> 1) & 1), c ^ (k & 1)
            peer = 4 * px + 2 * py + pc
            for i in range(n):
                pltpu.make_async_remote_copy(
                    src_ref=outs[i].at[peer], dst_ref=outs[i].at[peer],
                    send_sem=send_sems.at[i * NDEV + k], recv_sem=recv_sems.at[i * NDEV + k],
                    device_id=(px, py, pc), device_id_type=MESH).wait_recv()
        for i in range(n):
            sends[i].wait()
        for cp in sends[n:]:
            cp.wait_send()

    hbm = pl.BlockSpec(memory_space=pltpu.HBM)
    out_shape = tuple(jax.ShapeDtypeStruct(((NDEV,) + a.shape) if gather else a.shape, a.dtype) for a in arrs)
    return pl.pallas_call(
        body, name=name, out_shape=out_shape, in_specs=[hbm] * n, out_specs=tuple([hbm] * n),
        scratch_shapes=[pltpu.SemaphoreType.DMA((n * NDEV,)), pltpu.SemaphoreType.DMA((n * NDEV,)),
                        pltpu.SemaphoreType.DMA((n,))],
        compiler_params=pltpu.CompilerParams(has_side_effects=True),
    )(*arrs)


def _matmul(a, b, *, mode, out_dtype, tm, tn, tk, name, add=None):
    if mode == "nn":
        (m, kd), (_, n) = a.shape, b.shape
    elif mode == "nt":
        (m, kd), (n, _) = a.shape, b.shape
    else:
        (kd, m), (_, n) = a.shape, b.shape
    tm, tn, tk = _tile(m, tm, 16), _tile(n, tn, 128), _tile(kd, tk, 128)
    nk = kd // tk
    a_spec = pl.BlockSpec((tk, tm), lambda i, j, k: (k, i)) if mode == "tn" else pl.BlockSpec((tm, tk), lambda i, j, k: (i, k))
    b_spec = pl.BlockSpec((tn, tk), lambda i, j, k: (j, k)) if mode == "nt" else pl.BlockSpec((tk, tn), lambda i, j, k: (k, j))
    o_spec = pl.BlockSpec((tm, tn), lambda i, j, k: (i, j))
    dims = {"nn": ((1,), (0,)), "nt": ((1,), (1,)), "tn": ((0,), (0,))}[mode]

    def body(*refs, nk):
        a_ref, b_ref = refs[0], refs[1]
        o_ref, acc_ref = refs[-2], refs[-1]
        k = pl.program_id(2)

        @pl.when(k == 0)
        def _():
            acc_ref[...] = jnp.zeros_like(acc_ref)

        acc_ref[...] += lax.dot_general(a_ref[...], b_ref[...], (dims, ((), ())), preferred_element_type=F32)

        @pl.when(k == nk - 1)
        def _():
            r = acc_ref[...]
            if add is not None:
                r = r + refs[2][...]
            o_ref[...] = r.astype(o_ref.dtype)

    ins, specs = [a, b], [a_spec, b_spec]
    if add is not None:
        ins.append(add)
        specs.append(o_spec)
    return pl.pallas_call(
        functools.partial(body, nk=nk), name=name, grid=(m // tm, n // tn, nk),
        in_specs=specs, out_specs=o_spec, out_shape=jax.ShapeDtypeStruct((m, n), out_dtype),
        scratch_shapes=[pltpu.VMEM((tm, tn), F32)], compiler_params=_cp("parallel", "parallel", "arbitrary"),
    )(*ins)


def _rms_fwd(h, w, name):
    lp, d = h.shape
    tm = _tile(lp, 384)

    def body(h_ref, w_ref, o_ref):
        xf = h_ref[...]
        r = lax.rsqrt(jnp.mean(xf * xf, axis=-1, keepdims=True) + EPS)
        o_ref[...] = (xf * r * w_ref[...]).astype(o_ref.dtype)

    return pl.pallas_call(
        body, name=name, grid=(lp // tm,),
        in_specs=[pl.BlockSpec((tm, d), lambda i: (i, 0)), pl.BlockSpec((1, d), lambda i: (0, 0))],
        out_specs=pl.BlockSpec((tm, d), lambda i: (i, 0)), out_shape=jax.ShapeDtypeStruct((lp, d), BF16),
        compiler_params=_cp("parallel"))(h, w)


def _rms_bwd(h, w, dhn, dres, name):
    lp, d = h.shape
    tm = _tile(lp, 192)

    def body(h_ref, w_ref, dy_ref, dres_ref, dh_ref, dhb_ref, dw_ref):
        xf = h_ref[...]
        r = lax.rsqrt(jnp.mean(xf * xf, axis=-1, keepdims=True) + EPS)
        xhat = xf * r
        dy = dy_ref[...]
        dxhat = dy * w_ref[...]
        dx = r * (dxhat - xhat * jnp.mean(dxhat * xhat, axis=-1, keepdims=True))
        dh = dres_ref[...] + dx
        dh_ref[...] = dh
        dhb_ref[...] = dh.astype(BF16)

        @pl.when(pl.program_id(0) == 0)
        def _():
            dw_ref[...] = jnp.zeros_like(dw_ref)

        dw_ref[...] += jnp.sum(dy * xhat, axis=0, keepdims=True)

    row = pl.BlockSpec((tm, d), lambda i: (i, 0))
    vec = pl.BlockSpec((1, d), lambda i: (0, 0))
    return pl.pallas_call(
        body, name=name, grid=(lp // tm,), in_specs=[row, vec, row, row], out_specs=(row, row, vec),
        out_shape=(jax.ShapeDtypeStruct((lp, d), F32), jax.ShapeDtypeStruct((lp, d), BF16),
                   jax.ShapeDtypeStruct((1, d), F32)),
        compiler_params=_cp("arbitrary"))(h, w, dhn, dres)


def _conv_pre(xx, w, rows, off):
    acc = None
    for j in range(CONV_K):
        sh = CONV_K - 1 - j
        term = (pltpu.roll(xx, sh, 0) if sh else xx)[off:off + rows] * w[j]
        acc = term if acc is None else acc + term
    return acc


def _conv_fwd(proj, conv_w, ncols, name):
    lp = proj.shape[0]
    tm, tc = _tile(lp, 384), _tile(ncols, 1024, 128)
    hb = tm // 8

    def body(x_ref, xb_ref, w_ref, o_ref):
        before = jnp.where(pl.program_id(0) > 0, xb_ref[...], 0.0)
        xx = jnp.concatenate([before, x_ref[...]], axis=0)
        o_ref[...] = _silu(_conv_pre(xx, [w_ref[j:j + 1, :] for j in range(CONV_K)], tm, 8))

    return pl.pallas_call(
        body, name=name, grid=(lp // tm, ncols // tc),
        in_specs=[pl.BlockSpec((tm, tc), lambda i, j: (i, j)),
                  pl.BlockSpec((8, tc), lambda i, j: (jnp.maximum(i * hb - 1, 0), j)),
                  pl.BlockSpec((CONV_K, tc), lambda i, j: (0, j))],
        out_specs=pl.BlockSpec((tm, tc), lambda i, j: (i, j)),
        out_shape=jax.ShapeDtypeStruct((lp, ncols), F32), compiler_params=_cp("parallel", "parallel"))(proj, proj, conv_w)


def _conv_bwd(proj, col0, conv_w, dact, name):
    lp, ncols = dact.shape
    tm, tc = _tile(lp, 384), _tile(ncols, 512, 128)
    hb, nt, cb0 = tm // 8, lp // tm, col0 // tc
    assert col0 % tc == 0

    def body(x_ref, xb_ref, xa_ref, d_ref, da_ref, w_ref, dx_ref, dw_ref):
        i = pl.program_id(1)
        before = jnp.where(i > 0, xb_ref[...], 0.0)
        last = i == nt - 1
        xx = jnp.concatenate([before, x_ref[...], jnp.where(last, 0.0, xa_ref[...])], axis=0)
        w = [w_ref[j:j + 1, :] for j in range(CONV_K)]
        pre = _conv_pre(xx, w, tm + 8, 8)
        dd = jnp.concatenate([d_ref[...], jnp.where(last, 0.0, da_ref[...])], axis=0)
        dpre = dd * _dsilu(pre)
        dx = None
        for j in range(CONV_K):
            sh = CONV_K - 1 - j
            term = (pltpu.roll(dpre, tm + 8 - sh, 0) if sh else dpre)[:tm] * w[j]
            dx = term if dx is None else dx + term
        dx_ref[...] = dx.astype(BF16)

        @pl.when(i == 0)
        def _():
            dw_ref[...] = jnp.zeros_like(dw_ref)

        for j in range(CONV_K):
            sh = CONV_K - 1 - j
            xs = (pltpu.roll(xx, sh, 0) if sh else xx)[8:8 + tm]
            dw_ref[j:j + 1, :] += jnp.sum(dpre[:tm] * xs, axis=0, keepdims=True)

    return pl.pallas_call(
        body, name=name, grid=(ncols // tc, nt),
        in_specs=[pl.BlockSpec((tm, tc), lambda j, i: (i, cb0 + j)),
                  pl.BlockSpec((8, tc), lambda j, i: (jnp.maximum(i * hb - 1, 0), cb0 + j)),
                  pl.BlockSpec((8, tc), lambda j, i: (jnp.minimum((i + 1) * hb, nt * hb - 1), cb0 + j)),
                  pl.BlockSpec((tm, tc), lambda j, i: (i, j)),
                  pl.BlockSpec((8, tc), lambda j, i: (jnp.minimum((i + 1) * hb, nt * hb - 1), j)),
                  pl.BlockSpec((CONV_K, tc), lambda j, i: (0, j))],
        out_specs=(pl.BlockSpec((tm, tc), lambda j, i: (i, j)), pl.BlockSpec((CONV_K, tc), lambda j, i: (0, j))),
        out_shape=(jax.ShapeDtypeStruct((lp, ncols), BF16), jax.ShapeDtypeStruct((CONV_K, ncols), F32)),
        compiler_params=_cp("parallel", "arbitrary"))(proj, proj, proj, dact, dact, conv_w)


def _softplus(x):
    return jnp.maximum(x, 0.0) + jnp.log(1.0 + jnp.exp(-jnp.abs(x)))


def _gates_fwd(gl, a_log2, dt_bias2, name):
    lp, w2 = gl.shape
    hv = w2 // 2

    def body(gl_ref, al_ref, dt_ref, o_ref):
        x = gl_ref[...]
        live = _iota((lp, 1), 0) >= INERT
        is_beta = _iota((1, w2), 1) < hv
        g = -jnp.exp(al_ref[...]) * _softplus(x + dt_ref[...])
        o_ref[...] = jnp.where(live, jnp.where(is_beta, jax.nn.sigmoid(x), g), 0.0)

    return pl.pallas_call(body, name=name, out_shape=jax.ShapeDtypeStruct((lp, w2), F32))(gl, a_log2, dt_bias2)


def _gates_bwd(gl, a_log2, dt_bias2, dbg, name):
    lp, w2 = gl.shape
    hv = w2 // 2

    def body(gl_ref, al_ref, dt_ref, d_ref, dl_ref, dal_ref, ddt_ref):
        x = gl_ref[...]
        live = _iota((lp, 1), 0) >= INERT
        is_beta = _iota((1, w2), 1) < hv
        d = jnp.where(live, d_ref[...], 0.0)
        beta = jax.nn.sigmoid(x)
        ea = jnp.exp(al_ref[...])
        u = x + dt_ref[...]
        dg = jnp.where(is_beta, 0.0, d)
        dal_ref[...] = jnp.sum(dg * (-ea) * _softplus(u), axis=0, keepdims=True)
        du = dg * (-ea) * jax.nn.sigmoid(u)
        ddt_ref[...] = jnp.sum(du, axis=0, keepdims=True)
        dl_ref[...] = jnp.where(is_beta, d * beta * (1.0 - beta), du).astype(BF16)

    vec = jax.ShapeDtypeStruct((1, w2), F32)
    return pl.pallas_call(
        body, name=name, out_shape=(jax.ShapeDtypeStruct((lp, w2), BF16), vec, vec))(gl, a_log2, dt_bias2, dbg)


def _l2n(x):
    r = lax.rsqrt(jnp.sum(x * x, axis=-1, keepdims=True) + EPS)
    return x * r, r


def _tri_inverse(a):
    eye = (_iota((CH, CH), 0) == _iota((CH, CH), 1)).astype(F32)
    t = eye - a
    p = _dot_f32(a, a)
    n = 2
    while n < CH:
        t = t + _dot_f32(t, p)
        n *= 2
        if n < CH:
            p = _dot_f32(p, p)
    return t


def _chunk_fwd(qn, kn, v, b_row, g_row, s):
    ri, ci = _iota((CH, CH), 0), _iota((CH, CH), 1)
    incl, strict = ri >= ci, ri > ci
    gam_col = jnp.sum(jnp.where(incl, g_row, 0.0), axis=1, keepdims=True)
    gam_row = _col_to_row(gam_col)
    b_col = _row_to_col(b_row)
    dec = jnp.exp(jnp.where(incl, gam_col - gam_row, -jnp.inf))
    eg = jnp.exp(gam_col)
    gl = jnp.sum(g_row, axis=1, keepdims=True)
    egl = jnp.exp(gl)
    ekd = jnp.exp(gl - gam_col)
    kb = kn * b_col
    vb = v * b_col
    a = jnp.where(strict, _dot_nt(kb, kn) * dec, 0.0)
    t = _tri_inverse(a)
    kbg = kb * eg
    u = _dot(t, vb)
    w = _dot(t, kbg)
    p = jnp.where(incl, _dot_nt(qn, kn) * dec, 0.0)
    qd = qn * eg
    kd = kn * ekd
    vn = u - _dot(w, s)
    o = _dot(qd, s) + _dot(p, vn)
    s_new = egl * s + _dot_tn(kd, vn)
    return dict(o=o, s_new=s_new, dec=dec, eg=eg, egl=egl, ekd=ekd, kb=kb, vb=vb, a=a, t=t, kbg=kbg, w=w, p=p,
                qd=qd, kd=kd, vn=vn, b_col=b_col, incl=incl, strict=strict)


def _delta_fwd(act, gates, key_w, name):
    lp = act.shape[0]
    hk, nc = key_w // HD, lp // CH
    hv = 2 * hk

    def body(q_ref, k_ref, v_ref, g_ref, o_ref, st_ref, s_scr):
        @pl.when(pl.program_id(1) == 0)
        def _():
            s_scr[...] = jnp.zeros_like(s_scr)

        qn = _l2n(q_ref[...])[0] * (HD ** -0.5)
        kn = _l2n(k_ref[...])[0]
        for e in range(2):
            s = s_scr[e]
            st_ref[0, e] = s
            r = _chunk_fwd(qn, kn, v_ref[:, e * HD:(e + 1) * HD], g_ref[0, 0, e:e + 1, :], g_ref[0, 0, 2 + e:3 + e, :], s)
            s_scr[e] = r["s_new"]
            o_ref[:, e * HD:(e + 1) * HD] = r["o"]

    return pl.pallas_call(
        body, name=name, grid=(hk, nc),
        in_specs=[pl.BlockSpec((CH, HD), lambda h, c: (c, h)),
                  pl.BlockSpec((CH, HD), lambda h, c: (c, hk + h)),
                  pl.BlockSpec((CH, 2 * HD), lambda h, c: (c, hk + h)),
                  pl.BlockSpec((1, 1, 8, CH), lambda h, c: (h, c, 0, 0))],
        out_specs=(pl.BlockSpec((CH, 2 * HD), lambda h, c: (c, h)),
                   pl.BlockSpec((1, 2, HD, HD), lambda h, c: (c, h, 0, 0))),
        out_shape=(jax.ShapeDtypeStruct((lp, hv * HD), F32), jax.ShapeDtypeStruct((nc, hv, HD, HD), F32)),
        scratch_shapes=[pltpu.VMEM((2, HD, HD), F32)],
        compiler_params=_cp("parallel", "arbitrary"))(act, act, act, gates)


def _delta_bwd(act, gates, states, do, key_w, name):
    lp = act.shape[0]
    hk, nc = key_w // HD, lp // CH
    hv = 2 * hk
    scale = HD ** -0.5

    def body(q_ref, k_ref, v_ref, g_ref, st_ref, do_ref, dq_ref, dk_ref, dv_ref, dg_ref, ds_scr):
        @pl.when(pl.program_id(1) == 0)
        def _():
            ds_scr[...] = jnp.zeros_like(ds_scr)

        q_act, k_act = q_ref[...], k_ref[...]
        qh, qr = _l2n(q_act)
        kn, kr = _l2n(k_act)
        qn = qh * scale
        ri, ci = _iota((CH, CH), 0), _iota((CH, CH), 1)
        dqn = jnp.zeros((CH, HD), F32)
        dkn = jnp.zeros((CH, HD), F32)
        dg_ref[0, 0, 4:8, :] = jnp.zeros((4, CH), F32)
        for e in range(2):
            s = st_ref[0, e]
            v = v_ref[:, e * HD:(e + 1) * HD]
            r = _chunk_fwd(qn, kn, v, g_ref[0, 0, e:e + 1, :], g_ref[0, 0, 2 + e:3 + e, :], s)
            d_o = do_ref[:, e * HD:(e + 1) * HD]
            ds_new = ds_scr[e]
            t, dec, eg, ekd, egl, b_col = r["t"], r["dec"], r["eg"], r["ekd"], r["egl"], r["b_col"]
            dvn = _dot_tn(r["p"], d_o) + _dot(r["kd"], ds_new)
            ds_scr[e] = _dot_tn(r["qd"], d_o) + egl * ds_new - _dot_tn(r["w"], dvn)
            dqd = _dot_nt(d_o, s)
            dp = jnp.where(r["incl"], _dot_nt(d_o, r["vn"]), 0.0)
            dkd = _dot_nt(r["vn"], ds_new)
            dw = -_dot_nt(dvn, s)
            dt = _dot_nt(dvn, r["vb"]) + _dot_nt(dw, r["kbg"])
            dvb = _dot_tn(t, dvn)
            dkbg = _dot_tn(t, dw)
            da = -jnp.where(r["strict"], _dot_nt(_dot_tn(t, dt), t), 0.0)
            gmat = da * r["a"] + dp * r["p"]
            mm, nn = da * dec, dp * dec
            dkb = _dot(mm, kn) + dkbg * eg
            dkn = dkn + _dot_tn(mm, r["kb"]) + _dot_tn(nn, qn) + dkd * ekd + b_col * dkb
            dqn = dqn + _dot(nn, kn) + dqd * eg
            inner = lambda x, z: jnp.sum(x * z, axis=1, keepdims=True)
            dkd_kd = inner(dkd, r["kd"])
            dgam = (jnp.sum(gmat, axis=1, keepdims=True) - _row_to_col(jnp.sum(gmat, axis=0, keepdims=True))
                    + inner(dqd, r["qd"]) + inner(dkbg, r["kbg"]) - dkd_kd)
            dgl = egl * jnp.sum(s * ds_new, keepdims=True) + jnp.sum(dkd_kd, keepdims=True)
            dgam = dgam + jnp.where(_iota((CH, 1), 0) == CH - 1, dgl, 0.0)
            dg_ref[0, 0, 2 + e:3 + e, :] = jnp.sum(jnp.where(ri >= ci, dgam, 0.0), axis=0, keepdims=True)
            dg_ref[0, 0, e:e + 1, :] = _col_to_row(inner(dkb, kn) + inner(dvb, v))
            dv_ref[:, e * HD:(e + 1) * HD] = b_col * dvb
        dqh = dqn * scale
        dq_ref[...] = qr * (dqh - qh * jnp.sum(dqh * qh, axis=1, keepdims=True))
        dk_ref[...] = kr * (dkn - kn * jnp.sum(dkn * kn, axis=1, keepdims=True))

    rev = lambda c: nc - 1 - c
    return pl.pallas_call(
        body, name=name, grid=(hk, nc),
        in_specs=[pl.BlockSpec((CH, HD), lambda h, c: (rev(c), h)),
                  pl.BlockSpec((CH, HD), lambda h, c: (rev(c), hk + h)),
                  pl.BlockSpec((CH, 2 * HD), lambda h, c: (rev(c), hk + h)),
                  pl.BlockSpec((1, 1, 8, CH), lambda h, c: (h, rev(c), 0, 0)),
                  pl.BlockSpec((1, 2, HD, HD), lambda h, c: (rev(c), h, 0, 0)),
                  pl.BlockSpec((CH, 2 * HD), lambda h, c: (rev(c), h))],
        out_specs=(pl.BlockSpec((CH, HD), lambda h, c: (rev(c), h)),
                   pl.BlockSpec((CH, HD), lambda h, c: (rev(c), h)),
                   pl.BlockSpec((CH, 2 * HD), lambda h, c: (rev(c), h)),
                   pl.BlockSpec((1, 1, 8, CH), lambda h, c: (h, rev(c), 0, 0))),
        out_shape=(jax.ShapeDtypeStruct((lp, key_w), F32), jax.ShapeDtypeStruct((lp, key_w), F32),
                   jax.ShapeDtypeStruct((lp, hv * HD), F32), jax.ShapeDtypeStruct((hk, nc, 8, CH), F32)),
        scratch_shapes=[pltpu.VMEM((2, HD, HD), F32)],
        compiler_params=_cp("parallel", "arbitrary"))(act, act, act, gates, states, do)


def _outnorm_fwd(o, proj, z_col0, w, name):
    lp, val = o.shape
    tm, zb = _tile(lp, 1056), z_col0 // HD

    def body(o_ref, z_ref, w_ref, y_ref):
        xf = o_ref[...]
        r = lax.rsqrt(jnp.mean(xf * xf, axis=-1, keepdims=True) + EPS)
        y_ref[...] = (xf * r * w_ref[...] * _silu(z_ref[...])).astype(BF16)

    return pl.pallas_call(
        body, name=name, grid=(lp // tm, val // HD),
        in_specs=[pl.BlockSpec((tm, HD), lambda i, h: (i, h)), pl.BlockSpec((tm, HD), lambda i, h: (i, zb + h)),
                  pl.BlockSpec((1, HD), lambda i, h: (0, 0))],
        out_specs=pl.BlockSpec((tm, HD), lambda i, h: (i, h)), out_shape=jax.ShapeDtypeStruct((lp, val), BF16),
        compiler_params=_cp("parallel", "parallel"))(o, proj, w)


def _outnorm_bwd(o, proj, z_col0, w, dy, name):
    lp, val = o.shape
    tm, zb = _tile(lp, 1056), z_col0 // HD

    def body(o_ref, z_ref, w_ref, dy_ref, do_ref, dz_ref, dw_ref):
        xf, z, d = o_ref[...], z_ref[...], dy_ref[...]
        r = lax.rsqrt(jnp.mean(xf * xf, axis=-1, keepdims=True) + EPS)
        xhat = xf * r
        dn = d * _silu(z)
        dz_ref[...] = (d * xhat * w_ref[...] * _dsilu(z)).astype(BF16)
        dxhat = dn * w_ref[...]
        do_ref[...] = r * (dxhat - xhat * jnp.mean(dxhat * xhat, axis=-1, keepdims=True))

        @pl.when((pl.program_id(0) == 0) & (pl.program_id(1) == 0))
        def _():
            dw_ref[...] = jnp.zeros_like(dw_ref)

        dw_ref[...] += jnp.sum(dn * xhat, axis=0, keepdims=True)

    blk = pl.BlockSpec((tm, HD), lambda i, h: (i, h))
    vec = pl.BlockSpec((1, HD), lambda i, h: (0, 0))
    return pl.pallas_call(
        body, name=name, grid=(lp // tm, val // HD),
        in_specs=[blk, pl.BlockSpec((tm, HD), lambda i, h: (i, zb + h)), vec, blk], out_specs=(blk, blk, vec),
        out_shape=(jax.ShapeDtypeStruct((lp, val), F32), jax.ShapeDtypeStruct((lp, val), BF16),
                   jax.ShapeDtypeStruct((1, HD), F32)),
        compiler_params=_cp("arbitrary", "arbitrary"))(o, proj, w, dy)


def _qknorm_fwd(proj, qw, kw, width, name):
    lp = proj.shape[0]
    tm, nh = _tile(lp, 1056), width // HD

    def body(q_ref, k_ref, v_ref, qw_ref, kw_ref, qo_ref, ko_ref, vo_ref):
        for x_ref, w_ref, o_ref in ((q_ref, qw_ref, qo_ref), (k_ref, kw_ref, ko_ref)):
            xf = x_ref[...]
            r = lax.rsqrt(jnp.mean(xf * xf, axis=-1, keepdims=True) + EPS)
            o_ref[...] = (xf * r * w_ref[...]).astype(BF16)
        vo_ref[...] = v_ref[...].astype(BF16)

    blk = lambda off: pl.BlockSpec((tm, HD), lambda i, h: (i, off + h))
    vec = pl.BlockSpec((1, HD), lambda i, h: (0, 0))
    shp = jax.ShapeDtypeStruct((lp, width), BF16)
    return pl.pallas_call(
        body, name=name, grid=(lp // tm, nh), in_specs=[blk(0), blk(nh), blk(2 * nh), vec, vec],
        out_specs=(blk(0), blk(0), blk(0)), out_shape=(shp, shp, shp),
        compiler_params=_cp("parallel", "parallel"))(proj, proj, proj, qw, kw)


def _qknorm_bwd(proj, qw, kw, dqn, dkn, width, name):
    lp = proj.shape[0]
    tm, nh = _tile(lp, 1056), width // HD

    def body(q_ref, k_ref, qw_ref, kw_ref, dqn_ref, dkn_ref, dq_ref, dk_ref, dqw_ref, dkw_ref):
        first = (pl.program_id(0) == 0) & (pl.program_id(1) == 0)
        for x_ref, w_ref, dy_ref, dx_ref, dw_ref in ((q_ref, qw_ref, dqn_ref, dq_ref, dqw_ref),
                                                       (k_ref, kw_ref, dkn_ref, dk_ref, dkw_ref)):
            xf, dy = x_ref[...], dy_ref[...]
            r = lax.rsqrt(jnp.mean(xf * xf, axis=-1, keepdims=True) + EPS)
            xhat = xf * r
            dxhat = dy * w_ref[...]
            dx_ref[...] = (r * (dxhat - xhat * jnp.mean(dxhat * xhat, axis=-1, keepdims=True))).astype(BF16)

            @pl.when(first)
            def _():
                dw_ref[...] = jnp.zeros_like(dw_ref)

            dw_ref[...] += jnp.sum(dy * xhat, axis=0, keepdims=True)

    blk = lambda off: pl.BlockSpec((tm, HD), lambda i, h: (i, off + h))
    vec = pl.BlockSpec((1, HD), lambda i, h: (0, 0))
    shp = jax.ShapeDtypeStruct((lp, width), BF16)
    vshp = jax.ShapeDtypeStruct((1, HD), F32)
    return pl.pallas_call(
        body, name=name, grid=(lp // tm, nh), in_specs=[blk(0), blk(nh), vec, vec, blk(0), blk(0)],
        out_specs=(blk(0), blk(0), vec, vec), out_shape=(shp, shp, vshp, vshp),
        compiler_params=_cp("arbitrary", "arbitrary"))(proj, proj, qw, kw, dqn, dkn)


def _sb_scores(q, k, t_idx, kb):
    z = _dot_nt(q, k) * (HD ** -0.5)
    s_idx = kb * QB + _iota((1, QB), 1)
    valid = (s_idx < t_idx) & (s_idx >= INERT)
    sp = jnp.log(1.0 + jnp.exp(-jnp.abs(z)))
    lsz = jnp.minimum(z, 0.0) - sp
    lk = jnp.where(valid, -jnp.maximum(z, 0.0) - sp, 0.0)
    return valid, lsz, lk


def _sb_fwd(qn, kn, vv, proj, gate_col0, name):
    lp, width = qn.shape
    nh, nq, gb = width // HD, lp // QB, gate_col0 // HD

    def body(q_ref, k_ref, v_ref, g_ref, o_ref, og_ref, tot_ref):
        qb = pl.program_id(1)
        q = q_ref[...]
        t_idx = qb * QB + _iota((QB, 1), 0)
        upper = (_iota((QB, QB), 0) > _iota((QB, QB), 1)).astype(BF16)

        def step(i, carry):
            run, acc = carry
            kb = qb - i
            rows = pl.ds(pl.multiple_of(kb * QB, QB), QB)
            valid, lsz, lk = _sb_scores(q, k_ref[rows, :], t_idx, kb)
            tail = _dot_split(lk, upper) + run
            a = jnp.where(valid, jnp.exp(lsz + tail), 0.0)
            acc = acc + _dot(a, v_ref[rows, :])
            return run + jnp.sum(lk, axis=1, keepdims=True), acc

        run, acc = lax.fori_loop(0, qb + 1, step, (jnp.zeros((QB, 1), F32), jnp.zeros((QB, HD), F32)))
        o_ref[...] = acc
        og_ref[...] = (acc * _silu(g_ref[...])).astype(BF16)
        tot_ref[0, 0] = _col_to_row(run)

    full = pl.BlockSpec((lp, HD), lambda h, i: (0, h))
    blk = pl.BlockSpec((QB, HD), lambda h, i: (i, h))
    return pl.pallas_call(
        body, name=name, grid=(nh, nq),
        in_specs=[blk, full, full, pl.BlockSpec((QB, HD), lambda h, i: (i, gb + h))],
        out_specs=(blk, blk, pl.BlockSpec((1, 1, 1, QB), lambda h, i: (h, i, 0, 0))),
        out_shape=(jax.ShapeDtypeStruct((lp, width), F32), jax.ShapeDtypeStruct((lp, width), BF16),
                   jax.ShapeDtypeStruct((nh, nq, 1, QB), F32)),
        compiler_params=_cp("parallel", "parallel"))(qn, kn, vv, proj)


def _sb_bwd(qn, kn, vv, proj, gate_col0, att, tot, dog, name):
    lp, width = qn.shape
    nh, nq, gb = width // HD, lp // QB, gate_col0 // HD
    scale = HD ** -0.5

    def body(q_ref, k_ref, v_ref, g_ref, att_ref, tot_ref, dog_ref, dq_ref, dk_ref, dv_ref, dg_ref, dk_acc, dv_acc):
        qb = pl.program_id(1)

        @pl.when(qb == 0)
        def _():
            dk_acc[...] = jnp.zeros_like(dk_acc)
            dv_acc[...] = jnp.zeros_like(dv_acc)

        q, gate, dg_out = q_ref[...], g_ref[...], dog_ref[...]
        d_o = dg_out * _silu(gate)
        dg_ref[...] = (dg_out * att_ref[...] * _dsilu(gate)).astype(BF16)
        total = _row_to_col(tot_ref[0, 0])
        t_idx = qb * QB + _iota((QB, 1), 0)
        ri, ci = _iota((QB, QB), 0), _iota((QB, QB), 1)
        lower_incl = (ri <= ci).astype(BF16)
        lower_excl = (ri < ci).astype(BF16)

        def step(kb, carry):
            run, erun, dq = carry
            rows = pl.ds(pl.multiple_of(kb * QB, QB), QB)
            k, v = k_ref[rows, :], v_ref[rows, :]
            valid, lsz, lk = _sb_scores(q, k, t_idx, kb)
            tail = total - run - _dot_split(lk, lower_incl)
            a = jnp.where(valid, jnp.exp(lsz + tail), 0.0)
            e = a * _dot_nt(d_o, v)
            dv_acc[rows, :] += _dot_tn(a, d_o)
            epre = erun + _dot_split(e, lower_excl)
            sig = jnp.exp(lsz)
            dz = jnp.where(valid, e * (1.0 - sig) - sig * epre, 0.0) * scale
            dk_acc[rows, :] += _dot_tn(dz, q)
            return (run + jnp.sum(lk, axis=1, keepdims=True), erun + jnp.sum(e, axis=1, keepdims=True),
                    dq + _dot(dz, k))

        zero = jnp.zeros((QB, 1), F32)
        _, _, dq = lax.fori_loop(0, qb + 1, step, (zero, zero, jnp.zeros((QB, HD), F32)))
        dq_ref[...] = dq

        @pl.when(qb == nq - 1)
        def _():
            dk_ref[...] = dk_acc[...]
            dv_ref[...] = dv_acc[...].astype(BF16)

    full = pl.BlockSpec((lp, HD), lambda h, i: (0, h))
    blk = pl.BlockSpec((QB, HD), lambda h, i: (i, h))
    return pl.pallas_call(
        body, name=name, grid=(nh, nq),
        in_specs=[blk, full, full, pl.BlockSpec((QB, HD), lambda h, i: (i, gb + h)), blk,
                  pl.BlockSpec((1, 1, 1, QB), lambda h, i: (h, i, 0, 0)), blk],
        out_specs=(blk, full, full, blk),
        out_shape=(jax.ShapeDtypeStruct((lp, width), F32), jax.ShapeDtypeStruct((lp, width), F32),
                   jax.ShapeDtypeStruct((lp, width), BF16), jax.ShapeDtypeStruct((lp, width), BF16)),
        scratch_shapes=[pltpu.VMEM((lp, HD), F32), pltpu.VMEM((lp, HD), F32)],
        compiler_params=_cp("parallel", "arbitrary"))(qn, kn, vv, proj, att, tot, dog)


def _loss_head(h, target, name):
    lp, d = h.shape
    tm = _tile(PAD, 128)
    nt = lp // tm
    npad = PAD // tm

    def body(h_ref, t_ref, l_ref, dh_ref, dhb_ref):
        i = pl.program_id(0)
        err = jnp.where(i >= npad, h_ref[...] - t_ref[...], 0.0)
        dh = err * (1.0 / d)
        dh_ref[...] = dh
        dhb_ref[...] = dh.astype(BF16)

        @pl.when(i == 0)
        def _():
            l_ref[...] = jnp.zeros_like(l_ref)

        l_ref[...] += (0.5 / d) * jnp.sum(err * err, keepdims=True)

    row = pl.BlockSpec((tm, d), lambda i: (i, 0))
    return pl.pallas_call(
        body, name=name, grid=(nt,),
        in_specs=[row, pl.BlockSpec((tm, d), lambda i: (jnp.maximum(i - npad, 0), 0))],
        out_specs=(pl.BlockSpec((1, 1), lambda i: (0, 0)), row, row),
        out_shape=(jax.ShapeDtypeStruct((1, 1), F32), jax.ShapeDtypeStruct((lp, d), F32),
                   jax.ShapeDtypeStruct((lp, d), BF16)),
        compiler_params=_cp("arbitrary"))(h, target)


def _adamw(parts, w, m, v, name):
    r, c = w.shape
    tr = _tile(r, max(8, (1 << 18) // c // 8 * 8))

    def body(p_ref, w_ref, m_ref, v_ref, g_ref, d_ref, mo_ref, vo_ref):
        g = p_ref[0].astype(F32)
        for k in range(1, NDEV):
            g = g + p_ref[k].astype(F32)
        mn = ADAM_B1 * m_ref[...] + (1.0 - ADAM_B1) * g
        vn = ADAM_B2 * v_ref[...] + (1.0 - ADAM_B2) * jnp.square(g)
        m_hat = mn / (1.0 - ADAM_B1 ** ADAM_STEP)
        v_hat = vn / (1.0 - ADAM_B2 ** ADAM_STEP)
        g_ref[...] = g
        d_ref[...] = -ADAM_LR * (m_hat / (jnp.sqrt(v_hat) + ADAM_EPS) + ADAM_WD * w_ref[...])
        mo_ref[...] = mn
        vo_ref[...] = vn

    blk = pl.BlockSpec((tr, c), lambda i: (i, 0))
    shp = jax.ShapeDtypeStruct((r, c), F32)
    return pl.pallas_call(
        body, name=name, grid=(r // tr,), in_specs=[pl.BlockSpec((NDEV, tr, c), lambda i: (0, i, 0)), blk, blk, blk],
        out_specs=(blk, blk, blk, blk), out_shape=(shp, shp, shp, shp), compiler_params=_cp("parallel"))(parts, w, m, v)


def _unshard_cols(g):
    return jnp.transpose(g, (1, 0, 2)).reshape(g.shape[1], NDEV * g.shape[2])


def _shard_cols(a):
    r, c = a.shape
    return jnp.transpose(a.reshape(r, NDEV, c // NDEV), (1, 0, 2))


def kernel(x, meta_tokens, dn_norm_w, dn_w_in, dn_conv_w, dn_a_log, dn_dt_bias, dn_out_norm_w, dn_w_out, sb_norm_w, sb_w_in, sb_q_norm_w, sb_k_norm_w, sb_w_out, loss_target, m_meta_tokens, m_dn_norm_w, m_dn_w_in, m_dn_conv_w, m_dn_a_log, m_dn_dt_bias, m_dn_out_norm_w, m_dn_w_out, m_sb_norm_w, m_sb_w_in, m_sb_q_norm_w, m_sb_k_norm_w, m_sb_w_out, v_meta_tokens, v_dn_norm_w, v_dn_w_in, v_dn_conv_w, v_dn_a_log, v_dn_dt_bias, v_dn_out_norm_w, v_dn_w_out, v_sb_norm_w, v_sb_w_in, v_sb_q_norm_w, v_sb_k_norm_w, v_sb_w_out):
    seq, d = x.shape[1], x.shape[2]
    lp = PAD + seq
    key_w = d
    val_w = 2 * d
    hv = val_w // HD
    conv_w_cols = 2 * key_w + val_w
    main_w = conv_w_cols + val_w
    sb_w = d
    nc = lp // CH
    hk = key_w // HD

    (g_in0, g_out0, g_in1, g_out1) = _exchange(
        [dn_w_in[0].astype(BF16), dn_w_out[0].astype(BF16), sb_w_in[0].astype(BF16), sb_w_out[0].astype(BF16)],
        True, "gather_weights")
    (g_meta, g_sbn, g_conv) = _exchange([meta_tokens, sb_norm_w, dn_conv_w[0]], True, "gather_vectors")
    w_in0 = _unshard_cols(g_in0)
    w_in0_main, w_in0_gate = w_in0[:, :main_w], w_in0[:, main_w:]
    w_out0 = g_out0.reshape(val_w, d)
    w_in1 = _unshard_cols(g_in1)
    w_out1 = g_out1.reshape(sb_w, d)
    meta = _unshard_cols(g_meta)
    sbn_w = _unshard_cols(g_sbn)
    conv_w = _unshard_cols(g_conv)

    h0 = jnp.concatenate([jnp.zeros((INERT, d), F32), meta, x[0]], axis=0)
    hn0 = _rms_fwd(h0, dn_norm_w, "dn_norm")
    proj0 = _matmul(hn0, w_in0_main, mode="nn", out_dtype=F32, tm=1056, tn=512, tk=4096, name="dn_in_proj")
    gl0 = _matmul(hn0, w_in0_gate, mode="nn", out_dtype=F32, tm=1056, tn=512, tk=4096, name="dn_gate_proj")
    act0 = _conv_fwd(proj0, conv_w, conv_w_cols, "dn_conv")
    a_log2 = jnp.concatenate([jnp.zeros_like(dn_a_log), dn_a_log], axis=1)
    dt_bias2 = jnp.concatenate([jnp.zeros_like(dn_dt_bias), dn_dt_bias], axis=1)
    bg = _gates_fwd(gl0, a_log2, dt_bias2, "dn_gates")
    pack = lambda t: jnp.transpose(t.reshape(nc, CH, hk, 2), (2, 0, 3, 1))
    gates = jnp.concatenate([pack(bg[:, :hv]), pack(bg[:, hv:]), jnp.zeros((hk, nc, 4, CH), F32)], axis=2)
    o0, states = _delta_fwd(act0, gates, key_w, "dn_delta")
    o0g = _outnorm_fwd(o0, proj0, conv_w_cols, dn_out_norm_w, "dn_out_norm")
    h1 = _matmul(o0g, w_out0, mode="nn", out_dtype=F32, tm=1056, tn=512, tk=4096, name="dn_out_proj", add=h0)
    hn1 = _rms_fwd(h1, sbn_w, "sb_norm")
    proj1 = _matmul(hn1, w_in1, mode="nn", out_dtype=F32, tm=1056, tn=512, tk=4096, name="sb_in_proj")
    qn1, kn1, vv1 = _qknorm_fwd(proj1, sb_q_norm_w, sb_k_norm_w, sb_w, "sb_qk_norm")
    att1, o1g, tot1 = _sb_fwd(qn1, kn1, vv1, proj1, 3 * sb_w, "sb_attn")
    h2 = _matmul(o1g, w_out1, mode="nn", out_dtype=F32, tm=1056, tn=512, tk=4096, name="sb_out_proj", add=h1)
    loss_part, dh2, dh2b = _loss_head(h2, loss_target[0], "loss_head")
    loss = lax.psum(loss_part[0, 0], ("x", "y", "c"))

    p_out1 = _matmul(o1g, dh2b, mode="tn", out_dtype=BF16, tm=1024, tn=512, tk=lp, name="sb_out_wgrad")
    do1g = _matmul(dh2b, w_out1, mode="nt", out_dtype=F32, tm=1056, tn=512, tk=4096, name="sb_out_dgrad")
    dqn1, dkn1, dv1, dgate1 = _sb_bwd(qn1, kn1, vv1, proj1, 3 * sb_w, att1, tot1, do1g, "sb_attn_bwd")
    dq1, dk1, d_qw, d_kw = _qknorm_bwd(proj1, sb_q_norm_w, sb_k_norm_w, dqn1, dkn1, sb_w, "sb_qk_norm_bwd")
    dproj1 = jnp.concatenate([dq1, dk1, dv1, dgate1], axis=1)
    p_in1 = _matmul(hn1, dproj1, mode="tn", out_dtype=BF16, tm=1024, tn=512, tk=lp, name="sb_in_wgrad")
    dhn1 = _matmul(dproj1, w_in1, mode="nt", out_dtype=F32, tm=1056, tn=512, tk=4096, name="sb_in_dgrad")
    dh1, dh1b, d_sbn = _rms_bwd(h1, sbn_w, dhn1, dh2, "sb_norm_bwd")

    p_out0 = _matmul(o0g, dh1b, mode="tn", out_dtype=BF16, tm=1024, tn=512, tk=lp, name="dn_out_wgrad")
    do0g = _matmul(dh1b, w_out0, mode="nt", out_dtype=F32, tm=1056, tn=512, tk=4096, name="dn_out_dgrad")
    do0, dz0, d_onw = _outnorm_bwd(o0, proj0, conv_w_cols, dn_out_norm_w, do0g, "dn_out_norm_bwd")
    dq_act, dk_act, dv_act, dgates = _delta_bwd(act0, gates, states, do0, key_w, "dn_delta_bwd")
    unpack = lambda t: jnp.transpose(t, (1, 3, 0, 2)).reshape(lp, hv)
    dbg = jnp.concatenate([unpack(dgates[:, :, 0:2]), unpack(dgates[:, :, 2:4])], axis=1)
    dgl0, d_alog2, d_dtb2 = _gates_bwd(gl0, a_log2, dt_bias2, dbg, "dn_gates_bwd")
    d_alog, d_dtb = d_alog2[:, hv:], d_dtb2[:, hv:]
    dxq, dcw_q = _conv_bwd(proj0, 0, conv_w[:, :key_w], dq_act, "dn_conv_bwd_q")
    dxk, dcw_k = _conv_bwd(proj0, key_w, conv_w[:, key_w:2 * key_w], dk_act, "dn_conv_bwd_k")
    dxv, dcw_v = _conv_bwd(proj0, 2 * key_w, conv_w[:, 2 * key_w:], dv_act, "dn_conv_bwd_v")
    dproj0 = jnp.concatenate([dxq, dxk, dxv, dz0], axis=1)
    p_in0_main = _matmul(hn0, dproj0, mode="tn", out_dtype=BF16, tm=1024, tn=512, tk=lp, name="dn_in_wgrad")
    p_in0_gate = _matmul(hn0, dgl0, mode="tn", out_dtype=BF16, tm=1024, tn=512, tk=lp, name="dn_gate_wgrad")
    dhn0 = _matmul(dgl0, w_in0_gate, mode="nt", out_dtype=F32, tm=1056, tn=512, tk=4096, name="dn_gate_dgrad")
    dhn0 = _matmul(dproj0, w_in0_main, mode="nt", out_dtype=F32, tm=1056, tn=512, tk=4096, name="dn_in_dgrad", add=dhn0)
    dh0, _, d_dnn = _rms_bwd(h0, dn_norm_w, dhn0, dh1, "dn_norm_bwd")
    grad_x = dh0[PAD:][None]

    p_in0 = _shard_cols(jnp.concatenate([p_in0_main, p_in0_gate], axis=1))
    (r_in0, r_out0, r_in1, r_out1) = _exchange(
        [p_in0, p_out0.reshape(NDEV, val_w // NDEV, d), _shard_cols(p_in1), p_out1.reshape(NDEV, sb_w // NDEV, d)],
        False, "scatter_weight_grads")
    p_conv = _shard_cols(jnp.concatenate([dcw_q, dcw_k, dcw_v], axis=1))
    (r_meta, r_sbn, r_conv) = _exchange([_shard_cols(dh0[INERT:PAD]), _shard_cols(d_sbn), p_conv], False, "scatter_vector_grads")
    small = jnp.concatenate([d_dnn, d_alog, d_dtb, d_onw, d_qw, d_kw], axis=1)
    (r_small,) = _exchange([small], True, "gather_replicated_grads")

    outs = {}
    outs["meta_tokens"] = _adamw(r_meta, meta_tokens, m_meta_tokens, v_meta_tokens, "adamw_meta")
    outs["dn_w_in"] = _adamw(r_in0, dn_w_in[0], m_dn_w_in[0], v_dn_w_in[0], "adamw_dn_w_in")
    outs["dn_conv_w"] = _adamw(r_conv, dn_conv_w[0], m_dn_conv_w[0], v_dn_conv_w[0], "adamw_dn_conv")
    outs["dn_w_out"] = _adamw(r_out0, dn_w_out[0], m_dn_w_out[0], v_dn_w_out[0], "adamw_dn_w_out")
    outs["sb_norm_w"] = _adamw(r_sbn, sb_norm_w, m_sb_norm_w, v_sb_norm_w, "adamw_sb_norm")
    outs["sb_w_in"] = _adamw(r_in1, sb_w_in[0], m_sb_w_in[0], v_sb_w_in[0], "adamw_sb_w_in")
    outs["sb_w_out"] = _adamw(r_out1, sb_w_out[0], m_sb_w_out[0], v_sb_w_out[0], "adamw_sb_w_out")
    cat = lambda *a: jnp.concatenate(a, axis=1)
    rep = _adamw(r_small, cat(dn_norm_w, dn_a_log, dn_dt_bias, dn_out_norm_w, sb_q_norm_w, sb_k_norm_w),
                 cat(m_dn_norm_w, m_dn_a_log, m_dn_dt_bias, m_dn_out_norm_w, m_sb_q_norm_w, m_sb_k_norm_w),
                 cat(v_dn_norm_w, v_dn_a_log, v_dn_dt_bias, v_dn_out_norm_w, v_sb_q_norm_w, v_sb_k_norm_w),
                 "adamw_replicated")
    off = 0
    for nm, wd in (("dn_norm_w", d), ("dn_a_log", hv), ("dn_dt_bias", hv), ("dn_out_norm_w", HD),
                   ("sb_q_norm_w", HD), ("sb_k_norm_w", HD)):
        outs[nm] = tuple(t[:, off:off + wd] for t in rep)
        off += wd
    lead = ("dn_w_in", "dn_conv_w", "dn_w_out", "sb_w_in", "sb_w_out")
    order = ("meta_tokens", "dn_norm_w", "dn_w_in", "dn_conv_w", "dn_a_log", "dn_dt_bias", "dn_out_norm_w", "dn_w_out",
             "sb_norm_w", "sb_w_in", "sb_q_norm_w", "sb_k_norm_w", "sb_w_out")
    fix = lambda nm, t: t[None] if nm in lead else t
    result = [loss, grad_x]
    for kind in range(4):
        result += [fix(nm, outs[nm][kind]) for nm in order]
    return tuple(result)
```

```python
import functools

import jax
import jax.numpy as jnp
from jax import lax
from jax.experimental import pallas as pl
from jax.experimental.pallas import tpu as pltpu

F32 = jnp.float32
BF16 = jnp.bfloat16
HD = 128
CH = 64
QB = 128
N_META = 16
PAD = 128
INERT = PAD - N_META
NDEV = 8
CONV_K = 4
EPS = 1e-6
VMEM_LIMIT = 56 * 1024 * 1024

ADAM_LR, ADAM_B1, ADAM_B2, ADAM_EPS, ADAM_WD, ADAM_STEP = 0.001, 0.9, 0.999, 1e-08, 0.01, 10
MESH = pl.DeviceIdType.MESH


def _cp(*sem):
    return pltpu.CompilerParams(dimension_semantics=sem, vmem_limit_bytes=VMEM_LIMIT)


def _tile(n, pref, mult=8):
    if n <= pref:
        return n
    for t in range(pref - pref % mult, 0, -mult):
        if n % t == 0:
            return t
    return n


def _silu(x):
    return x * jax.nn.sigmoid(x)


def _dsilu(x):
    s = jax.nn.sigmoid(x)
    return s * (1.0 + x * (1.0 - s))


def _dot(a, b, dims=((1,), (0,))):
    return lax.dot_general(a.astype(BF16), b.astype(BF16), (dims, ((), ())), preferred_element_type=F32)


def _dot_nt(a, b):
    return _dot(a, b, ((1,), (1,)))


def _dot_tn(a, b):
    return _dot(a, b, ((0,), (0,)))


def _dot_f32(a, b):
    dn = (((1,), (0,)), ((), ()))
    ah, bh = a.astype(BF16), b.astype(BF16)
    al, bl = (a - ah.astype(F32)).astype(BF16), (b - bh.astype(F32)).astype(BF16)
    mm = lambda x, y: lax.dot_general(x, y, dn, preferred_element_type=F32)
    return mm(ah, bh) + (mm(ah, bl) + mm(al, bh))


def _dot_split(a, m):
    hi = a.astype(BF16)
    lo = (a - hi.astype(F32)).astype(BF16)
    dn = (((1,), (0,)), ((), ()))
    return (lax.dot_general(hi, m, dn, preferred_element_type=F32)
            + lax.dot_general(lo, m, dn, preferred_element_type=F32))


def _iota(shape, dim):
    return lax.broadcasted_iota(jnp.int32, shape, dim)


def _col_to_row(col):
    n = col.shape[0]
    eye = _iota((n, n), 0) == _iota((n, n), 1)
    return jnp.sum(jnp.where(eye, col, 0.0), axis=0, keepdims=True)


def _row_to_col(row):
    n = row.shape[1]
    eye = _iota((n, n), 0) == _iota((n, n), 1)
    return jnp.sum(jnp.where(eye, row, 0.0), axis=1, keepdims=True)


def _exchange(arrs, gather, name):
    n = len(arrs)

    def body(*refs):
        ins, outs = refs[:n], refs[n:2 * n]
        send_sems, recv_sems, local_sems = refs[2 * n:]
        x, y, c = lax.axis_index("x"), lax.axis_index("y"), lax.axis_index("c")
        me = 4 * x + 2 * y + c
        sends = []
        for i in range(n):
            mine = pltpu.make_async_copy(ins[i] if gather else ins[i].at[me], outs[i].at[me], local_sems.at[i])
            mine.start()
            sends.append(mine)
        for k in range(1, NDEV):
            px, py, pc = x ^ (k >> 2), y ^ ((k >> 1) & 1), c ^ (k & 1)
            peer = 4 * px + 2 * py + pc
            for i in range(n):
                cp = pltpu.make_async_remote_copy(
                    src_ref=ins[i] if gather else ins[i].at[peer], dst_ref=outs[i].at[me],
                    send_sem=send_sems.at[i * NDEV + k], recv_sem=recv_sems.at[i * NDEV + k],
                    device_id=(px, py, pc), device_id_type=MESH)
                cp.start()
                sends.append(cp)
        for k in range(1, NDEV):
            px, py, pc = x ^ (k >> 2), y ^ ((k >> 1) & 1), c ^ (k & 1)
            peer = 4 * px + 2 * py + pc
            for i in range(n):
                pltpu.make_async_remote_copy(
                    src_ref=outs[i].at[peer], dst_ref=outs[i].at[peer],
                    send_sem=send_sems.at[i * NDEV + k], recv_sem=recv_sems.at[i * NDEV + k],
                    device_id=(px, py, pc), device_id_type=MESH).wait_recv()
        for i in range(n):
            sends[i].wait()
        for cp in sends[n:]:
            cp.wait_send()

    hbm = pl.BlockSpec(memory_space=pltpu.HBM)
    out_shape = tuple(jax.ShapeDtypeStruct(((NDEV,) + a.shape) if gather else a.shape, a.dtype) for a in arrs)
    return pl.pallas_call(
        body, name=name, out_shape=out_shape, in_specs=[hbm] * n, out_specs=tuple([hbm] * n),
        scratch_shapes=[pltpu.SemaphoreType.DMA((n * NDEV,)), pltpu.SemaphoreType.DMA((n * NDEV,)),
                        pltpu.SemaphoreType.DMA((n,))],
        compiler_params=pltpu.CompilerParams(has_side_effects=True),
    )(*arrs)


def _matmul(a, b, *, mode, out_dtype, tm, tn, tk, name, add=None):
    if mode == "nn":
        (m, kd), (_, n) = a.shape, b.shape
    elif mode == "nt":
        (m, kd), (n, _) = a.shape, b.shape
    else:
        (kd, m), (_, n) = a.shape, b.shape
    tm, tn, tk = _tile(m, tm, 16), _tile(n, tn, 128), _tile(kd, tk, 128)
    nk = kd // tk
    a_spec = pl.BlockSpec((tk, tm), lambda i, j, k: (k, i)) if mode == "tn" else pl.BlockSpec((tm, tk), lambda i, j, k: (i, k))
    b_spec = pl.BlockSpec((tn, tk), lambda i, j, k: (j, k)) if mode == "nt" else pl.BlockSpec((tk, tn), lambda i, j, k: (k, j))
    o_spec = pl.BlockSpec((tm, tn), lambda i, j, k: (i, j))
    dims = {"nn": ((1,), (0,)), "nt": ((1,), (1,)), "tn": ((0,), (0,))}[mode]

    def body(*refs, nk):
        a_ref, b_ref = refs[0], refs[1]
        o_ref, acc_ref = refs[-2], refs[-1]
        k = pl.program_id(2)

        @pl.when(k == 0)
        def _():
            acc_ref[...] = jnp.zeros_like(acc_ref)

        acc_ref[...] += lax.dot_general(a_ref[...], b_ref[...], (dims, ((), ())), preferred_element_type=F32)

        @pl.when(k == nk - 1)
        def _():
            r = acc_ref[...]
            if add is not None:
                r = r + refs[2][...]
            o_ref[...] = r.astype(o_ref.dtype)

    ins, specs = [a, b], [a_spec, b_spec]
    if add is not None:
        ins.append(add)
        specs.append(o_spec)
    return pl.pallas_call(
        functools.partial(body, nk=nk), name=name, grid=(m // tm, n // tn, nk),
        in_specs=specs, out_specs=o_spec, out_shape=jax.ShapeDtypeStruct((m, n), out_dtype),
        scratch_shapes=[pltpu.VMEM((tm, tn), F32)], compiler_params=_cp("parallel", "parallel", "arbitrary"),
    )(*ins)


def _rms_fwd(h, w, name):
    lp, d = h.shape
    tm = _tile(lp, 384)

    def body(h_ref, w_ref, o_ref):
        xf = h_ref[...]
        r = lax.rsqrt(jnp.mean(xf * xf, axis=-1, keepdims=True) + EPS)
        o_ref[...] = (xf * r * w_ref[...]).astype(o_ref.dtype)

    return pl.pallas_call(
        body, name=name, grid=(lp // tm,),
        in_specs=[pl.BlockSpec((tm, d), lambda i: (i, 0)), pl.BlockSpec((1, d), lambda i: (0, 0))],
        out_specs=pl.BlockSpec((tm, d), lambda i: (i, 0)), out_shape=jax.ShapeDtypeStruct((lp, d), BF16),
        compiler_params=_cp("parallel"))(h, w)


def _rms_bwd(h, w, dhn, dres, name):
    lp, d = h.shape
    tm = _tile(lp, 192)

    def body(h_ref, w_ref, dy_ref, dres_ref, dh_ref, dhb_ref, dw_ref):
        xf = h_ref[...]
        r = lax.rsqrt(jnp.mean(xf * xf, axis=-1, keepdims=True) + EPS)
        xhat = xf * r
        dy = dy_ref[...]
        dxhat = dy * w_ref[...]
        dx = r * (dxhat - xhat * jnp.mean(dxhat * xhat, axis=-1, keepdims=True))
        dh = dres_ref[...] + dx
        dh_ref[...] = dh
        dhb_ref[...] = dh.astype(BF16)

        @pl.when(pl.program_id(0) == 0)
        def _():
            dw_ref[...] = jnp.zeros_like(dw_ref)

        dw_ref[...] += jnp.sum(dy * xhat, axis=0, keepdims=True)

    row = pl.BlockSpec((tm, d), lambda i: (i, 0))
    vec = pl.BlockSpec((1, d), lambda i: (0, 0))
    return pl.pallas_call(
        body, name=name, grid=(lp // tm,), in_specs=[row, vec, row, row], out_specs=(row, row, vec),
        out_shape=(jax.ShapeDtypeStruct((lp, d), F32), jax.ShapeDtypeStruct((lp, d), BF16),
                   jax.ShapeDtypeStruct((1, d), F32)),
        compiler_params=_cp("arbitrary"))(h, w, dhn, dres)


def _conv_pre(xx, w, rows, off):
    acc = None
    for j in range(CONV_K):
        sh = CONV_K - 1 - j
        term = (pltpu.roll(xx, sh, 0) if sh else xx)[off:off + rows] * w[j]
        acc = term if acc is None else acc + term
    return acc


def _conv_fwd(proj, conv_w, ncols, name):
    lp = proj.shape[0]
    tm, tc = _tile(lp, 384), _tile(ncols, 1024, 128)
    hb = tm // 8

    def body(x_ref, xb_ref, w_ref, o_ref):
        before = jnp.where(pl.program_id(0) > 0, xb_ref[...], 0.0)
        xx = jnp.concatenate([before, x_ref[...]], axis=0)
        o_ref[...] = _silu(_conv_pre(xx, [w_ref[j:j + 1, :] for j in range(CONV_K)], tm, 8))

    return pl.pallas_call(
        body, name=name, grid=(lp // tm, ncols // tc),
        in_specs=[pl.BlockSpec((tm, tc), lambda i, j: (i, j)),
                  pl.BlockSpec((8, tc), lambda i, j: (jnp.maximum(i * hb - 1, 0), j)),
                  pl.BlockSpec((CONV_K, tc), lambda i, j: (0, j))],
        out_specs=pl.BlockSpec((tm, tc), lambda i, j: (i, j)),
        out_shape=jax.ShapeDtypeStruct((lp, ncols), F32), compiler_params=_cp("parallel", "parallel"))(proj, proj, conv_w)


def _conv_bwd(proj, col0, conv_w, dact, name):
    lp, ncols = dact.shape
    tm, tc = _tile(lp, 384), _tile(ncols, 512, 128)
    hb, nt, cb0 = tm // 8, lp // tm, col0 // tc
    assert col0 % tc == 0

    def body(x_ref, xb_ref, xa_ref, d_ref, da_ref, w_ref, dx_ref, dw_ref):
        i = pl.program_id(1)
        before = jnp.where(i > 0, xb_ref[...], 0.0)
        last = i == nt - 1
        xx = jnp.concatenate([before, x_ref[...], jnp.where(last, 0.0, xa_ref[...])], axis=0)
        w = [w_ref[j:j + 1, :] for j in range(CONV_K)]
        pre = _conv_pre(xx, w, tm + 8, 8)
        dd = jnp.concatenate([d_ref[...], jnp.where(last, 0.0, da_ref[...])], axis=0)
        dpre = dd * _dsilu(pre)
        dx = None
        for j in range(CONV_K):
            sh = CONV_K - 1 - j
            term = (pltpu.roll(dpre, tm + 8 - sh, 0) if sh else dpre)[:tm] * w[j]
            dx = term if dx is None else dx + term
        dx_ref[...] = dx.astype(BF16)

        @pl.when(i == 0)
        def _():
            dw_ref[...] = jnp.zeros_like(dw_ref)

        for j in range(CONV_K):
            sh = CONV_K - 1 - j
            xs = (pltpu.roll(xx, sh, 0) if sh else xx)[8:8 + tm]
            dw_ref[j:j + 1, :] += jnp.sum(dpre[:tm] * xs, axis=0, keepdims=True)

    return pl.pallas_call(
        body, name=name, grid=(ncols // tc, nt),
        in_specs=[pl.BlockSpec((tm, tc), lambda j, i: (i, cb0 + j)),
                  pl.BlockSpec((8, tc), lambda j, i: (jnp.maximum(i * hb - 1, 0), cb0 + j)),
                  pl.BlockSpec((8, tc), lambda j, i: (jnp.minimum((i + 1) * hb, nt * hb - 1), cb0 + j)),
                  pl.BlockSpec((tm, tc), lambda j, i: (i, j)),
                  pl.BlockSpec((8, tc), lambda j, i: (jnp.minimum((i + 1) * hb, nt * hb - 1), j)),
                  pl.BlockSpec((CONV_K, tc), lambda j, i: (0, j))],
        out_specs=(pl.BlockSpec((tm, tc), lambda j, i: (i, j)), pl.BlockSpec((CONV_K, tc), lambda j, i: (0, j))),
        out_shape=(jax.ShapeDtypeStruct((lp, ncols), BF16), jax.ShapeDtypeStruct((CONV_K, ncols), F32)),
        compiler_params=_cp("parallel", "arbitrary"))(proj, proj, proj, dact, dact, conv_w)


def _softplus(x):
    return jnp.maximum(x, 0.0) + jnp.log(1.0 + jnp.exp(-jnp.abs(x)))


def _gates_fwd(gl, a_log2, dt_bias2, name):
    lp, w2 = gl.shape
    hv = w2 // 2

    def body(gl_ref, al_ref, dt_ref, o_ref):
        x = gl_ref[...]
        live = _iota((lp, 1), 0) >= INERT
        is_beta = _iota((1, w2), 1) < hv
        g = -jnp.exp(al_ref[...]) * _softplus(x + dt_ref[...])
        o_ref[...] = jnp.where(live, jnp.where(is_beta, jax.nn.sigmoid(x), g), 0.0)

    return pl.pallas_call(body, name=name, out_shape=jax.ShapeDtypeStruct((lp, w2), F32))(gl, a_log2, dt_bias2)


def _gates_bwd(gl, a_log2, dt_bias2, dbg, name):
    lp, w2 = gl.shape
    hv = w2 // 2

    def body(gl_ref, al_ref, dt_ref, d_ref, dl_ref, dal_ref, ddt_ref):
        x = gl_ref[...]
        live = _iota((lp, 1), 0) >= INERT
        is_beta = _iota((1, w2), 1) < hv
        d = jnp.where(live, d_ref[...], 0.0)
        beta = jax.nn.sigmoid(x)
        ea = jnp.exp(al_ref[...])
        u = x + dt_ref[...]
        dg = jnp.where(is_beta, 0.0, d)
        dal_ref[...] = jnp.sum(dg * (-ea) * _softplus(u), axis=0, keepdims=True)
        du = dg * (-ea) * jax.nn.sigmoid(u)
        ddt_ref[...] = jnp.sum(du, axis=0, keepdims=True)
        dl_ref[...] = jnp.where(is_beta, d * beta * (1.0 - beta), du).astype(BF16)

    vec = jax.ShapeDtypeStruct((1, w2), F32)
    return pl.pallas_call(
        body, name=name, out_shape=(jax.ShapeDtypeStruct((lp, w2), BF16), vec, vec))(gl, a_log2, dt_bias2, dbg)


def _l2n(x):
    r = lax.rsqrt(jnp.sum(x * x, axis=-1, keepdims=True) + EPS)
    return x * r, r


def _tri_inverse(mats):
    eye = (_iota((CH, CH), 0) == _iota((CH, CH), 1)).astype(F32)
    ts = [eye - a for a in mats]
    ps = [_dot_f32(a, a) for a in mats]
    n = 2
    while n < CH:
        ts = [t + _dot_f32(t, p) for t, p in zip(ts, ps)]
        n *= 2
        if n < CH:
            ps = [_dot_f32(p, p) for p in ps]
    return ts


def _chunk_local(qn, kn, v, b_row, g_row):
    ri, ci = _iota((CH, CH), 0), _iota((CH, CH), 1)
    incl, strict = ri >= ci, ri > ci
    gam_col = jnp.sum(jnp.where(incl, g_row, 0.0), axis=1, keepdims=True)
    gam_row = _col_to_row(gam_col)
    b_col = _row_to_col(b_row)
    dec = jnp.exp(jnp.where(incl, gam_col - gam_row, -jnp.inf))
    eg = jnp.exp(gam_col)
    gl = jnp.sum(g_row, axis=1, keepdims=True)
    ekd = jnp.exp(gl - gam_col)
    kb = kn * b_col
    a = jnp.where(strict, _dot_nt(kb, kn) * dec, 0.0)
    p = jnp.where(incl, _dot_nt(qn, kn) * dec, 0.0)
    return dict(dec=dec, eg=eg, ekd=ekd, kb=kb, vb=v * b_col, a=a, kbg=kb * eg, p=p, qd=qn * eg, kd=kn * ekd,
                b_col=b_col, incl=incl, strict=strict)


def _chunks_per_step(nc):
    return max(g for g in (1, 2, 3, 6) if nc % g == 0)


def _heads_per_step(hv):
    return min(hv, 8)


def _delta_local(act, gates, key_w, name):
    lp = act.shape[0]
    hk, nc = key_w // HD, lp // CH
    hv = 2 * hk
    g = _chunks_per_step(nc)
    tr = g * CH

    def body(q_ref, k_ref, v_ref, g_ref, u_ref, w_ref, qd_ref, kd_ref, p_ref, t_ref):
        items = []
        for j in range(g):
            rows = slice(j * CH, (j + 1) * CH)
            qn = _l2n(q_ref[rows, :])[0] * (HD ** -0.5)
            kn = _l2n(k_ref[rows, :])[0]
            for e in range(2):
                cols = slice(e * HD, (e + 1) * HD)
                r = _chunk_local(qn, kn, v_ref[rows, cols], g_ref[0, j, e:e + 1, :], g_ref[0, j, 2 + e:3 + e, :])
                qd_ref[rows, cols] = r["qd"].astype(BF16)
                kd_ref[rows, cols] = r["kd"].astype(BF16)
                p_ref[e, rows, :] = r["p"].astype(BF16)
                items.append((rows, cols, e, r["a"], r["vb"].astype(BF16), r["kbg"].astype(BF16)))
        ts = [t.astype(BF16) for t in _tri_inverse([it[3] for it in items])]
        us = [_dot(t, it[4]) for t, it in zip(ts, items)]
        ws = [_dot(t, it[5]) for t, it in zip(ts, items)]
        for (rows, cols, e, _, _, _), t, u, w in zip(items, ts, us, ws):
            u_ref[rows, cols] = u
            w_ref[rows, cols] = w.astype(BF16)
            t_ref[e, rows, :] = t

    wide = pl.BlockSpec((tr, 2 * HD), lambda h, c: (c, h))
    sq = pl.BlockSpec((2, tr, CH), lambda h, c: (h, c, 0))
    wshape = lambda dt: jax.ShapeDtypeStruct((lp, hv * HD), dt)
    sshape = jax.ShapeDtypeStruct((hv, lp, CH), BF16)
    return pl.pallas_call(
        body, name=name, grid=(hk, nc // g),
        in_specs=[pl.BlockSpec((tr, HD), lambda h, c: (c, h)),
                  pl.BlockSpec((tr, HD), lambda h, c: (c, hk + h)),
                  pl.BlockSpec((tr, 2 * HD), lambda h, c: (c, hk + h)),
                  pl.BlockSpec((1, g, 8, CH), lambda h, c: (h, c, 0, 0))],
        out_specs=(wide, wide, wide, wide, sq, sq),
        out_shape=(wshape(F32), wshape(BF16), wshape(BF16), wshape(BF16), sshape, sshape),
        compiler_params=_cp("parallel", "parallel"))(act, act, act, gates)


def _chunk_decay(g_ref, e):
    return jnp.exp(jnp.sum(g_ref[e // 2, 0, 2 + e % 2:3 + e % 2, :], axis=1, keepdims=True))


def _delta_scan(u, w, qd, kd, p, gates, name):
    lp, val = u.shape
    hv, nc = val // HD, lp // CH
    nh = _heads_per_step(hv)

    def body(u_ref, w_ref, qd_ref, kd_ref, p_ref, g_ref, o_ref, vn_ref, st_ref, s_scr):
        @pl.when(pl.program_id(1) == 0)
        def _():
            s_scr[...] = jnp.zeros_like(s_scr)

        heads = range(nh)
        col = lambda e: slice(e * HD, (e + 1) * HD)
        ss = [s_scr[e] for e in heads]
        sb = [s.astype(BF16) for s in ss]
        for e in heads:
            st_ref[0, e] = ss[e]
        ws = [_dot(w_ref[:, col(e)], sb[e]) for e in heads]
        qs = [_dot(qd_ref[:, col(e)], sb[e]) for e in heads]
        vns = [(u_ref[:, col(e)] - ws[e]).astype(BF16) for e in heads]
        pv = [_dot(p_ref[e], vns[e]) for e in heads]
        kv = [_dot_tn(kd_ref[:, col(e)], vns[e]) for e in heads]
        for e in heads:
            o_ref[:, col(e)] = qs[e] + pv[e]
            s_scr[e] = _chunk_decay(g_ref, e) * ss[e] + kv[e]
            vn_ref[:, col(e)] = vns[e]

    wide = pl.BlockSpec((CH, nh * HD), lambda h, c: (c, h))
    return pl.pallas_call(
        body, name=name, grid=(hv // nh, nc),
        in_specs=[wide, wide, wide, wide, pl.BlockSpec((nh, CH, CH), lambda h, c: (h, c, 0)),
                  pl.BlockSpec((nh // 2, 1, 8, CH), lambda h, c: (h, c, 0, 0))],
        out_specs=(wide, wide, pl.BlockSpec((1, nh, HD, HD), lambda h, c: (c, h, 0, 0))),
        out_shape=(jax.ShapeDtypeStruct((lp, val), F32), jax.ShapeDtypeStruct((lp, val), BF16),
                   jax.ShapeDtypeStruct((nc, hv, HD, HD), F32)),
        scratch_shapes=[pltpu.VMEM((nh, HD, HD), F32)],
        compiler_params=_cp("parallel", "arbitrary"))(u, w, qd, kd, p, gates)


def _delta_scan_bwd(do, w, qd, kd, p, vn, states, gates, name):
    lp, val = do.shape
    hv, nc = val // HD, lp // CH
    nh = _heads_per_step(hv)

    def body(do_ref, w_ref, qd_ref, kd_ref, p_ref, vn_ref, st_ref, g_ref,
             dvn_ref, dw_ref, dqd_ref, dkd_ref, dp_ref, sd_ref, ds_scr):
        @pl.when(pl.program_id(1) == 0)
        def _():
            ds_scr[...] = jnp.zeros_like(ds_scr)

        incl = _iota((CH, CH), 0) >= _iota((CH, CH), 1)
        heads = range(nh)
        col = lambda e: slice(e * HD, (e + 1) * HD)
        ss = [st_ref[0, e] for e in heads]
        dss = [ds_scr[e] for e in heads]
        sb = [s.astype(BF16) for s in ss]
        dsb = [d.astype(BF16) for d in dss]
        dos = [do_ref[:, col(e)].astype(BF16) for e in heads]
        egl = [_chunk_decay(g_ref, e) for e in heads]
        pdo = [_dot_tn(p_ref[e], dos[e]) for e in heads]
        kds = [_dot(kd_ref[:, col(e)], dsb[e]) for e in heads]
        qdo = [_dot_tn(qd_ref[:, col(e)], dos[e]) for e in heads]
        dqd = [_dot_nt(dos[e], sb[e]) for e in heads]
        dkd = [_dot_nt(vn_ref[:, col(e)], dsb[e]) for e in heads]
        dpp = [_dot_nt(dos[e], vn_ref[:, col(e)]) for e in heads]
        dvn = [(pdo[e] + kds[e]).astype(BF16) for e in heads]
        wdv = [_dot_tn(w_ref[:, col(e)], dvn[e]) for e in heads]
        dws = [_dot_nt(dvn[e], sb[e]) for e in heads]
        for e in heads:
            ds_scr[e] = qdo[e] + egl[e] * dss[e] - wdv[e]
            dvn_ref[:, col(e)] = dvn[e]
            dw_ref[:, col(e)] = (-dws[e]).astype(BF16)
            dqd_ref[:, col(e)] = dqd[e]
            dkd_ref[:, col(e)] = dkd[e]
            dp_ref[e] = jnp.where(incl, dpp[e], 0.0)
            sd_ref[0, 0, e:e + 1, :] = jnp.broadcast_to(egl[e] * jnp.sum(ss[e] * dss[e], keepdims=True), (1, HD))

    rev = lambda c: nc - 1 - c
    wide = pl.BlockSpec((CH, nh * HD), lambda h, c: (rev(c), h))
    sq = pl.BlockSpec((nh, CH, CH), lambda h, c: (h, rev(c), 0))
    wshape = lambda dt: jax.ShapeDtypeStruct((lp, val), dt)
    return pl.pallas_call(
        body, name=name, grid=(hv // nh, nc),
        in_specs=[wide, wide, wide, wide, sq, wide, pl.BlockSpec((1, nh, HD, HD), lambda h, c: (rev(c), h, 0, 0)),
                  pl.BlockSpec((nh // 2, 1, 8, CH), lambda h, c: (h, rev(c), 0, 0))],
        out_specs=(wide, wide, wide, wide, sq, pl.BlockSpec((1, 1, nh, HD), lambda h, c: (h, rev(c), 0, 0))),
        out_shape=(wshape(BF16), wshape(BF16), wshape(F32), wshape(F32), jax.ShapeDtypeStruct((hv, lp, CH), F32),
                   jax.ShapeDtypeStruct((hv // nh, nc, nh, HD), F32)),
        scratch_shapes=[pltpu.VMEM((nh, HD, HD), F32)],
        compiler_params=_cp("parallel", "arbitrary"))(do, w, qd, kd, p, vn, states, gates)


def _delta_local_bwd(act, gates, t, dvn, dw, dqd, dkd, dp, key_w, name):
    lp = act.shape[0]
    hk, nc = key_w // HD, lp // CH
    hv = 2 * hk
    g = _chunks_per_step(nc)
    tr = g * CH
    scale = HD ** -0.5

    def body(q_ref, k_ref, v_ref, g_ref, t_ref, dvn_ref, dw_ref, dqd_ref, dkd_ref, dp_ref, dq_ref, dk_ref, dv_ref, dg_ref):
        ri, ci = _iota((CH, CH), 0), _iota((CH, CH), 1)
        inner = lambda x, z: jnp.sum(x * z, axis=1, keepdims=True)
        norms, items = [], []
        for j in range(g):
            rows = slice(j * CH, (j + 1) * CH)
            qh, qr = _l2n(q_ref[rows, :])
            kn, kr = _l2n(k_ref[rows, :])
            qn = qh * scale
            norms.append((rows, qh, qr, kn, kr, qn))
            dg_ref[0, j, 4:8, :] = jnp.zeros((4, CH), F32)
            for e in range(2):
                cols = slice(e * HD, (e + 1) * HD)
                v = v_ref[rows, cols]
                r = _chunk_local(qn, kn, v, g_ref[0, j, e:e + 1, :], g_ref[0, j, 2 + e:3 + e, :])
                items.append(dict(r, j=j, e=e, rows=rows, cols=cols, v=v, kn=kn, qn=qn, t=t_ref[e, rows, :],
                                  dvn=dvn_ref[rows, cols], dw=dw_ref[rows, cols]))
        for it in items:
            it["dt"] = _dot_nt(it["dvn"], it["vb"]) + _dot_nt(it["dw"], it["kbg"])
            it["dvb"] = _dot_tn(it["t"], it["dvn"])
            it["dkbg"] = _dot_tn(it["t"], it["dw"])
        for it in items:
            it["x"] = _dot_tn(it["t"], it["dt"])
        for it in items:
            it["da"] = -jnp.where(it["strict"], _dot_nt(it["x"], it["t"]), 0.0)
        for it in items:
            dp = dp_ref[it["e"], it["rows"], :]
            it["gmat"] = it["da"] * it["a"] + dp * it["p"]
            mm, nn = (it["da"] * it["dec"]).astype(BF16), (dp * it["dec"]).astype(BF16)
            it["dkb"] = _dot(mm, it["kn"]) + it["dkbg"] * it["eg"]
            it["dkn"] = _dot_tn(mm, it["kb"]) + _dot_tn(nn, it["qn"])
            it["dqn"] = _dot(nn, it["kn"])
        for it in items:
            j, e, rows, cols = it["j"], it["e"], it["rows"], it["cols"]
            dqd, dkd, gmat, dkb = dqd_ref[rows, cols], dkd_ref[rows, cols], it["gmat"], it["dkb"]
            it["dkn"] = it["dkn"] + dkd * it["ekd"] + it["b_col"] * dkb
            it["dqn"] = it["dqn"] + dqd * it["eg"]
            dkd_kd = inner(dkd, it["kd"])
            dgam = (jnp.sum(gmat, axis=1, keepdims=True) - _row_to_col(jnp.sum(gmat, axis=0, keepdims=True))
                    + inner(dqd, it["qd"]) + inner(it["dkbg"], it["kbg"]) - dkd_kd)
            dgl = jnp.max(g_ref[0, j, 4 + e:5 + e, :], axis=1, keepdims=True) + jnp.sum(dkd_kd, keepdims=True)
            dgam = dgam + jnp.where(_iota((CH, 1), 0) == CH - 1, dgl, 0.0)
            dg_ref[0, j, 2 + e:3 + e, :] = jnp.sum(jnp.where(ri >= ci, dgam, 0.0), axis=0, keepdims=True)
            dg_ref[0, j, e:e + 1, :] = _col_to_row(inner(dkb, it["kn"]) + inner(it["dvb"], it["v"]))
            dv_ref[rows, cols] = it["b_col"] * it["dvb"]
        for j, (rows, qh, qr, kn, kr, _) in enumerate(norms):
            dqh = (items[2 * j]["dqn"] + items[2 * j + 1]["dqn"]) * scale
            dkn = items[2 * j]["dkn"] + items[2 * j + 1]["dkn"]
            dq_ref[rows, :] = qr * (dqh - qh * jnp.sum(dqh * qh, axis=1, keepdims=True))
            dk_ref[rows, :] = kr * (dkn - kn * jnp.sum(dkn * kn, axis=1, keepdims=True))

    narrow = pl.BlockSpec((tr, HD), lambda h, c: (c, h))
    wide = pl.BlockSpec((tr, 2 * HD), lambda h, c: (c, h))
    sq = pl.BlockSpec((2, tr, CH), lambda h, c: (h, c, 0))
    gate = pl.BlockSpec((1, g, 8, CH), lambda h, c: (h, c, 0, 0))
    return pl.pallas_call(
        body, name=name, grid=(hk, nc // g),
        in_specs=[narrow, pl.BlockSpec((tr, HD), lambda h, c: (c, hk + h)),
                  pl.BlockSpec((tr, 2 * HD), lambda h, c: (c, hk + h)), gate, sq, wide, wide, wide, wide, sq],
        out_specs=(narrow, narrow, wide, gate),
        out_shape=(jax.ShapeDtypeStruct((lp, key_w), F32), jax.ShapeDtypeStruct((lp, key_w), F32),
                   jax.ShapeDtypeStruct((lp, hv * HD), F32), jax.ShapeDtypeStruct((hk, nc, 8, CH), F32)),
        compiler_params=_cp("parallel", "parallel"))(act, act, act, gates, t, dvn, dw, dqd, dkd, dp)


def _outnorm_fwd(o, proj, z_col0, w, name):
    lp, val = o.shape
    tm, zb = _tile(lp, 1056), z_col0 // HD

    def body(o_ref, z_ref, w_ref, y_ref):
        xf = o_ref[...]
        r = lax.rsqrt(jnp.mean(xf * xf, axis=-1, keepdims=True) + EPS)
        y_ref[...] = (xf * r * w_ref[...] * _silu(z_ref[...])).astype(BF16)

    return pl.pallas_call(
        body, name=name, grid=(lp // tm, val // HD),
        in_specs=[pl.BlockSpec((tm, HD), lambda i, h: (i, h)), pl.BlockSpec((tm, HD), lambda i, h: (i, zb + h)),
                  pl.BlockSpec((1, HD), lambda i, h: (0, 0))],
        out_specs=pl.BlockSpec((tm, HD), lambda i, h: (i, h)), out_shape=jax.ShapeDtypeStruct((lp, val), BF16),
        compiler_params=_cp("parallel", "parallel"))(o, proj, w)


def _outnorm_bwd(o, proj, z_col0, w, dy, name):
    lp, val = o.shape
    tm, zb = _tile(lp, 1056), z_col0 // HD

    def body(o_ref, z_ref, w_ref, dy_ref, do_ref, dz_ref, dw_ref):
        xf, z, d = o_ref[...], z_ref[...], dy_ref[...]
        r = lax.rsqrt(jnp.mean(xf * xf, axis=-1, keepdims=True) + EPS)
        xhat = xf * r
        dn = d * _silu(z)
        dz_ref[...] = (d * xhat * w_ref[...] * _dsilu(z)).astype(BF16)
        dxhat = dn * w_ref[...]
        do_ref[...] = r * (dxhat - xhat * jnp.mean(dxhat * xhat, axis=-1, keepdims=True))

        @pl.when((pl.program_id(0) == 0) & (pl.program_id(1) == 0))
        def _():
            dw_ref[...] = jnp.zeros_like(dw_ref)

        dw_ref[...] += jnp.sum(dn * xhat, axis=0, keepdims=True)

    blk = pl.BlockSpec((tm, HD), lambda i, h: (i, h))
    vec = pl.BlockSpec((1, HD), lambda i, h: (0, 0))
    return pl.pallas_call(
        body, name=name, grid=(lp // tm, val // HD),
        in_specs=[blk, pl.BlockSpec((tm, HD), lambda i, h: (i, zb + h)), vec, blk], out_specs=(blk, blk, vec),
        out_shape=(jax.ShapeDtypeStruct((lp, val), F32), jax.ShapeDtypeStruct((lp, val), BF16),
                   jax.ShapeDtypeStruct((1, HD), F32)),
        compiler_params=_cp("arbitrary", "arbitrary"))(o, proj, w, dy)


def _qknorm_fwd(proj, qw, kw, width, name):
    lp = proj.shape[0]
    tm, nh = _tile(lp, 1056), width // HD

    def body(q_ref, k_ref, v_ref, qw_ref, kw_ref, qo_ref, ko_ref, vo_ref):
        for x_ref, w_ref, o_ref in ((q_ref, qw_ref, qo_ref), (k_ref, kw_ref, ko_ref)):
            xf = x_ref[...]
            r = lax.rsqrt(jnp.mean(xf * xf, axis=-1, keepdims=True) + EPS)
            o_ref[...] = (xf * r * w_ref[...]).astype(BF16)
        vo_ref[...] = v_ref[...].astype(BF16)

    blk = lambda off: pl.BlockSpec((tm, HD), lambda i, h: (i, off + h))
    vec = pl.BlockSpec((1, HD), lambda i, h: (0, 0))
    shp = jax.ShapeDtypeStruct((lp, width), BF16)
    return pl.pallas_call(
        body, name=name, grid=(lp // tm, nh), in_specs=[blk(0), blk(nh), blk(2 * nh), vec, vec],
        out_specs=(blk(0), blk(0), blk(0)), out_shape=(shp, shp, shp),
        compiler_params=_cp("parallel", "parallel"))(proj, proj, proj, qw, kw)


def _qknorm_bwd(proj, qw, kw, dqn, dkn, width, name):
    lp = proj.shape[0]
    tm, nh = _tile(lp, 1056), width // HD

    def body(q_ref, k_ref, qw_ref, kw_ref, dqn_ref, dkn_ref, dq_ref, dk_ref, dqw_ref, dkw_ref):
        first = (pl.program_id(0) == 0) & (pl.program_id(1) == 0)
        for x_ref, w_ref, dy_ref, dx_ref, dw_ref in ((q_ref, qw_ref, dqn_ref, dq_ref, dqw_ref),
                                                       (k_ref, kw_ref, dkn_ref, dk_ref, dkw_ref)):
            xf, dy = x_ref[...], dy_ref[...]
            r = lax.rsqrt(jnp.mean(xf * xf, axis=-1, keepdims=True) + EPS)
            xhat = xf * r
            dxhat = dy * w_ref[...]
            dx_ref[...] = (r * (dxhat - xhat * jnp.mean(dxhat * xhat, axis=-1, keepdims=True))).astype(BF16)

            @pl.when(first)
            def _():
                dw_ref[...] = jnp.zeros_like(dw_ref)

            dw_ref[...] += jnp.sum(dy * xhat, axis=0, keepdims=True)

    blk = lambda off: pl.BlockSpec((tm, HD), lambda i, h: (i, off + h))
    vec = pl.BlockSpec((1, HD), lambda i, h: (0, 0))
    shp = jax.ShapeDtypeStruct((lp, width), BF16)
    vshp = jax.ShapeDtypeStruct((1, HD), F32)
    return pl.pallas_call(
        body, name=name, grid=(lp // tm, nh), in_specs=[blk(0), blk(nh), vec, vec, blk(0), blk(0)],
        out_specs=(blk(0), blk(0), vec, vec), out_shape=(shp, shp, vshp, vshp),
        compiler_params=_cp("arbitrary", "arbitrary"))(proj, proj, qw, kw, dqn, dkn)


def _sb_tq(lp):
    return 3 * QB if lp % (3 * QB) == 0 else QB


def _sb_scores(qk, t_idx, kb):
    z = qk * (HD ** -0.5)
    s_idx = kb * QB + _iota((1, QB), 1)
    valid = (s_idx < t_idx) & (s_idx >= INERT)
    sp = jnp.log(1.0 + jnp.exp(-jnp.abs(z)))
    lsz = jnp.minimum(z, 0.0) - sp
    lk = jnp.where(valid, -jnp.maximum(z, 0.0) - sp, 0.0)
    return valid, lsz, lk


def _sb_fwd(qn, kn, vv, proj, gate_col0, name):
    lp, width = qn.shape
    tq = _sb_tq(lp)
    nh, nq, gb, nsub = width // HD, lp // tq, gate_col0 // HD, tq // QB

    def body(q_ref, k_ref, v_ref, g_ref, o_ref, og_ref, tot_ref):
        qb = pl.program_id(1)
        q = q_ref[...]
        t_idx = qb * tq + _iota((tq, 1), 0)
        upper = (_iota((QB, QB), 0) > _iota((QB, QB), 1)).astype(BF16)

        def step(i, carry):
            run, acc = carry
            kbs = [(qb - i) * nsub + sub for sub in reversed(range(nsub))]
            rows = [pl.ds(pl.multiple_of(kb * QB, QB), QB) for kb in kbs]
            qks = [_dot_nt(q, k_ref[r, :]) for r in rows]
            scores = [_sb_scores(qk, t_idx, kb) for qk, kb in zip(qks, kbs)]
            sums = [_dot_split(lk, upper) for _, _, lk in scores]
            probs = []
            for (valid, lsz, lk), part in zip(scores, sums):
                probs.append(jnp.where(valid, jnp.exp(lsz + part + run), 0.0).astype(BF16))
                run = run + jnp.sum(lk, axis=1, keepdims=True)
            for a, r in zip(probs, rows):
                acc = acc + _dot(a, v_ref[r, :])
            return run, acc

        run, acc = lax.fori_loop(0, qb + 1, step, (jnp.zeros((tq, 1), F32), jnp.zeros((tq, HD), F32)))
        o_ref[...] = acc
        og_ref[...] = (acc * _silu(g_ref[...])).astype(BF16)
        tot_ref[0, 0] = _col_to_row(run)

    full = pl.BlockSpec((lp, HD), lambda h, i: (0, h))
    blk = pl.BlockSpec((tq, HD), lambda h, i: (i, h))
    return pl.pallas_call(
        body, name=name, grid=(nh, nq),
        in_specs=[blk, full, full, pl.BlockSpec((tq, HD), lambda h, i: (i, gb + h))],
        out_specs=(blk, blk, pl.BlockSpec((1, 1, 1, tq), lambda h, i: (h, i, 0, 0))),
        out_shape=(jax.ShapeDtypeStruct((lp, width), F32), jax.ShapeDtypeStruct((lp, width), BF16),
                   jax.ShapeDtypeStruct((nh, nq, 1, tq), F32)),
        compiler_params=_cp("parallel", "parallel"))(qn, kn, vv, proj)


def _sb_bwd(qn, kn, vv, proj, gate_col0, att, tot, dog, name):
    lp, width = qn.shape
    tq = _sb_tq(lp)
    nh, nq, gb, nsub = width // HD, lp // tq, gate_col0 // HD, tq // QB
    scale = HD ** -0.5

    def body(q_ref, k_ref, v_ref, g_ref, att_ref, tot_ref, dog_ref, dq_ref, dk_ref, dv_ref, dg_ref, dk_acc, dv_acc):
        qb = pl.program_id(1)

        @pl.when(qb == 0)
        def _():
            dk_acc[...] = jnp.zeros_like(dk_acc)
            dv_acc[...] = jnp.zeros_like(dv_acc)

        q, gate, dg_out = q_ref[...], g_ref[...], dog_ref[...]
        d_o = (dg_out * _silu(gate)).astype(BF16)
        dg_ref[...] = (dg_out * att_ref[...] * _dsilu(gate)).astype(BF16)
        total = _row_to_col(tot_ref[0, 0])
        t_idx = qb * tq + _iota((tq, 1), 0)
        ri, ci = _iota((QB, QB), 0), _iota((QB, QB), 1)
        lower_incl = (ri <= ci).astype(BF16)
        lower_excl = (ri < ci).astype(BF16)

        def step(kg, carry):
            run, erun, dq = carry
            kbs = [kg * nsub + sub for sub in range(nsub)]
            rows = [pl.ds(pl.multiple_of(kb * QB, QB), QB) for kb in kbs]
            qks = [_dot_nt(q, k_ref[r, :]) for r in rows]
            dprobs = [_dot_nt(d_o, v_ref[r, :]) for r in rows]
            scores = [_sb_scores(qk, t_idx, kb) for qk, kb in zip(qks, kbs)]
            sums = [_dot_split(lk, lower_incl) for _, _, lk in scores]
            probs, es = [], []
            for (valid, lsz, lk), part, dprob in zip(scores, sums, dprobs):
                a = jnp.where(valid, jnp.exp(lsz + (total - run - part)), 0.0)
                probs.append(a.astype(BF16))
                es.append(a * dprob)
                run = run + jnp.sum(lk, axis=1, keepdims=True)
            esums = [_dot_split(e, lower_excl) for e in es]
            for a, r in zip(probs, rows):
                dv_acc[r, :] += _dot_tn(a, d_o)
            dzs = []
            for (valid, lsz, _), e, part in zip(scores, es, esums):
                sig = jnp.exp(lsz)
                dzs.append((jnp.where(valid, e * (1.0 - sig) - sig * (erun + part), 0.0) * scale).astype(BF16))
                erun = erun + jnp.sum(e, axis=1, keepdims=True)
            for dz, r in zip(dzs, rows):
                dk_acc[r, :] += _dot_tn(dz, q)
                dq = dq + _dot(dz, k_ref[r, :])
            return run, erun, dq

        zero = jnp.zeros((tq, 1), F32)
        _, _, dq = lax.fori_loop(0, qb + 1, step, (zero, zero, jnp.zeros((tq, HD), F32)))
        dq_ref[...] = dq

        @pl.when(qb == nq - 1)
        def _():
            dk_ref[...] = dk_acc[...]
            dv_ref[...] = dv_acc[...].astype(BF16)

    full = pl.BlockSpec((lp, HD), lambda h, i: (0, h))
    blk = pl.BlockSpec((tq, HD), lambda h, i: (i, h))
    return pl.pallas_call(
        body, name=name, grid=(nh, nq),
        in_specs=[blk, full, full, pl.BlockSpec((tq, HD), lambda h, i: (i, gb + h)), blk,
                  pl.BlockSpec((1, 1, 1, tq), lambda h, i: (h, i, 0, 0)), blk],
        out_specs=(blk, full, full, blk),
        out_shape=(jax.ShapeDtypeStruct((lp, width), F32), jax.ShapeDtypeStruct((lp, width), F32),
                   jax.ShapeDtypeStruct((lp, width), BF16), jax.ShapeDtypeStruct((lp, width), BF16)),
        scratch_shapes=[pltpu.VMEM((lp, HD), F32), pltpu.VMEM((lp, HD), F32)],
        compiler_params=_cp("parallel", "arbitrary"))(qn, kn, vv, proj, att, tot, dog)


def _loss_head(h, target, name):
    lp, d = h.shape
    tm = _tile(PAD, 128)
    nt = lp // tm
    npad = PAD // tm

    def body(h_ref, t_ref, l_ref, dh_ref, dhb_ref):
        i = pl.program_id(0)
        err = jnp.where(i >= npad, h_ref[...] - t_ref[...], 0.0)
        dh = err * (1.0 / d)
        dh_ref[...] = dh
        dhb_ref[...] = dh.astype(BF16)

        @pl.when(i == 0)
        def _():
            l_ref[...] = jnp.zeros_like(l_ref)

        l_ref[...] += (0.5 / d) * jnp.sum(err * err, keepdims=True)

    row = pl.BlockSpec((tm, d), lambda i: (i, 0))
    return pl.pallas_call(
        body, name=name, grid=(nt,),
        in_specs=[row, pl.BlockSpec((tm, d), lambda i: (jnp.maximum(i - npad, 0), 0))],
        out_specs=(pl.BlockSpec((1, 1), lambda i: (0, 0)), row, row),
        out_shape=(jax.ShapeDtypeStruct((1, 1), F32), jax.ShapeDtypeStruct((lp, d), F32),
                   jax.ShapeDtypeStruct((lp, d), BF16)),
        compiler_params=_cp("arbitrary"))(h, target)


def _adamw(parts, w, m, v, name):
    r, c = w.shape
    tr = _tile(r, max(8, (1 << 18) // c // 8 * 8))

    def body(p_ref, w_ref, m_ref, v_ref, g_ref, d_ref, mo_ref, vo_ref):
        g = p_ref[0].astype(F32)
        for k in range(1, NDEV):
            g = g + p_ref[k].astype(F32)
        mn = ADAM_B1 * m_ref[...] + (1.0 - ADAM_B1) * g
        vn = ADAM_B2 * v_ref[...] + (1.0 - ADAM_B2) * jnp.square(g)
        m_hat = mn / (1.0 - ADAM_B1 ** ADAM_STEP)
        v_hat = vn / (1.0 - ADAM_B2 ** ADAM_STEP)
        g_ref[...] = g
        d_ref[...] = -ADAM_LR * (m_hat / (jnp.sqrt(v_hat) + ADAM_EPS) + ADAM_WD * w_ref[...])
        mo_ref[...] = mn
        vo_ref[...] = vn

    blk = pl.BlockSpec((tr, c), lambda i: (i, 0))
    shp = jax.ShapeDtypeStruct((r, c), F32)
    return pl.pallas_call(
        body, name=name, grid=(r // tr,), in_specs=[pl.BlockSpec((NDEV, tr, c), lambda i: (0, i, 0)), blk, blk, blk],
        out_specs=(blk, blk, blk, blk), out_shape=(shp, shp, shp, shp), compiler_params=_cp("parallel"))(parts, w, m, v)


def _unshard_cols(g):
    return jnp.transpose(g, (1, 0, 2)).reshape(g.shape[1], NDEV * g.shape[2])


def _shard_cols(a):
    r, c = a.shape
    return jnp.transpose(a.reshape(r, NDEV, c // NDEV), (1, 0, 2))


def kernel(x, meta_tokens, dn_norm_w, dn_w_in, dn_conv_w, dn_a_log, dn_dt_bias, dn_out_norm_w, dn_w_out, sb_norm_w, sb_w_in, sb_q_norm_w, sb_k_norm_w, sb_w_out, loss_target, m_meta_tokens, m_dn_norm_w, m_dn_w_in, m_dn_conv_w, m_dn_a_log, m_dn_dt_bias, m_dn_out_norm_w, m_dn_w_out, m_sb_norm_w, m_sb_w_in, m_sb_q_norm_w, m_sb_k_norm_w, m_sb_w_out, v_meta_tokens, v_dn_norm_w, v_dn_w_in, v_dn_conv_w, v_dn_a_log, v_dn_dt_bias, v_dn_out_norm_w, v_dn_w_out, v_sb_norm_w, v_sb_w_in, v_sb_q_norm_w, v_sb_k_norm_w, v_sb_w_out):
    seq, d = x.shape[1], x.shape[2]
    lp = PAD + seq
    key_w = d
    val_w = 2 * d
    hv = val_w // HD
    conv_w_cols = 2 * key_w + val_w
    main_w = conv_w_cols + val_w
    sb_w = d
    nc = lp // CH
    hk = key_w // HD

    (g_in0, g_out0, g_in1, g_out1) = _exchange(
        [dn_w_in[0].astype(BF16), dn_w_out[0].astype(BF16), sb_w_in[0].astype(BF16), sb_w_out[0].astype(BF16)],
        True, "gather_weights")
    (g_meta, g_sbn, g_conv) = _exchange([meta_tokens, sb_norm_w, dn_conv_w[0]], True, "gather_vectors")
    w_in0 = _unshard_cols(g_in0)
    w_in0_main, w_in0_gate = w_in0[:, :main_w], w_in0[:, main_w:]
    w_out0 = g_out0.reshape(val_w, d)
    w_in1 = _unshard_cols(g_in1)
    w_out1 = g_out1.reshape(sb_w, d)
    meta = _unshard_cols(g_meta)
    sbn_w = _unshard_cols(g_sbn)
    conv_w = _unshard_cols(g_conv)

    h0 = jnp.concatenate([jnp.zeros((INERT, d), F32), meta, x[0]], axis=0)
    hn0 = _rms_fwd(h0, dn_norm_w, "dn_norm")
    proj0 = _matmul(hn0, w_in0_main, mode="nn", out_dtype=F32, tm=1056, tn=512, tk=4096, name="dn_in_proj")
    gl0 = _matmul(hn0, w_in0_gate, mode="nn", out_dtype=F32, tm=1056, tn=512, tk=4096, name="dn_gate_proj")
    act0 = _conv_fwd(proj0, conv_w, conv_w_cols, "dn_conv")
    a_log2 = jnp.concatenate([jnp.zeros_like(dn_a_log), dn_a_log], axis=1)
    dt_bias2 = jnp.concatenate([jnp.zeros_like(dn_dt_bias), dn_dt_bias], axis=1)
    bg = _gates_fwd(gl0, a_log2, dt_bias2, "dn_gates")
    pack = lambda t: jnp.transpose(t.reshape(nc, CH, hk, 2), (2, 0, 3, 1))
    gates = jnp.concatenate([pack(bg[:, :hv]), pack(bg[:, hv:]), jnp.zeros((hk, nc, 4, CH), F32)], axis=2)
    u0, w0, qd0, kd0, p0, t0 = _delta_local(act0, gates, key_w, "dn_delta_local")
    o0, vn0, states = _delta_scan(u0, w0, qd0, kd0, p0, gates, "dn_delta_scan")
    o0g = _outnorm_fwd(o0, proj0, conv_w_cols, dn_out_norm_w, "dn_out_norm")
    h1 = _matmul(o0g, w_out0, mode="nn", out_dtype=F32, tm=1056, tn=512, tk=4096, name="dn_out_proj", add=h0)
    hn1 = _rms_fwd(h1, sbn_w, "sb_norm")
    proj1 = _matmul(hn1, w_in1, mode="nn", out_dtype=F32, tm=1056, tn=512, tk=4096, name="sb_in_proj")
    qn1, kn1, vv1 = _qknorm_fwd(proj1, sb_q_norm_w, sb_k_norm_w, sb_w, "sb_qk_norm")
    att1, o1g, tot1 = _sb_fwd(qn1, kn1, vv1, proj1, 3 * sb_w, "sb_attn")
    h2 = _matmul(o1g, w_out1, mode="nn", out_dtype=F32, tm=1056, tn=512, tk=4096, name="sb_out_proj", add=h1)
    loss_part, dh2, dh2b = _loss_head(h2, loss_target[0], "loss_head")
    loss = lax.psum(loss_part[0, 0], ("x", "y", "c"))

    p_out1 = _matmul(o1g, dh2b, mode="tn", out_dtype=BF16, tm=1024, tn=512, tk=lp, name="sb_out_wgrad")
    do1g = _matmul(dh2b, w_out1, mode="nt", out_dtype=F32, tm=1056, tn=512, tk=4096, name="sb_out_dgrad")
    dqn1, dkn1, dv1, dgate1 = _sb_bwd(qn1, kn1, vv1, proj1, 3 * sb_w, att1, tot1, do1g, "sb_attn_bwd")
    dq1, dk1, d_qw, d_kw = _qknorm_bwd(proj1, sb_q_norm_w, sb_k_norm_w, dqn1, dkn1, sb_w, "sb_qk_norm_bwd")
    dproj1 = jnp.concatenate([dq1, dk1, dv1, dgate1], axis=1)
    p_in1 = _matmul(hn1, dproj1, mode="tn", out_dtype=BF16, tm=1024, tn=512, tk=lp, name="sb_in_wgrad")
    dhn1 = _matmul(dproj1, w_in1, mode="nt", out_dtype=F32, tm=1056, tn=512, tk=4096, name="sb_in_dgrad")
    dh1, dh1b, d_sbn = _rms_bwd(h1, sbn_w, dhn1, dh2, "sb_norm_bwd")

    p_out0 = _matmul(o0g, dh1b, mode="tn", out_dtype=BF16, tm=1024, tn=512, tk=lp, name="dn_out_wgrad")
    do0g = _matmul(dh1b, w_out0, mode="nt", out_dtype=F32, tm=1056, tn=512, tk=4096, name="dn_out_dgrad")
    do0, dz0, d_onw = _outnorm_bwd(o0, proj0, conv_w_cols, dn_out_norm_w, do0g, "dn_out_norm_bwd")
    dvn0, dw0, dqd0, dkd0, dp0, sd0 = _delta_scan_bwd(do0, w0, qd0, kd0, p0, vn0, states, gates, "dn_delta_scan_bwd")
    sd_rows = jnp.transpose(jnp.transpose(sd0[..., 0], (0, 2, 1)).reshape(hk, 2, nc), (0, 2, 1))
    gates_b = jnp.concatenate([gates[:, :, :4], jnp.broadcast_to(sd_rows[..., None], (hk, nc, 2, CH)),
                               jnp.zeros((hk, nc, 2, CH), F32)], axis=2)
    dq_act, dk_act, dv_act, dgates = _delta_local_bwd(act0, gates_b, t0, dvn0, dw0, dqd0, dkd0, dp0, key_w,
                                                      "dn_delta_local_bwd")
    unpack = lambda t: jnp.transpose(t, (1, 3, 0, 2)).reshape(lp, hv)
    dbg = jnp.concatenate([unpack(dgates[:, :, 0:2]), unpack(dgates[:, :, 2:4])], axis=1)
    dgl0, d_alog2, d_dtb2 = _gates_bwd(gl0, a_log2, dt_bias2, dbg, "dn_gates_bwd")
    d_alog, d_dtb = d_alog2[:, hv:], d_dtb2[:, hv:]
    dxq, dcw_q = _conv_bwd(proj0, 0, conv_w[:, :key_w], dq_act, "dn_conv_bwd_q")
    dxk, dcw_k = _conv_bwd(proj0, key_w, conv_w[:, key_w:2 * key_w], dk_act, "dn_conv_bwd_k")
    dxv, dcw_v = _conv_bwd(proj0, 2 * key_w, conv_w[:, 2 * key_w:], dv_act, "dn_conv_bwd_v")
    dproj0 = jnp.concatenate([dxq, dxk, dxv, dz0], axis=1)
    p_in0_main = _matmul(hn0, dproj0, mode="tn", out_dtype=BF16, tm=1024, tn=512, tk=lp, name="dn_in_wgrad")
    p_in0_gate = _matmul(hn0, dgl0, mode="tn", out_dtype=BF16, tm=1024, tn=512, tk=lp, name="dn_gate_wgrad")
    dhn0 = _matmul(dgl0, w_in0_gate, mode="nt", out_dtype=F32, tm=1056, tn=512, tk=4096, name="dn_gate_dgrad")
    dhn0 = _matmul(dproj0, w_in0_main, mode="nt", out_dtype=F32, tm=1056, tn=512, tk=4096, name="dn_in_dgrad", add=dhn0)
    dh0, _, d_dnn = _rms_bwd(h0, dn_norm_w, dhn0, dh1, "dn_norm_bwd")
    grad_x = dh0[PAD:][None]

    p_in0 = _shard_cols(jnp.concatenate([p_in0_main, p_in0_gate], axis=1))
    (r_in0, r_out0, r_in1, r_out1) = _exchange(
        [p_in0, p_out0.reshape(NDEV, val_w // NDEV, d), _shard_cols(p_in1), p_out1.reshape(NDEV, sb_w // NDEV, d)],
        False, "scatter_weight_grads")
    p_conv = _shard_cols(jnp.concatenate([dcw_q, dcw_k, dcw_v], axis=1))
    (r_meta, r_sbn, r_conv) = _exchange([_shard_cols(dh0[INERT:PAD]), _shard_cols(d_sbn), p_conv], False, "scatter_vector_grads")
    small = jnp.concatenate([d_dnn, d_alog, d_dtb, d_onw, d_qw, d_kw], axis=1)
    (r_small,) = _exchange([small], True, "gather_replicated_grads")

    outs = {}
    outs["meta_tokens"] = _adamw(r_meta, meta_tokens, m_meta_tokens, v_meta_tokens, "adamw_meta")
    outs["dn_w_in"] = _adamw(r_in0, dn_w_in[0], m_dn_w_in[0], v_dn_w_in[0], "adamw_dn_w_in")
    outs["dn_conv_w"] = _adamw(r_conv, dn_conv_w[0], m_dn_conv_w[0], v_dn_conv_w[0], "adamw_dn_conv")
    outs["dn_w_out"] = _adamw(r_out0, dn_w_out[0], m_dn_w_out[0], v_dn_w_out[0], "adamw_dn_w_out")
    outs["sb_norm_w"] = _adamw(r_sbn, sb_norm_w, m_sb_norm_w, v_sb_norm_w, "adamw_sb_norm")
    outs["sb_w_in"] = _adamw(r_in1, sb_w_in[0], m_sb_w_in[0], v_sb_w_in[0], "adamw_sb_w_in")
    outs["sb_w_out"] = _adamw(r_out1, sb_w_out[0], m_sb_w_out[0], v_sb_w_out[0], "adamw_sb_w_out")
    cat = lambda *a: jnp.concatenate(a, axis=1)
    rep = _adamw(r_small, cat(dn_norm_w, dn_a_log, dn_dt_bias, dn_out_norm_w, sb_q_norm_w, sb_k_norm_w),
                 cat(m_dn_norm_w, m_dn_a_log, m_dn_dt_bias, m_dn_out_norm_w, m_sb_q_norm_w, m_sb_k_norm_w),
                 cat(v_dn_norm_w, v_dn_a_log, v_dn_dt_bias, v_dn_out_norm_w, v_sb_q_norm_w, v_sb_k_norm_w),
                 "adamw_replicated")
    off = 0
    for nm, wd in (("dn_norm_w", d), ("dn_a_log", hv), ("dn_dt_bias", hv), ("dn_out_norm_w", HD),
                   ("sb_q_norm_w", HD), ("sb_k_norm_w", HD)):
        outs[nm] = tuple(t[:, off:off + wd] for t in rep)
        off += wd
    lead = ("dn_w_in", "dn_conv_w", "dn_w_out", "sb_w_in", "sb_w_out")
    order = ("meta_tokens", "dn_norm_w", "dn_w_in", "dn_conv_w", "dn_a_log", "dn_dt_bias", "dn_out_norm_w", "dn_w_out",
             "sb_norm_w", "sb_w_in", "sb_q_norm_w", "sb_k_norm_w", "sb_w_out")
    fix = lambda nm, t: t[None] if nm in lead else t
    result = [loss, grad_x]
    for kind in range(4):
        result += [fix(nm, outs[nm][kind]) for nm in order]
    return tuple(result)
```

```python
import functools

import jax
import jax.numpy as jnp
from jax import lax
from jax.experimental import pallas as pl
from jax.experimental.pallas import tpu as pltpu

F32 = jnp.float32
BF16 = jnp.bfloat16
HD = 128
CH = 64
QB = 128
N_META = 16
PAD = 128
INERT = PAD - N_META
NDEV = 8
CONV_K = 4
EPS = 1e-6
VMEM_LIMIT = 56 * 1024 * 1024

ADAM_LR, ADAM_B1, ADAM_B2, ADAM_EPS, ADAM_WD, ADAM_STEP = 0.001, 0.9, 0.999, 1e-08, 0.01, 10
MESH = pl.DeviceIdType.MESH


def _cp(*sem):
    return pltpu.CompilerParams(dimension_semantics=sem, vmem_limit_bytes=VMEM_LIMIT)


def _tile(n, pref, mult=8):
    if n <= pref:
        return n
    for t in range(pref - pref % mult, 0, -mult):
        if n % t == 0:
            return t
    return n


def _silu(x):
    return x * jax.nn.sigmoid(x)


def _dsilu(x):
    s = jax.nn.sigmoid(x)
    return s * (1.0 + x * (1.0 - s))


def _dot(a, b, dims=((1,), (0,))):
    return lax.dot_general(a.astype(BF16), b.astype(BF16), (dims, ((), ())), preferred_element_type=F32)


def _dot_nt(a, b):
    return _dot(a, b, ((1,), (1,)))


def _dot_tn(a, b):
    return _dot(a, b, ((0,), (0,)))


def _dot_f32(a, b):
    dn = (((1,), (0,)), ((), ()))
    ah, bh = a.astype(BF16), b.astype(BF16)
    al, bl = (a - ah.astype(F32)).astype(BF16), (b - bh.astype(F32)).astype(BF16)
    mm = lambda x, y: lax.dot_general(x, y, dn, preferred_element_type=F32)
    return mm(ah, bh) + (mm(ah, bl) + mm(al, bh))


def _dot_split(a, m):
    hi = a.astype(BF16)
    lo = (a - hi.astype(F32)).astype(BF16)
    dn = (((1,), (0,)), ((), ()))
    return (lax.dot_general(hi, m, dn, preferred_element_type=F32)
            + lax.dot_general(lo, m, dn, preferred_element_type=F32))


def _iota(shape, dim):
    return lax.broadcasted_iota(jnp.int32, shape, dim)


def _col_to_row(col):
    n = col.shape[0]
    eye = _iota((n, n), 0) == _iota((n, n), 1)
    return jnp.sum(jnp.where(eye, col, 0.0), axis=0, keepdims=True)


def _row_to_col(row):
    n = row.shape[1]
    eye = _iota((n, n), 0) == _iota((n, n), 1)
    return jnp.sum(jnp.where(eye, row, 0.0), axis=1, keepdims=True)


def _exchange(arrs, gather, name):
    n = len(arrs)

    def body(*refs):
        ins, outs = refs[:n], refs[n:2 * n]
        send_sems, recv_sems, local_sems = refs[2 * n:]
        x, y, c = lax.axis_index("x"), lax.axis_index("y"), lax.axis_index("c")
        me = 4 * x + 2 * y + c
        sends = []
        for i in range(n):
            mine = pltpu.make_async_copy(ins[i] if gather else ins[i].at[me], outs[i].at[me], local_sems.at[i])
            mine.start()
            sends.append(mine)
        for k in range(1, NDEV):
            px, py, pc = x ^ (k >> 2), y ^ ((k >> 1) & 1), c ^ (k & 1)
            peer = 4 * px + 2 * py + pc
            for i in range(n):
                cp = pltpu.make_async_remote_copy(
                    src_ref=ins[i] if gather else ins[i].at[peer], dst_ref=outs[i].at[me],
                    send_sem=send_sems.at[i * NDEV + k], recv_sem=recv_sems.at[i * NDEV + k],
                    device_id=(px, py, pc), device_id_type=MESH)
                cp.start()
                sends.append(cp)
        for k in range(1, NDEV):
            px, py, pc = x ^ (k >> 2), y ^ ((k >> 1) & 1), c ^ (k & 1)
            peer = 4 * px + 2 * py + pc
            for i in range(n):
                pltpu.make_async_remote_copy(
                    src_ref=outs[i].at[peer], dst_ref=outs[i].at[peer],
                    send_sem=send_sems.at[i * NDEV + k], recv_sem=recv_sems.at[i * NDEV + k],
                    device_id=(px, py, pc), device_id_type=MESH).wait_recv()
        for i in range(n):
            sends[i].wait()
        for cp in sends[n:]:
            cp.wait_send()

    hbm = pl.BlockSpec(memory_space=pltpu.HBM)
    out_shape = tuple(jax.ShapeDtypeStruct(((NDEV,) + a.shape) if gather else a.shape, a.dtype) for a in arrs)
    return pl.pallas_call(
        body, name=name, out_shape=out_shape, in_specs=[hbm] * n, out_specs=tuple([hbm] * n),
        scratch_shapes=[pltpu.SemaphoreType.DMA((n * NDEV,)), pltpu.SemaphoreType.DMA((n * NDEV,)),
                        pltpu.SemaphoreType.DMA((n,))],
        compiler_params=pltpu.CompilerParams(has_side_effects=True),
    )(*arrs)


_HBM_SPEC = pl.BlockSpec(memory_space=pltpu.HBM)
_SEM_SPEC = pl.BlockSpec(memory_space=pltpu.SEMAPHORE)
_EFFECT = pltpu.SideEffectType.DATAFLOW_SIDE_EFFECTING


def _exchange_copy(k, i, src, land, send_sems, recv_sems, gather, for_wait):
    x, y, c = lax.axis_index("x"), lax.axis_index("y"), lax.axis_index("c")
    px, py, pc = x ^ (k >> 2), y ^ ((k >> 1) & 1), c ^ (k & 1)
    me, peer = 4 * x + 2 * y + c, 4 * px + 2 * py + pc
    return pltpu.make_async_remote_copy(
        src_ref=src if gather else src.at[peer], dst_ref=land.at[peer if for_wait else me],
        send_sem=send_sems.at[i * NDEV + k], recv_sem=recv_sems.at[i * NDEV + k],
        device_id=(px, py, pc), device_id_type=MESH)


def _exchange_start(arrs, gather, name, after=None):
    n = len(arrs)
    lands = [lax.empty(((NDEV,) + a.shape) if gather else a.shape, a.dtype) for a in arrs]
    extra = [] if after is None else [after]

    def body(*refs):
        ins, lnd = refs[:n], refs[n:2 * n]
        send_sems, recv_sems = refs[2 * n + len(extra)], refs[2 * n + len(extra) + 1]
        token = refs[-1]
        for k in range(1, NDEV):
            for i in range(n):
                _exchange_copy(k, i, ins[i], lnd[i], send_sems, recv_sems, gather, False).start()
        token[...] = jnp.zeros_like(token)

    sems = pltpu.SemaphoreType.DMA((n * NDEV,))
    outs = pl.pallas_call(
        body, name=name,
        out_shape=(sems, sems, *[pltpu.HBM(a.shape, a.dtype) for a in arrs], *[pltpu.HBM(l.shape, l.dtype) for l in lands],
                   jax.ShapeDtypeStruct((8, 128), F32)),
        in_specs=[_HBM_SPEC] * (2 * n) + [pl.BlockSpec(memory_space=pl.ANY)] * len(extra),
        out_specs=(_SEM_SPEC, _SEM_SPEC, *[_HBM_SPEC] * (2 * n), pl.BlockSpec(memory_space=pltpu.VMEM)),
        input_output_aliases={i: 2 + i for i in range(2 * n)},
        compiler_params=pltpu.CompilerParams(has_side_effects=_EFFECT),
    )(*[pltpu.with_memory_space_constraint(a, pltpu.HBM) for a in arrs],
      *[pltpu.with_memory_space_constraint(l, pltpu.HBM) for l in lands], *extra)
    return (outs[0], outs[1], list(outs[2:2 + n]), list(outs[2 + n:2 + 2 * n]), gather), outs[-1]


def _exchange_wait(state, after, name):
    send_sems, recv_sems, srcs, lands, gather = state
    n = len(srcs)

    def body(*refs):
        ins, lnd = refs[:n], refs[n:2 * n]
        send_sems, recv_sems = refs[2 * n], refs[2 * n + 1]
        for k in range(1, NDEV):
            for i in range(n):
                cp = _exchange_copy(k, i, ins[i], lnd[i], send_sems, recv_sems, gather, True)
                cp.wait_send()
                cp.wait_recv()

    outs = pl.pallas_call(
        body, name=name,
        out_shape=tuple(pltpu.HBM(a.shape, a.dtype) for a in srcs + lands),
        in_specs=[_HBM_SPEC] * (2 * n) + [_SEM_SPEC, _SEM_SPEC, pl.BlockSpec(memory_space=pl.ANY)],
        out_specs=tuple([_HBM_SPEC] * (2 * n)), input_output_aliases={i: i for i in range(2 * n)},
        compiler_params=pltpu.CompilerParams(has_side_effects=_EFFECT),
    )(*srcs, *lands, send_sems, recv_sems, after)
    return list(outs[n:]), list(outs[:n])


def _place_own(lands, srcs, gather, name):
    n = len(lands)

    def body(*refs):
        lnd_in, ins, outs, sems = refs[:n], refs[n:2 * n], refs[2 * n:3 * n], refs[3 * n]
        me = 4 * lax.axis_index("x") + 2 * lax.axis_index("y") + lax.axis_index("c")
        copies = [pltpu.make_async_copy(ins[i] if gather else ins[i].at[me], outs[i].at[me], sems.at[i]) for i in range(n)]
        for cp in copies:
            cp.start()
        for cp in copies:
            cp.wait()

    return pl.pallas_call(
        body, name=name, out_shape=tuple(jax.ShapeDtypeStruct(l.shape, l.dtype) for l in lands),
        in_specs=[_HBM_SPEC] * (2 * n), out_specs=tuple([_HBM_SPEC] * n),
        input_output_aliases={i: i for i in range(n)}, scratch_shapes=[pltpu.SemaphoreType.DMA((n,))],
        compiler_params=pltpu.CompilerParams(has_side_effects=True),
    )(*lands, *srcs)


def _matmul(a, b, *, mode, out_dtype, tm, tn, tk, name, add=None, after=None):
    if mode == "nn":
        (m, kd), (_, n) = a.shape, b.shape
    elif mode == "nt":
        (m, kd), (n, _) = a.shape, b.shape
    else:
        (kd, m), (_, n) = a.shape, b.shape
    tm, tn, tk = _tile(m, tm, 16), _tile(n, tn, 128), _tile(kd, tk, 128)
    nk = kd // tk
    a_spec = pl.BlockSpec((tk, tm), lambda i, j, k: (k, i)) if mode == "tn" else pl.BlockSpec((tm, tk), lambda i, j, k: (i, k))
    b_spec = pl.BlockSpec((tn, tk), lambda i, j, k: (j, k)) if mode == "nt" else pl.BlockSpec((tk, tn), lambda i, j, k: (k, j))
    o_spec = pl.BlockSpec((tm, tn), lambda i, j, k: (i, j))
    dims = {"nn": ((1,), (0,)), "nt": ((1,), (1,)), "tn": ((0,), (0,))}[mode]

    def body(*refs, nk):
        a_ref, b_ref = refs[0], refs[1]
        o_ref, acc_ref = refs[-2], refs[-1]
        k = pl.program_id(2)

        @pl.when(k == 0)
        def _():
            acc_ref[...] = jnp.zeros_like(acc_ref)

        acc_ref[...] += lax.dot_general(a_ref[...], b_ref[...], (dims, ((), ())), preferred_element_type=F32)

        @pl.when(k == nk - 1)
        def _():
            r = acc_ref[...]
            if add is not None:
                r = r + refs[2][...]
            o_ref[...] = r.astype(o_ref.dtype)

    ins, specs = [a, b], [a_spec, b_spec]
    if add is not None:
        ins.append(add)
        specs.append(o_spec)
    if after is not None:
        ins.append(after)
        specs.append(pl.BlockSpec(after.shape, lambda i, j, k: (0, 0)))
    return pl.pallas_call(
        functools.partial(body, nk=nk), name=name, grid=(m // tm, n // tn, nk),
        in_specs=specs, out_specs=o_spec, out_shape=jax.ShapeDtypeStruct((m, n), out_dtype),
        scratch_shapes=[pltpu.VMEM((tm, tn), F32)], compiler_params=_cp("parallel", "parallel", "arbitrary"),
    )(*ins)


def _rms_fwd(h, w, name):
    lp, d = h.shape
    tm = _tile(lp, 384)

    def body(h_ref, w_ref, o_ref):
        xf = h_ref[...]
        r = lax.rsqrt(jnp.mean(xf * xf, axis=-1, keepdims=True) + EPS)
        o_ref[...] = (xf * r * w_ref[...]).astype(o_ref.dtype)

    return pl.pallas_call(
        body, name=name, grid=(lp // tm,),
        in_specs=[pl.BlockSpec((tm, d), lambda i: (i, 0)), pl.BlockSpec((1, d), lambda i: (0, 0))],
        out_specs=pl.BlockSpec((tm, d), lambda i: (i, 0)), out_shape=jax.ShapeDtypeStruct((lp, d), BF16),
        compiler_params=_cp("parallel"))(h, w)


def _rms_bwd(h, w, dhn, dres, name):
    lp, d = h.shape
    tm = _tile(lp, 192)

    def body(h_ref, w_ref, dy_ref, dres_ref, dh_ref, dhb_ref, dw_ref):
        xf = h_ref[...]
        r = lax.rsqrt(jnp.mean(xf * xf, axis=-1, keepdims=True) + EPS)
        xhat = xf * r
        dy = dy_ref[...]
        dxhat = dy * w_ref[...]
        dx = r * (dxhat - xhat * jnp.mean(dxhat * xhat, axis=-1, keepdims=True))
        dh = dres_ref[...] + dx
        dh_ref[...] = dh
        dhb_ref[...] = dh.astype(BF16)

        @pl.when(pl.program_id(0) == 0)
        def _():
            dw_ref[...] = jnp.zeros_like(dw_ref)

        dw_ref[...] += jnp.sum(dy * xhat, axis=0, keepdims=True)

    row = pl.BlockSpec((tm, d), lambda i: (i, 0))
    vec = pl.BlockSpec((1, d), lambda i: (0, 0))
    return pl.pallas_call(
        body, name=name, grid=(lp // tm,), in_specs=[row, vec, row, row], out_specs=(row, row, vec),
        out_shape=(jax.ShapeDtypeStruct((lp, d), F32), jax.ShapeDtypeStruct((lp, d), BF16),
                   jax.ShapeDtypeStruct((1, d), F32)),
        compiler_params=_cp("arbitrary"))(h, w, dhn, dres)


def _conv_pre(xx, w, rows, off):
    acc = None
    for j in range(CONV_K):
        sh = CONV_K - 1 - j
        term = (pltpu.roll(xx, sh, 0) if sh else xx)[off:off + rows] * w[j]
        acc = term if acc is None else acc + term
    return acc


def _conv_fwd(proj, conv_w, ncols, name):
    lp = proj.shape[0]
    tm, tc = _tile(lp, 384), _tile(ncols, 1024, 128)
    hb = tm // 8

    def body(x_ref, xb_ref, w_ref, o_ref):
        before = jnp.where(pl.program_id(0) > 0, xb_ref[...], 0.0)
        xx = jnp.concatenate([before, x_ref[...]], axis=0)
        o_ref[...] = _silu(_conv_pre(xx, [w_ref[j:j + 1, :] for j in range(CONV_K)], tm, 8))

    return pl.pallas_call(
        body, name=name, grid=(lp // tm, ncols // tc),
        in_specs=[pl.BlockSpec((tm, tc), lambda i, j: (i, j)),
                  pl.BlockSpec((8, tc), lambda i, j: (jnp.maximum(i * hb - 1, 0), j)),
                  pl.BlockSpec((CONV_K, tc), lambda i, j: (0, j))],
        out_specs=pl.BlockSpec((tm, tc), lambda i, j: (i, j)),
        out_shape=jax.ShapeDtypeStruct((lp, ncols), F32), compiler_params=_cp("parallel", "parallel"))(proj, proj, conv_w)


def _conv_bwd(proj, col0, conv_w, dact, name):
    lp, ncols = dact.shape
    tm, tc = _tile(lp, 384), _tile(ncols, 512, 128)
    hb, nt, cb0 = tm // 8, lp // tm, col0 // tc
    assert col0 % tc == 0

    def body(x_ref, xb_ref, xa_ref, d_ref, da_ref, w_ref, dx_ref, dw_ref):
        i = pl.program_id(1)
        before = jnp.where(i > 0, xb_ref[...], 0.0)
        last = i == nt - 1
        xx = jnp.concatenate([before, x_ref[...], jnp.where(last, 0.0, xa_ref[...])], axis=0)
        w = [w_ref[j:j + 1, :] for j in range(CONV_K)]
        pre = _conv_pre(xx, w, tm + 8, 8)
        dd = jnp.concatenate([d_ref[...], jnp.where(last, 0.0, da_ref[...])], axis=0)
        dpre = dd * _dsilu(pre)
        dx = None
        for j in range(CONV_K):
            sh = CONV_K - 1 - j
            term = (pltpu.roll(dpre, tm + 8 - sh, 0) if sh else dpre)[:tm] * w[j]
            dx = term if dx is None else dx + term
        dx_ref[...] = dx.astype(BF16)

        @pl.when(i == 0)
        def _():
            dw_ref[...] = jnp.zeros_like(dw_ref)

        for j in range(CONV_K):
            sh = CONV_K - 1 - j
            xs = (pltpu.roll(xx, sh, 0) if sh else xx)[8:8 + tm]
            dw_ref[j:j + 1, :] += jnp.sum(dpre[:tm] * xs, axis=0, keepdims=True)

    return pl.pallas_call(
        body, name=name, grid=(ncols // tc, nt),
        in_specs=[pl.BlockSpec((tm, tc), lambda j, i: (i, cb0 + j)),
                  pl.BlockSpec((8, tc), lambda j, i: (jnp.maximum(i * hb - 1, 0), cb0 + j)),
                  pl.BlockSpec((8, tc), lambda j, i: (jnp.minimum((i + 1) * hb, nt * hb - 1), cb0 + j)),
                  pl.BlockSpec((tm, tc), lambda j, i: (i, j)),
                  pl.BlockSpec((8, tc), lambda j, i: (jnp.minimum((i + 1) * hb, nt * hb - 1), j)),
                  pl.BlockSpec((CONV_K, tc), lambda j, i: (0, j))],
        out_specs=(pl.BlockSpec((tm, tc), lambda j, i: (i, j)), pl.BlockSpec((CONV_K, tc), lambda j, i: (0, j))),
        out_shape=(jax.ShapeDtypeStruct((lp, ncols), BF16), jax.ShapeDtypeStruct((CONV_K, ncols), F32)),
        compiler_params=_cp("parallel", "arbitrary"))(proj, proj, proj, dact, dact, conv_w)


def _softplus(x):
    return jnp.maximum(x, 0.0) + jnp.log(1.0 + jnp.exp(-jnp.abs(x)))


def _gates_fwd(gl, a_log2, dt_bias2, name):
    lp, w2 = gl.shape
    hv = w2 // 2

    def body(gl_ref, al_ref, dt_ref, o_ref):
        x = gl_ref[...]
        live = _iota((lp, 1), 0) >= INERT
        is_beta = _iota((1, w2), 1) < hv
        g = -jnp.exp(al_ref[...]) * _softplus(x + dt_ref[...])
        o_ref[...] = jnp.where(live, jnp.where(is_beta, jax.nn.sigmoid(x), g), 0.0)

    return pl.pallas_call(body, name=name, out_shape=jax.ShapeDtypeStruct((lp, w2), F32))(gl, a_log2, dt_bias2)


def _gates_bwd(gl, a_log2, dt_bias2, dbg, name):
    lp, w2 = gl.shape
    hv = w2 // 2

    def body(gl_ref, al_ref, dt_ref, d_ref, dl_ref, dal_ref, ddt_ref):
        x = gl_ref[...]
        live = _iota((lp, 1), 0) >= INERT
        is_beta = _iota((1, w2), 1) < hv
        d = jnp.where(live, d_ref[...], 0.0)
        beta = jax.nn.sigmoid(x)
        ea = jnp.exp(al_ref[...])
        u = x + dt_ref[...]
        dg = jnp.where(is_beta, 0.0, d)
        dal_ref[...] = jnp.sum(dg * (-ea) * _softplus(u), axis=0, keepdims=True)
        du = dg * (-ea) * jax.nn.sigmoid(u)
        ddt_ref[...] = jnp.sum(du, axis=0, keepdims=True)
        dl_ref[...] = jnp.where(is_beta, d * beta * (1.0 - beta), du).astype(BF16)

    vec = jax.ShapeDtypeStruct((1, w2), F32)
    return pl.pallas_call(
        body, name=name, out_shape=(jax.ShapeDtypeStruct((lp, w2), BF16), vec, vec))(gl, a_log2, dt_bias2, dbg)


def _l2n(x):
    r = lax.rsqrt(jnp.sum(x * x, axis=-1, keepdims=True) + EPS)
    return x * r, r


def _tri_inverse(mats):
    eye = (_iota((CH, CH), 0) == _iota((CH, CH), 1)).astype(F32)
    ts = [eye - a for a in mats]
    ps = [_dot_f32(a, a) for a in mats]
    n = 2
    while n < CH:
        ts = [t + _dot_f32(t, p) for t, p in zip(ts, ps)]
        n *= 2
        if n < CH:
            ps = [_dot_f32(p, p) for p in ps]
    return ts


def _chunk_local(qn, kn, v, b_row, g_row):
    ri, ci = _iota((CH, CH), 0), _iota((CH, CH), 1)
    incl, strict = ri >= ci, ri > ci
    gam_col = jnp.sum(jnp.where(incl, g_row, 0.0), axis=1, keepdims=True)
    gam_row = _col_to_row(gam_col)
    b_col = _row_to_col(b_row)
    dec = jnp.exp(jnp.where(incl, gam_col - gam_row, -jnp.inf))
    eg = jnp.exp(gam_col)
    gl = jnp.sum(g_row, axis=1, keepdims=True)
    ekd = jnp.exp(gl - gam_col)
    kb = kn * b_col
    a = jnp.where(strict, _dot_nt(kb, kn) * dec, 0.0)
    p = jnp.where(incl, _dot_nt(qn, kn) * dec, 0.0)
    return dict(dec=dec, eg=eg, ekd=ekd, kb=kb, vb=v * b_col, a=a, kbg=kb * eg, p=p, qd=qn * eg, kd=kn * ekd,
                b_col=b_col, incl=incl, strict=strict)


def _chunks_per_step(nc):
    return max(g for g in (1, 2, 3, 6) if nc % g == 0)


def _heads_per_step(hv):
    return min(hv, 8)


def _delta_local(act, gates, key_w, name):
    lp = act.shape[0]
    hk, nc = key_w // HD, lp // CH
    hv = 2 * hk
    g = _chunks_per_step(nc)
    tr = g * CH

    def body(q_ref, k_ref, v_ref, g_ref, u_ref, w_ref, qd_ref, kd_ref, p_ref, t_ref):
        items = []
        for j in range(g):
            rows = slice(j * CH, (j + 1) * CH)
            qn = _l2n(q_ref[rows, :])[0] * (HD ** -0.5)
            kn = _l2n(k_ref[rows, :])[0]
            for e in range(2):
                cols = slice(e * HD, (e + 1) * HD)
                r = _chunk_local(qn, kn, v_ref[rows, cols], g_ref[0, j, e:e + 1, :], g_ref[0, j, 2 + e:3 + e, :])
                qd_ref[rows, cols] = r["qd"].astype(BF16)
                kd_ref[rows, cols] = r["kd"].astype(BF16)
                p_ref[e, rows, :] = r["p"].astype(BF16)
                items.append((rows, cols, e, r["a"], r["vb"].astype(BF16), r["kbg"].astype(BF16)))
        ts = [t.astype(BF16) for t in _tri_inverse([it[3] for it in items])]
        us = [_dot(t, it[4]) for t, it in zip(ts, items)]
        ws = [_dot(t, it[5]) for t, it in zip(ts, items)]
        for (rows, cols, e, _, _, _), t, u, w in zip(items, ts, us, ws):
            u_ref[rows, cols] = u
            w_ref[rows, cols] = w.astype(BF16)
            t_ref[e, rows, :] = t

    wide = pl.BlockSpec((tr, 2 * HD), lambda h, c: (c, h))
    sq = pl.BlockSpec((2, tr, CH), lambda h, c: (h, c, 0))
    wshape = lambda dt: jax.ShapeDtypeStruct((lp, hv * HD), dt)
    sshape = jax.ShapeDtypeStruct((hv, lp, CH), BF16)
    return pl.pallas_call(
        body, name=name, grid=(hk, nc // g),
        in_specs=[pl.BlockSpec((tr, HD), lambda h, c: (c, h)),
                  pl.BlockSpec((tr, HD), lambda h, c: (c, hk + h)),
                  pl.BlockSpec((tr, 2 * HD), lambda h, c: (c, hk + h)),
                  pl.BlockSpec((1, g, 8, CH), lambda h, c: (h, c, 0, 0))],
        out_specs=(wide, wide, wide, wide, sq, sq),
        out_shape=(wshape(F32), wshape(BF16), wshape(BF16), wshape(BF16), sshape, sshape),
        compiler_params=_cp("parallel", "parallel"))(act, act, act, gates)


def _chunk_decay(g_ref, e):
    return jnp.exp(jnp.sum(g_ref[e // 2, 0, 2 + e % 2:3 + e % 2, :], axis=1, keepdims=True))


def _delta_scan(u, w, qd, kd, p, gates, name):
    lp, val = u.shape
    hv, nc = val // HD, lp // CH
    nh = _heads_per_step(hv)

    def body(u_ref, w_ref, qd_ref, kd_ref, p_ref, g_ref, o_ref, vn_ref, st_ref, s_scr):
        @pl.when(pl.program_id(1) == 0)
        def _():
            s_scr[...] = jnp.zeros_like(s_scr)

        heads = range(nh)
        col = lambda e: slice(e * HD, (e + 1) * HD)
        ss = [s_scr[e] for e in heads]
        sb = [s.astype(BF16) for s in ss]
        for e in heads:
            st_ref[0, e] = ss[e]
        ws = [_dot(w_ref[:, col(e)], sb[e]) for e in heads]
        qs = [_dot(qd_ref[:, col(e)], sb[e]) for e in heads]
        vns = [(u_ref[:, col(e)] - ws[e]).astype(BF16) for e in heads]
        pv = [_dot(p_ref[e], vns[e]) for e in heads]
        kv = [_dot_tn(kd_ref[:, col(e)], vns[e]) for e in heads]
        for e in heads:
            o_ref[:, col(e)] = qs[e] + pv[e]
            s_scr[e] = _chunk_decay(g_ref, e) * ss[e] + kv[e]
            vn_ref[:, col(e)] = vns[e]

    wide = pl.BlockSpec((CH, nh * HD), lambda h, c: (c, h))
    return pl.pallas_call(
        body, name=name, grid=(hv // nh, nc),
        in_specs=[wide, wide, wide, wide, pl.BlockSpec((nh, CH, CH), lambda h, c: (h, c, 0)),
                  pl.BlockSpec((nh // 2, 1, 8, CH), lambda h, c: (h, c, 0, 0))],
        out_specs=(wide, wide, pl.BlockSpec((1, nh, HD, HD), lambda h, c: (c, h, 0, 0))),
        out_shape=(jax.ShapeDtypeStruct((lp, val), F32), jax.ShapeDtypeStruct((lp, val), BF16),
                   jax.ShapeDtypeStruct((nc, hv, HD, HD), F32)),
        scratch_shapes=[pltpu.VMEM((nh, HD, HD), F32)],
        compiler_params=_cp("parallel", "arbitrary"))(u, w, qd, kd, p, gates)


def _delta_scan_bwd(do, w, qd, kd, p, vn, states, gates, name):
    lp, val = do.shape
    hv, nc = val // HD, lp // CH
    nh = _heads_per_step(hv)

    def body(do_ref, w_ref, qd_ref, kd_ref, p_ref, vn_ref, st_ref, g_ref,
             dvn_ref, dw_ref, dqd_ref, dkd_ref, dp_ref, sd_ref, ds_scr):
        @pl.when(pl.program_id(1) == 0)
        def _():
            ds_scr[...] = jnp.zeros_like(ds_scr)

        incl = _iota((CH, CH), 0) >= _iota((CH, CH), 1)
        heads = range(nh)
        col = lambda e: slice(e * HD, (e + 1) * HD)
        ss = [st_ref[0, e] for e in heads]
        dss = [ds_scr[e] for e in heads]
        sb = [s.astype(BF16) for s in ss]
        dsb = [d.astype(BF16) for d in dss]
        dos = [do_ref[:, col(e)].astype(BF16) for e in heads]
        egl = [_chunk_decay(g_ref, e) for e in heads]
        pdo = [_dot_tn(p_ref[e], dos[e]) for e in heads]
        kds = [_dot(kd_ref[:, col(e)], dsb[e]) for e in heads]
        qdo = [_dot_tn(qd_ref[:, col(e)], dos[e]) for e in heads]
        dqd = [_dot_nt(dos[e], sb[e]) for e in heads]
        dkd = [_dot_nt(vn_ref[:, col(e)], dsb[e]) for e in heads]
        dpp = [_dot_nt(dos[e], vn_ref[:, col(e)]) for e in heads]
        dvn = [(pdo[e] + kds[e]).astype(BF16) for e in heads]
        wdv = [_dot_tn(w_ref[:, col(e)], dvn[e]) for e in heads]
        dws = [_dot_nt(dvn[e], sb[e]) for e in heads]
        for e in heads:
            ds_scr[e] = qdo[e] + egl[e] * dss[e] - wdv[e]
            dvn_ref[:, col(e)] = dvn[e]
            dw_ref[:, col(e)] = (-dws[e]).astype(BF16)
            dqd_ref[:, col(e)] = dqd[e]
            dkd_ref[:, col(e)] = dkd[e]
            dp_ref[e] = jnp.where(incl, dpp[e], 0.0)
            sd_ref[0, 0, e:e + 1, :] = jnp.broadcast_to(egl[e] * jnp.sum(ss[e] * dss[e], keepdims=True), (1, HD))

    rev = lambda c: nc - 1 - c
    wide = pl.BlockSpec((CH, nh * HD), lambda h, c: (rev(c), h))
    sq = pl.BlockSpec((nh, CH, CH), lambda h, c: (h, rev(c), 0))
    wshape = lambda dt: jax.ShapeDtypeStruct((lp, val), dt)
    return pl.pallas_call(
        body, name=name, grid=(hv // nh, nc),
        in_specs=[wide, wide, wide, wide, sq, wide, pl.BlockSpec((1, nh, HD, HD), lambda h, c: (rev(c), h, 0, 0)),
                  pl.BlockSpec((nh // 2, 1, 8, CH), lambda h, c: (h, rev(c), 0, 0))],
        out_specs=(wide, wide, wide, wide, sq, pl.BlockSpec((1, 1, nh, HD), lambda h, c: (h, rev(c), 0, 0))),
        out_shape=(wshape(BF16), wshape(BF16), wshape(F32), wshape(F32), jax.ShapeDtypeStruct((hv, lp, CH), F32),
                   jax.ShapeDtypeStruct((hv // nh, nc, nh, HD), F32)),
        scratch_shapes=[pltpu.VMEM((nh, HD, HD), F32)],
        compiler_params=_cp("parallel", "arbitrary"))(do, w, qd, kd, p, vn, states, gates)


def _delta_local_bwd(act, gates, t, dvn, dw, dqd, dkd, dp, key_w, name):
    lp = act.shape[0]
    hk, nc = key_w // HD, lp // CH
    hv = 2 * hk
    g = _chunks_per_step(nc)
    tr = g * CH
    scale = HD ** -0.5

    def body(q_ref, k_ref, v_ref, g_ref, t_ref, dvn_ref, dw_ref, dqd_ref, dkd_ref, dp_ref, dq_ref, dk_ref, dv_ref, dg_ref):
        ri, ci = _iota((CH, CH), 0), _iota((CH, CH), 1)
        inner = lambda x, z: jnp.sum(x * z, axis=1, keepdims=True)
        norms, items = [], []
        for j in range(g):
            rows = slice(j * CH, (j + 1) * CH)
            qh, qr = _l2n(q_ref[rows, :])
            kn, kr = _l2n(k_ref[rows, :])
            qn = qh * scale
            norms.append((rows, qh, qr, kn, kr, qn))
            dg_ref[0, j, 4:8, :] = jnp.zeros((4, CH), F32)
            for e in range(2):
                cols = slice(e * HD, (e + 1) * HD)
                v = v_ref[rows, cols]
                r = _chunk_local(qn, kn, v, g_ref[0, j, e:e + 1, :], g_ref[0, j, 2 + e:3 + e, :])
                items.append(dict(r, j=j, e=e, rows=rows, cols=cols, v=v, kn=kn, qn=qn, t=t_ref[e, rows, :],
                                  dvn=dvn_ref[rows, cols], dw=dw_ref[rows, cols]))
        for it in items:
            it["dt"] = _dot_nt(it["dvn"], it["vb"]) + _dot_nt(it["dw"], it["kbg"])
            it["dvb"] = _dot_tn(it["t"], it["dvn"])
            it["dkbg"] = _dot_tn(it["t"], it["dw"])
        for it in items:
            it["x"] = _dot_tn(it["t"], it["dt"])
        for it in items:
            it["da"] = -jnp.where(it["strict"], _dot_nt(it["x"], it["t"]), 0.0)
        for it in items:
            dp = dp_ref[it["e"], it["rows"], :]
            it["gmat"] = it["da"] * it["a"] + dp * it["p"]
            mm, nn = (it["da"] * it["dec"]).astype(BF16), (dp * it["dec"]).astype(BF16)
            it["dkb"] = _dot(mm, it["kn"]) + it["dkbg"] * it["eg"]
            it["dkn"] = _dot_tn(mm, it["kb"]) + _dot_tn(nn, it["qn"])
            it["dqn"] = _dot(nn, it["kn"])
        for it in items:
            j, e, rows, cols = it["j"], it["e"], it["rows"], it["cols"]
            dqd, dkd, gmat, dkb = dqd_ref[rows, cols], dkd_ref[rows, cols], it["gmat"], it["dkb"]
            it["dkn"] = it["dkn"] + dkd * it["ekd"] + it["b_col"] * dkb
            it["dqn"] = it["dqn"] + dqd * it["eg"]
            dkd_kd = inner(dkd, it["kd"])
            dgam = (jnp.sum(gmat, axis=1, keepdims=True) - _row_to_col(jnp.sum(gmat, axis=0, keepdims=True))
                    + inner(dqd, it["qd"]) + inner(it["dkbg"], it["kbg"]) - dkd_kd)
            dgl = jnp.max(g_ref[0, j, 4 + e:5 + e, :], axis=1, keepdims=True) + jnp.sum(dkd_kd, keepdims=True)
            dgam = dgam + jnp.where(_iota((CH, 1), 0) == CH - 1, dgl, 0.0)
            dg_ref[0, j, 2 + e:3 + e, :] = jnp.sum(jnp.where(ri >= ci, dgam, 0.0), axis=0, keepdims=True)
            dg_ref[0, j, e:e + 1, :] = _col_to_row(inner(dkb, it["kn"]) + inner(it["dvb"], it["v"]))
            dv_ref[rows, cols] = it["b_col"] * it["dvb"]
        for j, (rows, qh, qr, kn, kr, _) in enumerate(norms):
            dqh = (items[2 * j]["dqn"] + items[2 * j + 1]["dqn"]) * scale
            dkn = items[2 * j]["dkn"] + items[2 * j + 1]["dkn"]
            dq_ref[rows, :] = qr * (dqh - qh * jnp.sum(dqh * qh, axis=1, keepdims=True))
            dk_ref[rows, :] = kr * (dkn - kn * jnp.sum(dkn * kn, axis=1, keepdims=True))

    narrow = pl.BlockSpec((tr, HD), lambda h, c: (c, h))
    wide = pl.BlockSpec((tr, 2 * HD), lambda h, c: (c, h))
    sq = pl.BlockSpec((2, tr, CH), lambda h, c: (h, c, 0))
    gate = pl.BlockSpec((1, g, 8, CH), lambda h, c: (h, c, 0, 0))
    return pl.pallas_call(
        body, name=name, grid=(hk, nc // g),
        in_specs=[narrow, pl.BlockSpec((tr, HD), lambda h, c: (c, hk + h)),
                  pl.BlockSpec((tr, 2 * HD), lambda h, c: (c, hk + h)), gate, sq, wide, wide, wide, wide, sq],
        out_specs=(narrow, narrow, wide, gate),
        out_shape=(jax.ShapeDtypeStruct((lp, key_w), F32), jax.ShapeDtypeStruct((lp, key_w), F32),
                   jax.ShapeDtypeStruct((lp, hv * HD), F32), jax.ShapeDtypeStruct((hk, nc, 8, CH), F32)),
        compiler_params=_cp("parallel", "parallel"))(act, act, act, gates, t, dvn, dw, dqd, dkd, dp)


def _outnorm_fwd(o, proj, z_col0, w, name):
    lp, val = o.shape
    tm, zb = _tile(lp, 1056), z_col0 // HD

    def body(o_ref, z_ref, w_ref, y_ref):
        xf = o_ref[...]
        r = lax.rsqrt(jnp.mean(xf * xf, axis=-1, keepdims=True) + EPS)
        y_ref[...] = (xf * r * w_ref[...] * _silu(z_ref[...])).astype(BF16)

    return pl.pallas_call(
        body, name=name, grid=(lp // tm, val // HD),
        in_specs=[pl.BlockSpec((tm, HD), lambda i, h: (i, h)), pl.BlockSpec((tm, HD), lambda i, h: (i, zb + h)),
                  pl.BlockSpec((1, HD), lambda i, h: (0, 0))],
        out_specs=pl.BlockSpec((tm, HD), lambda i, h: (i, h)), out_shape=jax.ShapeDtypeStruct((lp, val), BF16),
        compiler_params=_cp("parallel", "parallel"))(o, proj, w)


def _outnorm_bwd(o, proj, z_col0, w, dy, name):
    lp, val = o.shape
    tm, zb = _tile(lp, 1056), z_col0 // HD

    def body(o_ref, z_ref, w_ref, dy_ref, do_ref, dz_ref, dw_ref):
        xf, z, d = o_ref[...], z_ref[...], dy_ref[...]
        r = lax.rsqrt(jnp.mean(xf * xf, axis=-1, keepdims=True) + EPS)
        xhat = xf * r
        dn = d * _silu(z)
        dz_ref[...] = (d * xhat * w_ref[...] * _dsilu(z)).astype(BF16)
        dxhat = dn * w_ref[...]
        do_ref[...] = r * (dxhat - xhat * jnp.mean(dxhat * xhat, axis=-1, keepdims=True))

        @pl.when((pl.program_id(0) == 0) & (pl.program_id(1) == 0))
        def _():
            dw_ref[...] = jnp.zeros_like(dw_ref)

        dw_ref[...] += jnp.sum(dn * xhat, axis=0, keepdims=True)

    blk = pl.BlockSpec((tm, HD), lambda i, h: (i, h))
    vec = pl.BlockSpec((1, HD), lambda i, h: (0, 0))
    return pl.pallas_call(
        body, name=name, grid=(lp // tm, val // HD),
        in_specs=[blk, pl.BlockSpec((tm, HD), lambda i, h: (i, zb + h)), vec, blk], out_specs=(blk, blk, vec),
        out_shape=(jax.ShapeDtypeStruct((lp, val), F32), jax.ShapeDtypeStruct((lp, val), BF16),
                   jax.ShapeDtypeStruct((1, HD), F32)),
        compiler_params=_cp("arbitrary", "arbitrary"))(o, proj, w, dy)


def _qknorm_fwd(proj, qw, kw, width, name):
    lp = proj.shape[0]
    tm, nh = _tile(lp, 1056), width // HD

    def body(q_ref, k_ref, v_ref, qw_ref, kw_ref, qo_ref, ko_ref, vo_ref):
        for x_ref, w_ref, o_ref in ((q_ref, qw_ref, qo_ref), (k_ref, kw_ref, ko_ref)):
            xf = x_ref[...]
            r = lax.rsqrt(jnp.mean(xf * xf, axis=-1, keepdims=True) + EPS)
            o_ref[...] = (xf * r * w_ref[...]).astype(BF16)
        vo_ref[...] = v_ref[...].astype(BF16)

    blk = lambda off: pl.BlockSpec((tm, HD), lambda i, h: (i, off + h))
    vec = pl.BlockSpec((1, HD), lambda i, h: (0, 0))
    shp = jax.ShapeDtypeStruct((lp, width), BF16)
    return pl.pallas_call(
        body, name=name, grid=(lp // tm, nh), in_specs=[blk(0), blk(nh), blk(2 * nh), vec, vec],
        out_specs=(blk(0), blk(0), blk(0)), out_shape=(shp, shp, shp),
        compiler_params=_cp("parallel", "parallel"))(proj, proj, proj, qw, kw)


def _qknorm_bwd(proj, qw, kw, dqn, dkn, width, name):
    lp = proj.shape[0]
    tm, nh = _tile(lp, 1056), width // HD

    def body(q_ref, k_ref, qw_ref, kw_ref, dqn_ref, dkn_ref, dq_ref, dk_ref, dqw_ref, dkw_ref):
        first = (pl.program_id(0) == 0) & (pl.program_id(1) == 0)
        for x_ref, w_ref, dy_ref, dx_ref, dw_ref in ((q_ref, qw_ref, dqn_ref, dq_ref, dqw_ref),
                                                       (k_ref, kw_ref, dkn_ref, dk_ref, dkw_ref)):
            xf, dy = x_ref[...], dy_ref[...]
            r = lax.rsqrt(jnp.mean(xf * xf, axis=-1, keepdims=True) + EPS)
            xhat = xf * r
            dxhat = dy * w_ref[...]
            dx_ref[...] = (r * (dxhat - xhat * jnp.mean(dxhat * xhat, axis=-1, keepdims=True))).astype(BF16)

            @pl.when(first)
            def _():
                dw_ref[...] = jnp.zeros_like(dw_ref)

            dw_ref[...] += jnp.sum(dy * xhat, axis=0, keepdims=True)

    blk = lambda off: pl.BlockSpec((tm, HD), lambda i, h: (i, off + h))
    vec = pl.BlockSpec((1, HD), lambda i, h: (0, 0))
    shp = jax.ShapeDtypeStruct((lp, width), BF16)
    vshp = jax.ShapeDtypeStruct((1, HD), F32)
    return pl.pallas_call(
        body, name=name, grid=(lp // tm, nh), in_specs=[blk(0), blk(nh), vec, vec, blk(0), blk(0)],
        out_specs=(blk(0), blk(0), vec, vec), out_shape=(shp, shp, vshp, vshp),
        compiler_params=_cp("arbitrary", "arbitrary"))(proj, proj, qw, kw, dqn, dkn)


def _sb_tq(lp):
    return 3 * QB if lp % (3 * QB) == 0 else QB


def _sb_scores(qk, t_idx, kb):
    z = qk * (HD ** -0.5)
    s_idx = kb * QB + _iota((1, QB), 1)
    valid = (s_idx < t_idx) & (s_idx >= INERT)
    sp = jnp.log(1.0 + jnp.exp(-jnp.abs(z)))
    lsz = jnp.minimum(z, 0.0) - sp
    lk = jnp.where(valid, -jnp.maximum(z, 0.0) - sp, 0.0)
    return valid, lsz, lk


def _sb_fwd(qn, kn, vv, proj, gate_col0, name):
    lp, width = qn.shape
    tq = _sb_tq(lp)
    nh, nq, gb, nsub = width // HD, lp // tq, gate_col0 // HD, tq // QB

    def body(q_ref, k_ref, v_ref, g_ref, o_ref, og_ref, tot_ref):
        qb = pl.program_id(1)
        q = q_ref[...]
        t_idx = qb * tq + _iota((tq, 1), 0)
        upper = (_iota((QB, QB), 0) > _iota((QB, QB), 1)).astype(BF16)

        def step(i, carry):
            run, acc = carry
            kbs = [(qb - i) * nsub + sub for sub in reversed(range(nsub))]
            rows = [pl.ds(pl.multiple_of(kb * QB, QB), QB) for kb in kbs]
            qks = [_dot_nt(q, k_ref[r, :]) for r in rows]
            scores = [_sb_scores(qk, t_idx, kb) for qk, kb in zip(qks, kbs)]
            sums = [_dot_split(lk, upper) for _, _, lk in scores]
            probs = []
            for (valid, lsz, lk), part in zip(scores, sums):
                probs.append(jnp.where(valid, jnp.exp(lsz + part + run), 0.0).astype(BF16))
                run = run + jnp.sum(lk, axis=1, keepdims=True)
            for a, r in zip(probs, rows):
                acc = acc + _dot(a, v_ref[r, :])
            return run, acc

        run, acc = lax.fori_loop(0, qb + 1, step, (jnp.zeros((tq, 1), F32), jnp.zeros((tq, HD), F32)))
        o_ref[...] = acc
        og_ref[...] = (acc * _silu(g_ref[...])).astype(BF16)
        tot_ref[0, 0] = _col_to_row(run)

    full = pl.BlockSpec((lp, HD), lambda h, i: (0, h))
    blk = pl.BlockSpec((tq, HD), lambda h, i: (i, h))
    return pl.pallas_call(
        body, name=name, grid=(nh, nq),
        in_specs=[blk, full, full, pl.BlockSpec((tq, HD), lambda h, i: (i, gb + h))],
        out_specs=(blk, blk, pl.BlockSpec((1, 1, 1, tq), lambda h, i: (h, i, 0, 0))),
        out_shape=(jax.ShapeDtypeStruct((lp, width), F32), jax.ShapeDtypeStruct((lp, width), BF16),
                   jax.ShapeDtypeStruct((nh, nq, 1, tq), F32)),
        compiler_params=_cp("parallel", "parallel"))(qn, kn, vv, proj)


def _sb_bwd(qn, kn, vv, proj, gate_col0, att, tot, dog, name):
    lp, width = qn.shape
    tq = _sb_tq(lp)
    nh, nq, gb, nsub = width // HD, lp // tq, gate_col0 // HD, tq // QB
    scale = HD ** -0.5

    def body(q_ref, k_ref, v_ref, g_ref, att_ref, tot_ref, dog_ref, dq_ref, dk_ref, dv_ref, dg_ref, dk_acc, dv_acc):
        qb = pl.program_id(1)

        @pl.when(qb == 0)
        def _():
            dk_acc[...] = jnp.zeros_like(dk_acc)
            dv_acc[...] = jnp.zeros_like(dv_acc)

        q, gate, dg_out = q_ref[...], g_ref[...], dog_ref[...]
        d_o = (dg_out * _silu(gate)).astype(BF16)
        dg_ref[...] = (dg_out * att_ref[...] * _dsilu(gate)).astype(BF16)
        total = _row_to_col(tot_ref[0, 0])
        t_idx = qb * tq + _iota((tq, 1), 0)
        ri, ci = _iota((QB, QB), 0), _iota((QB, QB), 1)
        lower_incl = (ri <= ci).astype(BF16)
        lower_excl = (ri < ci).astype(BF16)

        def step(kg, carry):
            run, erun, dq = carry
            kbs = [kg * nsub + sub for sub in range(nsub)]
            rows = [pl.ds(pl.multiple_of(kb * QB, QB), QB) for kb in kbs]
            qks = [_dot_nt(q, k_ref[r, :]) for r in rows]
            dprobs = [_dot_nt(d_o, v_ref[r, :]) for r in rows]
            scores = [_sb_scores(qk, t_idx, kb) for qk, kb in zip(qks, kbs)]
            sums = [_dot_split(lk, lower_incl) for _, _, lk in scores]
            probs, es = [], []
            for (valid, lsz, lk), part, dprob in zip(scores, sums, dprobs):
                a = jnp.where(valid, jnp.exp(lsz + (total - run - part)), 0.0)
                probs.append(a.astype(BF16))
                es.append(a * dprob)
                run = run + jnp.sum(lk, axis=1, keepdims=True)
            esums = [_dot_split(e, lower_excl) for e in es]
            for a, r in zip(probs, rows):
                dv_acc[r, :] += _dot_tn(a, d_o)
            dzs = []
            for (valid, lsz, _), e, part in zip(scores, es, esums):
                sig = jnp.exp(lsz)
                dzs.append((jnp.where(valid, e * (1.0 - sig) - sig * (erun + part), 0.0) * scale).astype(BF16))
                erun = erun + jnp.sum(e, axis=1, keepdims=True)
            for dz, r in zip(dzs, rows):
                dk_acc[r, :] += _dot_tn(dz, q)
                dq = dq + _dot(dz, k_ref[r, :])
            return run, erun, dq

        zero = jnp.zeros((tq, 1), F32)
        _, _, dq = lax.fori_loop(0, qb + 1, step, (zero, zero, jnp.zeros((tq, HD), F32)))
        dq_ref[...] = dq

        @pl.when(qb == nq - 1)
        def _():
            dk_ref[...] = dk_acc[...]
            dv_ref[...] = dv_acc[...].astype(BF16)

    full = pl.BlockSpec((lp, HD), lambda h, i: (0, h))
    blk = pl.BlockSpec((tq, HD), lambda h, i: (i, h))
    return pl.pallas_call(
        body, name=name, grid=(nh, nq),
        in_specs=[blk, full, full, pl.BlockSpec((tq, HD), lambda h, i: (i, gb + h)), blk,
                  pl.BlockSpec((1, 1, 1, tq), lambda h, i: (h, i, 0, 0)), blk],
        out_specs=(blk, full, full, blk),
        out_shape=(jax.ShapeDtypeStruct((lp, width), F32), jax.ShapeDtypeStruct((lp, width), F32),
                   jax.ShapeDtypeStruct((lp, width), BF16), jax.ShapeDtypeStruct((lp, width), BF16)),
        scratch_shapes=[pltpu.VMEM((lp, HD), F32), pltpu.VMEM((lp, HD), F32)],
        compiler_params=_cp("parallel", "arbitrary"))(qn, kn, vv, proj, att, tot, dog)


def _loss_head(h, target, name):
    lp, d = h.shape
    tm = _tile(PAD, 128)
    nt = lp // tm
    npad = PAD // tm

    def body(h_ref, t_ref, l_ref, dh_ref, dhb_ref):
        i = pl.program_id(0)
        err = jnp.where(i >= npad, h_ref[...] - t_ref[...], 0.0)
        dh = err * (1.0 / d)
        dh_ref[...] = dh
        dhb_ref[...] = dh.astype(BF16)

        @pl.when(i == 0)
        def _():
            l_ref[...] = jnp.zeros_like(l_ref)

        l_ref[...] += (0.5 / d) * jnp.sum(err * err, keepdims=True)

    row = pl.BlockSpec((tm, d), lambda i: (i, 0))
    return pl.pallas_call(
        body, name=name, grid=(nt,),
        in_specs=[row, pl.BlockSpec((tm, d), lambda i: (jnp.maximum(i - npad, 0), 0))],
        out_specs=(pl.BlockSpec((1, 1), lambda i: (0, 0)), row, row),
        out_shape=(jax.ShapeDtypeStruct((1, 1), F32), jax.ShapeDtypeStruct((lp, d), F32),
                   jax.ShapeDtypeStruct((lp, d), BF16)),
        compiler_params=_cp("arbitrary"))(h, target)


def _adamw(parts, w, m, v, name):
    r, c = w.shape
    tr = _tile(r, max(8, (1 << 18) // c // 8 * 8))

    def body(p_ref, w_ref, m_ref, v_ref, g_ref, d_ref, mo_ref, vo_ref):
        g = p_ref[0].astype(F32)
        for k in range(1, NDEV):
            g = g + p_ref[k].astype(F32)
        mn = ADAM_B1 * m_ref[...] + (1.0 - ADAM_B1) * g
        vn = ADAM_B2 * v_ref[...] + (1.0 - ADAM_B2) * jnp.square(g)
        m_hat = mn / (1.0 - ADAM_B1 ** ADAM_STEP)
        v_hat = vn / (1.0 - ADAM_B2 ** ADAM_STEP)
        g_ref[...] = g
        d_ref[...] = -ADAM_LR * (m_hat / (jnp.sqrt(v_hat) + ADAM_EPS) + ADAM_WD * w_ref[...])
        mo_ref[...] = mn
        vo_ref[...] = vn

    blk = pl.BlockSpec((tr, c), lambda i: (i, 0))
    shp = jax.ShapeDtypeStruct((r, c), F32)
    return pl.pallas_call(
        body, name=name, grid=(r // tr,), in_specs=[pl.BlockSpec((NDEV, tr, c), lambda i: (0, i, 0)), blk, blk, blk],
        out_specs=(blk, blk, blk, blk), out_shape=(shp, shp, shp, shp), compiler_params=_cp("parallel"))(parts, w, m, v)


def _unshard_cols(g):
    return jnp.transpose(g, (1, 0, 2)).reshape(g.shape[1], NDEV * g.shape[2])


def _shard_cols(a):
    r, c = a.shape
    return jnp.transpose(a.reshape(r, NDEV, c // NDEV), (1, 0, 2))


def kernel(x, meta_tokens, dn_norm_w, dn_w_in, dn_conv_w, dn_a_log, dn_dt_bias, dn_out_norm_w, dn_w_out, sb_norm_w, sb_w_in, sb_q_norm_w, sb_k_norm_w, sb_w_out, loss_target, m_meta_tokens, m_dn_norm_w, m_dn_w_in, m_dn_conv_w, m_dn_a_log, m_dn_dt_bias, m_dn_out_norm_w, m_dn_w_out, m_sb_norm_w, m_sb_w_in, m_sb_q_norm_w, m_sb_k_norm_w, m_sb_w_out, v_meta_tokens, v_dn_norm_w, v_dn_w_in, v_dn_conv_w, v_dn_a_log, v_dn_dt_bias, v_dn_out_norm_w, v_dn_w_out, v_sb_norm_w, v_sb_w_in, v_sb_q_norm_w, v_sb_k_norm_w, v_sb_w_out):
    seq, d = x.shape[1], x.shape[2]
    lp = PAD + seq
    key_w = d
    val_w = 2 * d
    hv = val_w // HD
    conv_w_cols = 2 * key_w + val_w
    main_w = conv_w_cols + val_w
    sb_w = d
    nc = lp // CH
    hk = key_w // HD

    st_a, tok_a = _exchange_start([dn_w_in[0].astype(BF16)], True, "gather_w_in0_start")
    st_b, tok_b = _exchange_start([dn_w_out[0].astype(BF16), sb_w_in[0].astype(BF16), sb_w_out[0].astype(BF16)],
                                  True, "gather_w_rest_start", after=tok_a)
    (g_meta, g_sbn, g_conv) = _exchange([meta_tokens + tok_b[:1, :1], sb_norm_w, dn_conv_w[0]], True, "gather_vectors")
    meta = _unshard_cols(g_meta)
    sbn_w = _unshard_cols(g_sbn)
    conv_w = _unshard_cols(g_conv)
    h0 = jnp.concatenate([jnp.zeros((INERT, d), F32), meta, x[0]], axis=0)
    hn0 = _rms_fwd(h0, dn_norm_w, "dn_norm")
    lands_a, srcs_a = _exchange_wait(st_a, hn0, "gather_w_in0_wait")
    (g_in0,) = _place_own(lands_a, srcs_a, True, "gather_w_in0_own")
    w_in0 = _unshard_cols(g_in0)
    w_in0_main, w_in0_gate = w_in0[:, :main_w], w_in0[:, main_w:]

    proj0 = _matmul(hn0, w_in0_main, mode="nn", out_dtype=F32, tm=1056, tn=512, tk=4096, name="dn_in_proj")
    gl0 = _matmul(hn0, w_in0_gate, mode="nn", out_dtype=F32, tm=1056, tn=512, tk=4096, name="dn_gate_proj")
    act0 = _conv_fwd(proj0, conv_w, conv_w_cols, "dn_conv")
    a_log2 = jnp.concatenate([jnp.zeros_like(dn_a_log), dn_a_log], axis=1)
    dt_bias2 = jnp.concatenate([jnp.zeros_like(dn_dt_bias), dn_dt_bias], axis=1)
    bg = _gates_fwd(gl0, a_log2, dt_bias2, "dn_gates")
    pack = lambda t: jnp.transpose(t.reshape(nc, CH, hk, 2), (2, 0, 3, 1))
    gates = jnp.concatenate([pack(bg[:, :hv]), pack(bg[:, hv:]), jnp.zeros((hk, nc, 4, CH), F32)], axis=2)
    u0, w0, qd0, kd0, p0, t0 = _delta_local(act0, gates, key_w, "dn_delta_local")
    o0, vn0, states = _delta_scan(u0, w0, qd0, kd0, p0, gates, "dn_delta_scan")
    o0g = _outnorm_fwd(o0, proj0, conv_w_cols, dn_out_norm_w, "dn_out_norm")
    lands_b, srcs_b = _exchange_wait(st_b, o0g, "gather_w_rest_wait")
    g_out0, g_in1, g_out1 = _place_own(lands_b, srcs_b, True, "gather_w_rest_own")
    w_out0 = g_out0.reshape(val_w, d)
    w_in1 = _unshard_cols(g_in1)
    w_out1 = g_out1.reshape(sb_w, d)
    h1 = _matmul(o0g, w_out0, mode="nn", out_dtype=F32, tm=1056, tn=512, tk=4096, name="dn_out_proj", add=h0)
    hn1 = _rms_fwd(h1, sbn_w, "sb_norm")
    proj1 = _matmul(hn1, w_in1, mode="nn", out_dtype=F32, tm=1056, tn=512, tk=4096, name="sb_in_proj")
    qn1, kn1, vv1 = _qknorm_fwd(proj1, sb_q_norm_w, sb_k_norm_w, sb_w, "sb_qk_norm")
    att1, o1g, tot1 = _sb_fwd(qn1, kn1, vv1, proj1, 3 * sb_w, "sb_attn")
    h2 = _matmul(o1g, w_out1, mode="nn", out_dtype=F32, tm=1056, tn=512, tk=4096, name="sb_out_proj", add=h1)
    loss_part, dh2, dh2b = _loss_head(h2, loss_target[0], "loss_head")
    loss = lax.psum(loss_part[0, 0], ("x", "y", "c"))

    p_out1 = _matmul(o1g, dh2b, mode="tn", out_dtype=BF16, tm=1024, tn=512, tk=lp, name="sb_out_wgrad")
    do1g = _matmul(dh2b, w_out1, mode="nt", out_dtype=F32, tm=1056, tn=512, tk=4096, name="sb_out_dgrad")
    dqn1, dkn1, dv1, dgate1 = _sb_bwd(qn1, kn1, vv1, proj1, 3 * sb_w, att1, tot1, do1g, "sb_attn_bwd")
    dq1, dk1, d_qw, d_kw = _qknorm_bwd(proj1, sb_q_norm_w, sb_k_norm_w, dqn1, dkn1, sb_w, "sb_qk_norm_bwd")
    dproj1 = jnp.concatenate([dq1, dk1, dv1, dgate1], axis=1)
    p_in1 = _matmul(hn1, dproj1, mode="tn", out_dtype=BF16, tm=1024, tn=512, tk=lp, name="sb_in_wgrad")
    st_s1, tok_s1 = _exchange_start([p_out1.reshape(NDEV, sb_w // NDEV, d), _shard_cols(p_in1)], False, "scatter_sb_start")
    dhn1 = _matmul(dproj1, w_in1, mode="nt", out_dtype=F32, tm=1056, tn=512, tk=4096, name="sb_in_dgrad", after=tok_s1)
    dh1, dh1b, d_sbn = _rms_bwd(h1, sbn_w, dhn1, dh2, "sb_norm_bwd")

    p_out0 = _matmul(o0g, dh1b, mode="tn", out_dtype=BF16, tm=1024, tn=512, tk=lp, name="dn_out_wgrad")
    st_s2, tok_s2 = _exchange_start([p_out0.reshape(NDEV, val_w // NDEV, d)], False, "scatter_dn_out_start")
    do0g = _matmul(dh1b, w_out0, mode="nt", out_dtype=F32, tm=1056, tn=512, tk=4096, name="dn_out_dgrad", after=tok_s2)
    do0, dz0, d_onw = _outnorm_bwd(o0, proj0, conv_w_cols, dn_out_norm_w, do0g, "dn_out_norm_bwd")
    dvn0, dw0, dqd0, dkd0, dp0, sd0 = _delta_scan_bwd(do0, w0, qd0, kd0, p0, vn0, states, gates, "dn_delta_scan_bwd")
    sd_rows = jnp.transpose(jnp.transpose(sd0[..., 0], (0, 2, 1)).reshape(hk, 2, nc), (0, 2, 1))
    gates_b = jnp.concatenate([gates[:, :, :4], jnp.broadcast_to(sd_rows[..., None], (hk, nc, 2, CH)),
                               jnp.zeros((hk, nc, 2, CH), F32)], axis=2)
    dq_act, dk_act, dv_act, dgates = _delta_local_bwd(act0, gates_b, t0, dvn0, dw0, dqd0, dkd0, dp0, key_w,
                                                      "dn_delta_local_bwd")
    unpack = lambda t: jnp.transpose(t, (1, 3, 0, 2)).reshape(lp, hv)
    dbg = jnp.concatenate([unpack(dgates[:, :, 0:2]), unpack(dgates[:, :, 2:4])], axis=1)
    dgl0, d_alog2, d_dtb2 = _gates_bwd(gl0, a_log2, dt_bias2, dbg, "dn_gates_bwd")
    d_alog, d_dtb = d_alog2[:, hv:], d_dtb2[:, hv:]
    dxq, dcw_q = _conv_bwd(proj0, 0, conv_w[:, :key_w], dq_act, "dn_conv_bwd_q")
    dxk, dcw_k = _conv_bwd(proj0, key_w, conv_w[:, key_w:2 * key_w], dk_act, "dn_conv_bwd_k")
    dxv, dcw_v = _conv_bwd(proj0, 2 * key_w, conv_w[:, 2 * key_w:], dv_act, "dn_conv_bwd_v")
    dproj0 = jnp.concatenate([dxq, dxk, dxv, dz0], axis=1)
    p_in0_main = _matmul(hn0, dproj0, mode="tn", out_dtype=BF16, tm=1024, tn=512, tk=lp, name="dn_in_wgrad")
    p_in0_gate = _matmul(hn0, dgl0, mode="tn", out_dtype=BF16, tm=1024, tn=512, tk=lp, name="dn_gate_wgrad")
    p_in0 = _shard_cols(jnp.concatenate([p_in0_main, p_in0_gate], axis=1))
    st_s3, tok_s3 = _exchange_start([p_in0], False, "scatter_dn_in_start")
    dhn0 = _matmul(dgl0, w_in0_gate, mode="nt", out_dtype=F32, tm=1056, tn=512, tk=4096, name="dn_gate_dgrad", after=tok_s3)
    dhn0 = _matmul(dproj0, w_in0_main, mode="nt", out_dtype=F32, tm=1056, tn=512, tk=4096, name="dn_in_dgrad", add=dhn0)
    dh0, _, d_dnn = _rms_bwd(h0, dn_norm_w, dhn0, dh1, "dn_norm_bwd")
    grad_x = dh0[PAD:][None]

    p_conv = _shard_cols(jnp.concatenate([dcw_q, dcw_k, dcw_v], axis=1))
    (r_meta, r_sbn, r_conv) = _exchange([_shard_cols(dh0[INERT:PAD]), _shard_cols(d_sbn), p_conv], False, "scatter_vector_grads")
    small = jnp.concatenate([d_dnn, d_alog, d_dtb, d_onw, d_qw, d_kw], axis=1)
    (r_small,) = _exchange([small], True, "gather_replicated_grads")
    outs = {}
    outs["meta_tokens"] = _adamw(r_meta, meta_tokens, m_meta_tokens, v_meta_tokens, "adamw_meta")
    outs["dn_conv_w"] = _adamw(r_conv, dn_conv_w[0], m_dn_conv_w[0], v_dn_conv_w[0], "adamw_dn_conv")
    outs["sb_norm_w"] = _adamw(r_sbn, sb_norm_w, m_sb_norm_w, v_sb_norm_w, "adamw_sb_norm")
    lands_s1, srcs_s1 = _exchange_wait(st_s1, r_small, "scatter_sb_wait")
    r_out1, r_in1 = _place_own(lands_s1, srcs_s1, False, "scatter_sb_own")
    outs["sb_w_in"] = _adamw(r_in1, sb_w_in[0], m_sb_w_in[0], v_sb_w_in[0], "adamw_sb_w_in")
    outs["sb_w_out"] = _adamw(r_out1, sb_w_out[0], m_sb_w_out[0], v_sb_w_out[0], "adamw_sb_w_out")
    lands_s2, srcs_s2 = _exchange_wait(st_s2, outs["sb_w_in"][1], "scatter_dn_out_wait")
    (r_out0,) = _place_own(lands_s2, srcs_s2, False, "scatter_dn_out_own")
    outs["dn_w_out"] = _adamw(r_out0, dn_w_out[0], m_dn_w_out[0], v_dn_w_out[0], "adamw_dn_w_out")
    lands_s3, srcs_s3 = _exchange_wait(st_s3, outs["dn_w_out"][1], "scatter_dn_in_wait")
    (r_in0,) = _place_own(lands_s3, srcs_s3, False, "scatter_dn_in_own")
    outs["dn_w_in"] = _adamw(r_in0, dn_w_in[0], m_dn_w_in[0], v_dn_w_in[0], "adamw_dn_w_in")
    cat = lambda *a: jnp.concatenate(a, axis=1)
    rep = _adamw(r_small, cat(dn_norm_w, dn_a_log, dn_dt_bias, dn_out_norm_w, sb_q_norm_w, sb_k_norm_w),
                 cat(m_dn_norm_w, m_dn_a_log, m_dn_dt_bias, m_dn_out_norm_w, m_sb_q_norm_w, m_sb_k_norm_w),
                 cat(v_dn_norm_w, v_dn_a_log, v_dn_dt_bias, v_dn_out_norm_w, v_sb_q_norm_w, v_sb_k_norm_w),
                 "adamw_replicated")
    off = 0
    for nm, wd in (("dn_norm_w", d), ("dn_a_log", hv), ("dn_dt_bias", hv), ("dn_out_norm_w", HD),
                   ("sb_q_norm_w", HD), ("sb_k_norm_w", HD)):
        outs[nm] = tuple(t[:, off:off + wd] for t in rep)
        off += wd
    lead = ("dn_w_in", "dn_conv_w", "dn_w_out", "sb_w_in", "sb_w_out")
    order = ("meta_tokens", "dn_norm_w", "dn_w_in", "dn_conv_w", "dn_a_log", "dn_dt_bias", "dn_out_norm_w", "dn_w_out",
             "sb_norm_w", "sb_w_in", "sb_q_norm_w", "sb_k_norm_w", "sb_w_out")
    fix = lambda nm, t: t[None] if nm in lead else t
    result = [loss, grad_x]
    for kind in range(4):
        result += [fix(nm, outs[nm][kind]) for nm in order]
    return tuple(result)
```

```python
import functools

import jax
import jax.numpy as jnp
from jax import lax
from jax.experimental import pallas as pl
from jax.experimental.pallas import tpu as pltpu

F32 = jnp.float32
BF16 = jnp.bfloat16
HD = 128
CH = 64
QB = 128
N_META = 16
PAD = 128
INERT = PAD - N_META
NDEV = 8
CONV_K = 4
EPS = 1e-6
VMEM_LIMIT = 56 * 1024 * 1024

ADAM_LR, ADAM_B1, ADAM_B2, ADAM_EPS, ADAM_WD, ADAM_STEP = 0.001, 0.9, 0.999, 1e-08, 0.01, 10
MESH = pl.DeviceIdType.MESH


def _cp(*sem):
    return pltpu.CompilerParams(dimension_semantics=sem, vmem_limit_bytes=VMEM_LIMIT)


def _tile(n, pref, mult=8):
    if n <= pref:
        return n
    for t in range(pref - pref % mult, 0, -mult):
        if n % t == 0:
            return t
    return n


def _silu(x):
    return x * jax.nn.sigmoid(x)


def _dsilu(x):
    s = jax.nn.sigmoid(x)
    return s * (1.0 + x * (1.0 - s))


def _dot(a, b, dims=((1,), (0,))):
    return lax.dot_general(a.astype(BF16), b.astype(BF16), (dims, ((), ())), preferred_element_type=F32)


def _dot_nt(a, b):
    return _dot(a, b, ((1,), (1,)))


def _dot_tn(a, b):
    return _dot(a, b, ((0,), (0,)))


def _dot_f32(a, b):
    dn = (((1,), (0,)), ((), ()))
    ah, bh = a.astype(BF16), b.astype(BF16)
    al, bl = (a - ah.astype(F32)).astype(BF16), (b - bh.astype(F32)).astype(BF16)
    mm = lambda x, y: lax.dot_general(x, y, dn, preferred_element_type=F32)
    return mm(ah, bh) + (mm(ah, bl) + mm(al, bh))


def _dot_split(a, m):
    hi = a.astype(BF16)
    lo = (a - hi.astype(F32)).astype(BF16)
    dn = (((1,), (0,)), ((), ()))
    return (lax.dot_general(hi, m, dn, preferred_element_type=F32)
            + lax.dot_general(lo, m, dn, preferred_element_type=F32))


def _iota(shape, dim):
    return lax.broadcasted_iota(jnp.int32, shape, dim)


def _col_to_row(col):
    n = col.shape[0]
    eye = _iota((n, n), 0) == _iota((n, n), 1)
    return jnp.sum(jnp.where(eye, col, 0.0), axis=0, keepdims=True)


def _row_to_col(row):
    n = row.shape[1]
    eye = _iota((n, n), 0) == _iota((n, n), 1)
    return jnp.sum(jnp.where(eye, row, 0.0), axis=1, keepdims=True)


def _exchange(arrs, gather, name):
    n = len(arrs)

    def body(*refs):
        ins, outs = refs[:n], refs[n:2 * n]
        send_sems, recv_sems, local_sems = refs[2 * n:]
        x, y, c = lax.axis_index("x"), lax.axis_index("y"), lax.axis_index("c")
        me = 4 * x + 2 * y + c
        sends = []
        for i in range(n):
            mine = pltpu.make_async_copy(ins[i] if gather else ins[i].at[me], outs[i].at[me], local_sems.at[i])
            mine.start()
            sends.append(mine)
        for k in range(1, NDEV):
            px, py, pc = x ^ (k >> 2), y ^ ((k >> 1) & 1), c ^ (k & 1)
            peer = 4 * px + 2 * py + pc
            for i in range(n):
                cp = pltpu.make_async_remote_copy(
                    src_ref=ins[i] if gather else ins[i].at[peer], dst_ref=outs[i].at[me],
                    send_sem=send_sems.at[i * NDEV + k], recv_sem=recv_sems.at[i * NDEV + k],
                    device_id=(px, py, pc), device_id_type=MESH)
                cp.start()
                sends.append(cp)
        for k in range(1, NDEV):
            px, py, pc = x ^ (k >> 2), y ^ ((k >> 1) & 1), c ^ (k & 1)
            peer = 4 * px + 2 * py + pc
            for i in range(n):
                pltpu.make_async_remote_copy(
                    src_ref=outs[i].at[peer], dst_ref=outs[i].at[peer],
                    send_sem=send_sems.at[i * NDEV + k], recv_sem=recv_sems.at[i * NDEV + k],
                    device_id=(px, py, pc), device_id_type=MESH).wait_recv()
        for i in range(n):
            sends[i].wait()
        for cp in sends[n:]:
            cp.wait_send()

    hbm = pl.BlockSpec(memory_space=pltpu.HBM)
    out_shape = tuple(jax.ShapeDtypeStruct(((NDEV,) + a.shape) if gather else a.shape, a.dtype) for a in arrs)
    return pl.pallas_call(
        body, name=name, out_shape=out_shape, in_specs=[hbm] * n, out_specs=tuple([hbm] * n),
        scratch_shapes=[pltpu.SemaphoreType.DMA((n * NDEV,)), pltpu.SemaphoreType.DMA((n * NDEV,)),
                        pltpu.SemaphoreType.DMA((n,))],
        compiler_params=pltpu.CompilerParams(has_side_effects=True),
    )(*arrs)


_HBM_SPEC = pl.BlockSpec(memory_space=pltpu.HBM)
_SEM_SPEC = pl.BlockSpec(memory_space=pltpu.SEMAPHORE)
_EFFECT = pltpu.SideEffectType.DATAFLOW_SIDE_EFFECTING


def _exchange_copy(k, i, src, land, send_sems, recv_sems, gather, for_wait):
    x, y, c = lax.axis_index("x"), lax.axis_index("y"), lax.axis_index("c")
    px, py, pc = x ^ (k >> 2), y ^ ((k >> 1) & 1), c ^ (k & 1)
    me, peer = 4 * x + 2 * y + c, 4 * px + 2 * py + pc
    slot = (peer if for_wait else me) if gather else k - 1
    return pltpu.make_async_remote_copy(
        src_ref=src if gather else src.at[peer], dst_ref=land.at[slot],
        send_sem=send_sems.at[i * NDEV + k], recv_sem=recv_sems.at[i * NDEV + k],
        device_id=(px, py, pc), device_id_type=MESH)


def _exchange_start(arrs, gather, name, after=None):
    n = len(arrs)
    lands = [lax.empty(((NDEV,) + a.shape) if gather else ((NDEV - 1,) + a.shape[1:]), a.dtype) for a in arrs]
    extra = [] if after is None else [after]

    def body(*refs):
        ins, lnd = refs[:n], refs[n:2 * n]
        send_sems, recv_sems = refs[2 * n + len(extra)], refs[2 * n + len(extra) + 1]
        token = refs[-1]
        for k in range(1, NDEV):
            for i in range(n):
                _exchange_copy(k, i, ins[i], lnd[i], send_sems, recv_sems, gather, False).start()
        token[...] = jnp.zeros_like(token)

    sems = pltpu.SemaphoreType.DMA((n * NDEV,))
    outs = pl.pallas_call(
        body, name=name,
        out_shape=(sems, sems, *[pltpu.HBM(a.shape, a.dtype) for a in arrs], *[pltpu.HBM(l.shape, l.dtype) for l in lands],
                   jax.ShapeDtypeStruct((8, 128), F32)),
        in_specs=[_HBM_SPEC] * (2 * n) + [pl.BlockSpec(memory_space=pl.ANY)] * len(extra),
        out_specs=(_SEM_SPEC, _SEM_SPEC, *[_HBM_SPEC] * (2 * n), pl.BlockSpec(memory_space=pltpu.VMEM)),
        input_output_aliases={i: 2 + i for i in range(2 * n)},
        compiler_params=pltpu.CompilerParams(has_side_effects=_EFFECT),
    )(*[pltpu.with_memory_space_constraint(a, pltpu.HBM) for a in arrs],
      *[pltpu.with_memory_space_constraint(l, pltpu.HBM) for l in lands], *extra)
    return (outs[0], outs[1], list(outs[2:2 + n]), list(outs[2 + n:2 + 2 * n]), gather), outs[-1]


def _exchange_wait(state, after, name):
    send_sems, recv_sems, srcs, lands, gather = state
    n = len(srcs)

    def body(*refs):
        ins, lnd = refs[:n], refs[n:2 * n]
        send_sems, recv_sems = refs[2 * n], refs[2 * n + 1]
        for k in range(1, NDEV):
            for i in range(n):
                cp = _exchange_copy(k, i, ins[i], lnd[i], send_sems, recv_sems, gather, True)
                cp.wait_send()
                cp.wait_recv()

    outs = pl.pallas_call(
        body, name=name,
        out_shape=tuple(pltpu.HBM(a.shape, a.dtype) for a in srcs + lands),
        in_specs=[_HBM_SPEC] * (2 * n) + [_SEM_SPEC, _SEM_SPEC, pl.BlockSpec(memory_space=pl.ANY)],
        out_specs=tuple([_HBM_SPEC] * (2 * n)), input_output_aliases={i: i for i in range(2 * n)},
        compiler_params=pltpu.CompilerParams(has_side_effects=_EFFECT),
    )(*srcs, *lands, send_sems, recv_sems, after)
    return list(outs[n:]), list(outs[:n])


def _own_index():
    return 4 * lax.axis_index("x") + 2 * lax.axis_index("y") + lax.axis_index("c")


def _with_own(land, mine):
    return lax.dynamic_update_index_in_dim(land, mine, _own_index(), 0)


def _matmul(a, b, *, mode, out_dtype, tm, tn, tk, name, add=None, after=None):
    if mode == "nn":
        (m, kd), (_, n) = a.shape, b.shape
    elif mode == "nt":
        (m, kd), (n, _) = a.shape, b.shape
    else:
        (kd, m), (_, n) = a.shape, b.shape
    tm, tn, tk = _tile(m, tm, 16), _tile(n, tn, 128), _tile(kd, tk, 128)
    nk = kd // tk
    a_spec = pl.BlockSpec((tk, tm), lambda i, j, k: (k, i)) if mode == "tn" else pl.BlockSpec((tm, tk), lambda i, j, k: (i, k))
    b_spec = pl.BlockSpec((tn, tk), lambda i, j, k: (j, k)) if mode == "nt" else pl.BlockSpec((tk, tn), lambda i, j, k: (k, j))
    o_spec = pl.BlockSpec((tm, tn), lambda i, j, k: (i, j))
    dims = {"nn": ((1,), (0,)), "nt": ((1,), (1,)), "tn": ((0,), (0,))}[mode]

    def body(*refs, nk):
        a_ref, b_ref = refs[0], refs[1]
        o_ref, acc_ref = refs[-2], refs[-1]
        k = pl.program_id(2)

        @pl.when(k == 0)
        def _():
            acc_ref[...] = jnp.zeros_like(acc_ref)

        acc_ref[...] += lax.dot_general(a_ref[...], b_ref[...], (dims, ((), ())), preferred_element_type=F32)

        @pl.when(k == nk - 1)
        def _():
            r = acc_ref[...]
            if add is not None:
                r = r + refs[2][...]
            o_ref[...] = r.astype(o_ref.dtype)

    ins, specs = [a, b], [a_spec, b_spec]
    if add is not None:
        ins.append(add)
        specs.append(o_spec)
    if after is not None:
        ins.append(after)
        specs.append(pl.BlockSpec(after.shape, lambda i, j, k: (0, 0)))
    return pl.pallas_call(
        functools.partial(body, nk=nk), name=name, grid=(m // tm, n // tn, nk),
        in_specs=specs, out_specs=o_spec, out_shape=jax.ShapeDtypeStruct((m, n), out_dtype),
        scratch_shapes=[pltpu.VMEM((tm, tn), F32)], compiler_params=_cp("parallel", "parallel", "arbitrary"),
    )(*ins)


def _rms_fwd(h, w, name):
    lp, d = h.shape
    tm = _tile(lp, 384)

    def body(h_ref, w_ref, o_ref):
        xf = h_ref[...]
        r = lax.rsqrt(jnp.mean(xf * xf, axis=-1, keepdims=True) + EPS)
        o_ref[...] = (xf * r * w_ref[...]).astype(o_ref.dtype)

    return pl.pallas_call(
        body, name=name, grid=(lp // tm,),
        in_specs=[pl.BlockSpec((tm, d), lambda i: (i, 0)), pl.BlockSpec((1, d), lambda i: (0, 0))],
        out_specs=pl.BlockSpec((tm, d), lambda i: (i, 0)), out_shape=jax.ShapeDtypeStruct((lp, d), BF16),
        compiler_params=_cp("parallel"))(h, w)


def _rms_bwd(h, w, dhn, dres, name):
    lp, d = h.shape
    tm = _tile(lp, 192)

    def body(h_ref, w_ref, dy_ref, dres_ref, dh_ref, dhb_ref, dw_ref):
        xf = h_ref[...]
        r = lax.rsqrt(jnp.mean(xf * xf, axis=-1, keepdims=True) + EPS)
        xhat = xf * r
        dy = dy_ref[...]
        dxhat = dy * w_ref[...]
        dx = r * (dxhat - xhat * jnp.mean(dxhat * xhat, axis=-1, keepdims=True))
        dh = dres_ref[...] + dx
        dh_ref[...] = dh
        dhb_ref[...] = dh.astype(BF16)

        @pl.when(pl.program_id(0) == 0)
        def _():
            dw_ref[...] = jnp.zeros_like(dw_ref)

        dw_ref[...] += jnp.sum(dy * xhat, axis=0, keepdims=True)

    row = pl.BlockSpec((tm, d), lambda i: (i, 0))
    vec = pl.BlockSpec((1, d), lambda i: (0, 0))
    return pl.pallas_call(
        body, name=name, grid=(lp // tm,), in_specs=[row, vec, row, row], out_specs=(row, row, vec),
        out_shape=(jax.ShapeDtypeStruct((lp, d), F32), jax.ShapeDtypeStruct((lp, d), BF16),
                   jax.ShapeDtypeStruct((1, d), F32)),
        compiler_params=_cp("arbitrary"))(h, w, dhn, dres)


def _conv_pre(xx, w, rows, off):
    acc = None
    for j in range(CONV_K):
        sh = CONV_K - 1 - j
        term = (pltpu.roll(xx, sh, 0) if sh else xx)[off:off + rows] * w[j]
        acc = term if acc is None else acc + term
    return acc


def _conv_fwd(proj, conv_w, ncols, name):
    lp = proj.shape[0]
    tm, tc = _tile(lp, 384), _tile(ncols, 1024, 128)
    hb = tm // 8

    def body(x_ref, xb_ref, w_ref, o_ref):
        before = jnp.where(pl.program_id(0) > 0, xb_ref[...], 0.0)
        xx = jnp.concatenate([before, x_ref[...]], axis=0)
        o_ref[...] = _silu(_conv_pre(xx, [w_ref[j:j + 1, :] for j in range(CONV_K)], tm, 8))

    return pl.pallas_call(
        body, name=name, grid=(lp // tm, ncols // tc),
        in_specs=[pl.BlockSpec((tm, tc), lambda i, j: (i, j)),
                  pl.BlockSpec((8, tc), lambda i, j: (jnp.maximum(i * hb - 1, 0), j)),
                  pl.BlockSpec((CONV_K, tc), lambda i, j: (0, j))],
        out_specs=pl.BlockSpec((tm, tc), lambda i, j: (i, j)),
        out_shape=jax.ShapeDtypeStruct((lp, ncols), F32), compiler_params=_cp("parallel", "parallel"))(proj, proj, conv_w)


def _conv_bwd(proj, col0, conv_w, dact, name):
    lp, ncols = dact.shape
    tm, tc = _tile(lp, 384), _tile(ncols, 512, 128)
    hb, nt, cb0 = tm // 8, lp // tm, col0 // tc
    assert col0 % tc == 0

    def body(x_ref, xb_ref, xa_ref, d_ref, da_ref, w_ref, dx_ref, dw_ref):
        i = pl.program_id(1)
        before = jnp.where(i > 0, xb_ref[...], 0.0)
        last = i == nt - 1
        xx = jnp.concatenate([before, x_ref[...], jnp.where(last, 0.0, xa_ref[...])], axis=0)
        w = [w_ref[j:j + 1, :] for j in range(CONV_K)]
        pre = _conv_pre(xx, w, tm + 8, 8)
        dd = jnp.concatenate([d_ref[...], jnp.where(last, 0.0, da_ref[...])], axis=0)
        dpre = dd * _dsilu(pre)
        dx = None
        for j in range(CONV_K):
            sh = CONV_K - 1 - j
            term = (pltpu.roll(dpre, tm + 8 - sh, 0) if sh else dpre)[:tm] * w[j]
            dx = term if dx is None else dx + term
        dx_ref[...] = dx.astype(BF16)

        @pl.when(i == 0)
        def _():
            dw_ref[...] = jnp.zeros_like(dw_ref)

        for j in range(CONV_K):
            sh = CONV_K - 1 - j
            xs = (pltpu.roll(xx, sh, 0) if sh else xx)[8:8 + tm]
            dw_ref[j:j + 1, :] += jnp.sum(dpre[:tm] * xs, axis=0, keepdims=True)

    return pl.pallas_call(
        body, name=name, grid=(ncols // tc, nt),
        in_specs=[pl.BlockSpec((tm, tc), lambda j, i: (i, cb0 + j)),
                  pl.BlockSpec((8, tc), lambda j, i: (jnp.maximum(i * hb - 1, 0), cb0 + j)),
                  pl.BlockSpec((8, tc), lambda j, i: (jnp.minimum((i + 1) * hb, nt * hb - 1), cb0 + j)),
                  pl.BlockSpec((tm, tc), lambda j, i: (i, j)),
                  pl.BlockSpec((8, tc), lambda j, i: (jnp.minimum((i + 1) * hb, nt * hb - 1), j)),
                  pl.BlockSpec((CONV_K, tc), lambda j, i: (0, j))],
        out_specs=(pl.BlockSpec((tm, tc), lambda j, i: (i, j)), pl.BlockSpec((CONV_K, tc), lambda j, i: (0, j))),
        out_shape=(jax.ShapeDtypeStruct((lp, ncols), BF16), jax.ShapeDtypeStruct((CONV_K, ncols), F32)),
        compiler_params=_cp("parallel", "arbitrary"))(proj, proj, proj, dact, dact, conv_w)


def _softplus(x):
    return jnp.maximum(x, 0.0) + jnp.log(1.0 + jnp.exp(-jnp.abs(x)))


def _gates_fwd(gl, a_log2, dt_bias2, name):
    lp, w2 = gl.shape
    hv = w2 // 2

    def body(gl_ref, al_ref, dt_ref, o_ref):
        x = gl_ref[...]
        live = _iota((lp, 1), 0) >= INERT
        is_beta = _iota((1, w2), 1) < hv
        g = -jnp.exp(al_ref[...]) * _softplus(x + dt_ref[...])
        o_ref[...] = jnp.where(live, jnp.where(is_beta, jax.nn.sigmoid(x), g), 0.0)

    return pl.pallas_call(body, name=name, out_shape=jax.ShapeDtypeStruct((lp, w2), F32))(gl, a_log2, dt_bias2)


def _gates_bwd(gl, a_log2, dt_bias2, dbg, name):
    lp, w2 = gl.shape
    hv = w2 // 2

    def body(gl_ref, al_ref, dt_ref, d_ref, dl_ref, dal_ref, ddt_ref):
        x = gl_ref[...]
        live = _iota((lp, 1), 0) >= INERT
        is_beta = _iota((1, w2), 1) < hv
        d = jnp.where(live, d_ref[...], 0.0)
        beta = jax.nn.sigmoid(x)
        ea = jnp.exp(al_ref[...])
        u = x + dt_ref[...]
        dg = jnp.where(is_beta, 0.0, d)
        dal_ref[...] = jnp.sum(dg * (-ea) * _softplus(u), axis=0, keepdims=True)
        du = dg * (-ea) * jax.nn.sigmoid(u)
        ddt_ref[...] = jnp.sum(du, axis=0, keepdims=True)
        dl_ref[...] = jnp.where(is_beta, d * beta * (1.0 - beta), du).astype(BF16)

    vec = jax.ShapeDtypeStruct((1, w2), F32)
    return pl.pallas_call(
        body, name=name, out_shape=(jax.ShapeDtypeStruct((lp, w2), BF16), vec, vec))(gl, a_log2, dt_bias2, dbg)


def _l2n(x):
    r = lax.rsqrt(jnp.sum(x * x, axis=-1, keepdims=True) + EPS)
    return x * r, r


def _tri_inverse(mats):
    eye = (_iota((CH, CH), 0) == _iota((CH, CH), 1)).astype(F32)
    ts = [eye - a for a in mats]
    ps = [_dot_f32(a, a) for a in mats]
    n = 2
    while n < CH:
        ts = [t + _dot_f32(t, p) for t, p in zip(ts, ps)]
        n *= 2
        if n < CH:
            ps = [_dot_f32(p, p) for p in ps]
    return ts


def _chunk_local(qn, kn, v, b_row, g_row):
    ri, ci = _iota((CH, CH), 0), _iota((CH, CH), 1)
    incl, strict = ri >= ci, ri > ci
    gam_col = jnp.sum(jnp.where(incl, g_row, 0.0), axis=1, keepdims=True)
    gam_row = _col_to_row(gam_col)
    b_col = _row_to_col(b_row)
    dec = jnp.exp(jnp.where(incl, gam_col - gam_row, -jnp.inf))
    eg = jnp.exp(gam_col)
    gl = jnp.sum(g_row, axis=1, keepdims=True)
    ekd = jnp.exp(gl - gam_col)
    kb = kn * b_col
    a = jnp.where(strict, _dot_nt(kb, kn) * dec, 0.0)
    p = jnp.where(incl, _dot_nt(qn, kn) * dec, 0.0)
    return dict(dec=dec, eg=eg, ekd=ekd, kb=kb, vb=v * b_col, a=a, kbg=kb * eg, p=p, qd=qn * eg, kd=kn * ekd,
                b_col=b_col, incl=incl, strict=strict)


def _chunks_per_step(nc):
    return max(g for g in (1, 2, 3, 6) if nc % g == 0)


def _heads_per_step(hv):
    return min(hv, 8)


def _delta_local(act, gates, key_w, name):
    lp = act.shape[0]
    hk, nc = key_w // HD, lp // CH
    hv = 2 * hk
    g = _chunks_per_step(nc)
    tr = g * CH

    def body(q_ref, k_ref, v_ref, g_ref, u_ref, w_ref, qd_ref, kd_ref, p_ref, t_ref):
        items = []
        for j in range(g):
            rows = slice(j * CH, (j + 1) * CH)
            qn = _l2n(q_ref[rows, :])[0] * (HD ** -0.5)
            kn = _l2n(k_ref[rows, :])[0]
            for e in range(2):
                cols = slice(e * HD, (e + 1) * HD)
                r = _chunk_local(qn, kn, v_ref[rows, cols], g_ref[0, j, e:e + 1, :], g_ref[0, j, 2 + e:3 + e, :])
                qd_ref[rows, cols] = r["qd"].astype(BF16)
                kd_ref[rows, cols] = r["kd"].astype(BF16)
                p_ref[e, rows, :] = r["p"].astype(BF16)
                items.append((rows, cols, e, r["a"], r["vb"].astype(BF16), r["kbg"].astype(BF16)))
        ts = [t.astype(BF16) for t in _tri_inverse([it[3] for it in items])]
        us = [_dot(t, it[4]) for t, it in zip(ts, items)]
        ws = [_dot(t, it[5]) for t, it in zip(ts, items)]
        for (rows, cols, e, _, _, _), t, u, w in zip(items, ts, us, ws):
            u_ref[rows, cols] = u
            w_ref[rows, cols] = w.astype(BF16)
            t_ref[e, rows, :] = t

    wide = pl.BlockSpec((tr, 2 * HD), lambda h, c: (c, h))
    sq = pl.BlockSpec((2, tr, CH), lambda h, c: (h, c, 0))
    wshape = lambda dt: jax.ShapeDtypeStruct((lp, hv * HD), dt)
    sshape = jax.ShapeDtypeStruct((hv, lp, CH), BF16)
    return pl.pallas_call(
        body, name=name, grid=(hk, nc // g),
        in_specs=[pl.BlockSpec((tr, HD), lambda h, c: (c, h)),
                  pl.BlockSpec((tr, HD), lambda h, c: (c, hk + h)),
                  pl.BlockSpec((tr, 2 * HD), lambda h, c: (c, hk + h)),
                  pl.BlockSpec((1, g, 8, CH), lambda h, c: (h, c, 0, 0))],
        out_specs=(wide, wide, wide, wide, sq, sq),
        out_shape=(wshape(F32), wshape(BF16), wshape(BF16), wshape(BF16), sshape, sshape),
        compiler_params=_cp("parallel", "parallel"))(act, act, act, gates)


def _chunk_decay(g_ref, e):
    return jnp.exp(jnp.sum(g_ref[e // 2, 0, 2 + e % 2:3 + e % 2, :], axis=1, keepdims=True))


def _delta_scan(u, w, qd, kd, p, gates, name):
    lp, val = u.shape
    hv, nc = val // HD, lp // CH
    nh = _heads_per_step(hv)

    def body(u_ref, w_ref, qd_ref, kd_ref, p_ref, g_ref, o_ref, vn_ref, st_ref, s_scr):
        @pl.when(pl.program_id(1) == 0)
        def _():
            s_scr[...] = jnp.zeros_like(s_scr)

        heads = range(nh)
        col = lambda e: slice(e * HD, (e + 1) * HD)
        ss = [s_scr[e] for e in heads]
        sb = [s.astype(BF16) for s in ss]
        for e in heads:
            st_ref[0, e] = ss[e]
        ws = [_dot(w_ref[:, col(e)], sb[e]) for e in heads]
        qs = [_dot(qd_ref[:, col(e)], sb[e]) for e in heads]
        vns = [(u_ref[:, col(e)] - ws[e]).astype(BF16) for e in heads]
        pv = [_dot(p_ref[e], vns[e]) for e in heads]
        kv = [_dot_tn(kd_ref[:, col(e)], vns[e]) for e in heads]
        for e in heads:
            o_ref[:, col(e)] = qs[e] + pv[e]
            s_scr[e] = _chunk_decay(g_ref, e) * ss[e] + kv[e]
            vn_ref[:, col(e)] = vns[e]

    wide = pl.BlockSpec((CH, nh * HD), lambda h, c: (c, h))
    return pl.pallas_call(
        body, name=name, grid=(hv // nh, nc),
        in_specs=[wide, wide, wide, wide, pl.BlockSpec((nh, CH, CH), lambda h, c: (h, c, 0)),
                  pl.BlockSpec((nh // 2, 1, 8, CH), lambda h, c: (h, c, 0, 0))],
        out_specs=(wide, wide, pl.BlockSpec((1, nh, HD, HD), lambda h, c: (c, h, 0, 0))),
        out_shape=(jax.ShapeDtypeStruct((lp, val), F32), jax.ShapeDtypeStruct((lp, val), BF16),
                   jax.ShapeDtypeStruct((nc, hv, HD, HD), F32)),
        scratch_shapes=[pltpu.VMEM((nh, HD, HD), F32)],
        compiler_params=_cp("parallel", "arbitrary"))(u, w, qd, kd, p, gates)


def _delta_scan_bwd(do, w, qd, kd, p, vn, states, gates, name):
    lp, val = do.shape
    hv, nc = val // HD, lp // CH
    nh = _heads_per_step(hv)

    def body(do_ref, w_ref, qd_ref, kd_ref, p_ref, vn_ref, st_ref, g_ref,
             dvn_ref, dw_ref, dqd_ref, dkd_ref, dp_ref, sd_ref, ds_scr):
        @pl.when(pl.program_id(1) == 0)
        def _():
            ds_scr[...] = jnp.zeros_like(ds_scr)

        incl = _iota((CH, CH), 0) >= _iota((CH, CH), 1)
        heads = range(nh)
        col = lambda e: slice(e * HD, (e + 1) * HD)
        ss = [st_ref[0, e] for e in heads]
        dss = [ds_scr[e] for e in heads]
        sb = [s.astype(BF16) for s in ss]
        dsb = [d.astype(BF16) for d in dss]
        dos = [do_ref[:, col(e)].astype(BF16) for e in heads]
        egl = [_chunk_decay(g_ref, e) for e in heads]
        pdo = [_dot_tn(p_ref[e], dos[e]) for e in heads]
        kds = [_dot(kd_ref[:, col(e)], dsb[e]) for e in heads]
        qdo = [_dot_tn(qd_ref[:, col(e)], dos[e]) for e in heads]
        dqd = [_dot_nt(dos[e], sb[e]) for e in heads]
        dkd = [_dot_nt(vn_ref[:, col(e)], dsb[e]) for e in heads]
        dpp = [_dot_nt(dos[e], vn_ref[:, col(e)]) for e in heads]
        dvn = [(pdo[e] + kds[e]).astype(BF16) for e in heads]
        wdv = [_dot_tn(w_ref[:, col(e)], dvn[e]) for e in heads]
        dws = [_dot_nt(dvn[e], sb[e]) for e in heads]
        for e in heads:
            ds_scr[e] = qdo[e] + egl[e] * dss[e] - wdv[e]
            dvn_ref[:, col(e)] = dvn[e]
            dw_ref[:, col(e)] = (-dws[e]).astype(BF16)
            dqd_ref[:, col(e)] = dqd[e]
            dkd_ref[:, col(e)] = dkd[e]
            dp_ref[e] = jnp.where(incl, dpp[e], 0.0)
            sd_ref[0, 0, e:e + 1, :] = jnp.broadcast_to(egl[e] * jnp.sum(ss[e] * dss[e], keepdims=True), (1, HD))

    rev = lambda c: nc - 1 - c
    wide = pl.BlockSpec((CH, nh * HD), lambda h, c: (rev(c), h))
    sq = pl.BlockSpec((nh, CH, CH), lambda h, c: (h, rev(c), 0))
    wshape = lambda dt: jax.ShapeDtypeStruct((lp, val), dt)
    return pl.pallas_call(
        body, name=name, grid=(hv // nh, nc),
        in_specs=[wide, wide, wide, wide, sq, wide, pl.BlockSpec((1, nh, HD, HD), lambda h, c: (rev(c), h, 0, 0)),
                  pl.BlockSpec((nh // 2, 1, 8, CH), lambda h, c: (h, rev(c), 0, 0))],
        out_specs=(wide, wide, wide, wide, sq, pl.BlockSpec((1, 1, nh, HD), lambda h, c: (h, rev(c), 0, 0))),
        out_shape=(wshape(BF16), wshape(BF16), wshape(F32), wshape(F32), jax.ShapeDtypeStruct((hv, lp, CH), F32),
                   jax.ShapeDtypeStruct((hv // nh, nc, nh, HD), F32)),
        scratch_shapes=[pltpu.VMEM((nh, HD, HD), F32)],
        compiler_params=_cp("parallel", "arbitrary"))(do, w, qd, kd, p, vn, states, gates)


def _delta_local_bwd(act, gates, t, dvn, dw, dqd, dkd, dp, key_w, name):
    lp = act.shape[0]
    hk, nc = key_w // HD, lp // CH
    hv = 2 * hk
    g = _chunks_per_step(nc)
    tr = g * CH
    scale = HD ** -0.5

    def body(q_ref, k_ref, v_ref, g_ref, t_ref, dvn_ref, dw_ref, dqd_ref, dkd_ref, dp_ref, dq_ref, dk_ref, dv_ref, dg_ref):
        ri, ci = _iota((CH, CH), 0), _iota((CH, CH), 1)
        inner = lambda x, z: jnp.sum(x * z, axis=1, keepdims=True)
        norms, items = [], []
        for j in range(g):
            rows = slice(j * CH, (j + 1) * CH)
            qh, qr = _l2n(q_ref[rows, :])
            kn, kr = _l2n(k_ref[rows, :])
            qn = qh * scale
            norms.append((rows, qh, qr, kn, kr, qn))
            dg_ref[0, j, 4:8, :] = jnp.zeros((4, CH), F32)
            for e in range(2):
                cols = slice(e * HD, (e + 1) * HD)
                v = v_ref[rows, cols]
                r = _chunk_local(qn, kn, v, g_ref[0, j, e:e + 1, :], g_ref[0, j, 2 + e:3 + e, :])
                items.append(dict(r, j=j, e=e, rows=rows, cols=cols, v=v, kn=kn, qn=qn, t=t_ref[e, rows, :],
                                  dvn=dvn_ref[rows, cols], dw=dw_ref[rows, cols]))
        for it in items:
            it["dt"] = _dot_nt(it["dvn"], it["vb"]) + _dot_nt(it["dw"], it["kbg"])
            it["dvb"] = _dot_tn(it["t"], it["dvn"])
            it["dkbg"] = _dot_tn(it["t"], it["dw"])
        for it in items:
            it["x"] = _dot_tn(it["t"], it["dt"])
        for it in items:
            it["da"] = -jnp.where(it["strict"], _dot_nt(it["x"], it["t"]), 0.0)
        for it in items:
            dp = dp_ref[it["e"], it["rows"], :]
            it["gmat"] = it["da"] * it["a"] + dp * it["p"]
            mm, nn = (it["da"] * it["dec"]).astype(BF16), (dp * it["dec"]).astype(BF16)
            it["dkb"] = _dot(mm, it["kn"]) + it["dkbg"] * it["eg"]
            it["dkn"] = _dot_tn(mm, it["kb"]) + _dot_tn(nn, it["qn"])
            it["dqn"] = _dot(nn, it["kn"])
        for it in items:
            j, e, rows, cols = it["j"], it["e"], it["rows"], it["cols"]
            dqd, dkd, gmat, dkb = dqd_ref[rows, cols], dkd_ref[rows, cols], it["gmat"], it["dkb"]
            it["dkn"] = it["dkn"] + dkd * it["ekd"] + it["b_col"] * dkb
            it["dqn"] = it["dqn"] + dqd * it["eg"]
            dkd_kd = inner(dkd, it["kd"])
            dgam = (jnp.sum(gmat, axis=1, keepdims=True) - _row_to_col(jnp.sum(gmat, axis=0, keepdims=True))
                    + inner(dqd, it["qd"]) + inner(it["dkbg"], it["kbg"]) - dkd_kd)
            dgl = jnp.max(g_ref[0, j, 4 + e:5 + e, :], axis=1, keepdims=True) + jnp.sum(dkd_kd, keepdims=True)
            dgam = dgam + jnp.where(_iota((CH, 1), 0) == CH - 1, dgl, 0.0)
            dg_ref[0, j, 2 + e:3 + e, :] = jnp.sum(jnp.where(ri >= ci, dgam, 0.0), axis=0, keepdims=True)
            dg_ref[0, j, e:e + 1, :] = _col_to_row(inner(dkb, it["kn"]) + inner(it["dvb"], it["v"]))
            dv_ref[rows, cols] = it["b_col"] * it["dvb"]
        for j, (rows, qh, qr, kn, kr, _) in enumerate(norms):
            dqh = (items[2 * j]["dqn"] + items[2 * j + 1]["dqn"]) * scale
            dkn = items[2 * j]["dkn"] + items[2 * j + 1]["dkn"]
            dq_ref[rows, :] = qr * (dqh - qh * jnp.sum(dqh * qh, axis=1, keepdims=True))
            dk_ref[rows, :] = kr * (dkn - kn * jnp.sum(dkn * kn, axis=1, keepdims=True))

    narrow = pl.BlockSpec((tr, HD), lambda h, c: (c, h))
    wide = pl.BlockSpec((tr, 2 * HD), lambda h, c: (c, h))
    sq = pl.BlockSpec((2, tr, CH), lambda h, c: (h, c, 0))
    gate = pl.BlockSpec((1, g, 8, CH), lambda h, c: (h, c, 0, 0))
    return pl.pallas_call(
        body, name=name, grid=(hk, nc // g),
        in_specs=[narrow, pl.BlockSpec((tr, HD), lambda h, c: (c, hk + h)),
                  pl.BlockSpec((tr, 2 * HD), lambda h, c: (c, hk + h)), gate, sq, wide, wide, wide, wide, sq],
        out_specs=(narrow, narrow, wide, gate),
        out_shape=(jax.ShapeDtypeStruct((lp, key_w), F32), jax.ShapeDtypeStruct((lp, key_w), F32),
                   jax.ShapeDtypeStruct((lp, hv * HD), F32), jax.ShapeDtypeStruct((hk, nc, 8, CH), F32)),
        compiler_params=_cp("parallel", "parallel"))(act, act, act, gates, t, dvn, dw, dqd, dkd, dp)


def _outnorm_fwd(o, proj, z_col0, w, name):
    lp, val = o.shape
    tm, zb = _tile(lp, 1056), z_col0 // HD

    def body(o_ref, z_ref, w_ref, y_ref):
        xf = o_ref[...]
        r = lax.rsqrt(jnp.mean(xf * xf, axis=-1, keepdims=True) + EPS)
        y_ref[...] = (xf * r * w_ref[...] * _silu(z_ref[...])).astype(BF16)

    return pl.pallas_call(
        body, name=name, grid=(lp // tm, val // HD),
        in_specs=[pl.BlockSpec((tm, HD), lambda i, h: (i, h)), pl.BlockSpec((tm, HD), lambda i, h: (i, zb + h)),
                  pl.BlockSpec((1, HD), lambda i, h: (0, 0))],
        out_specs=pl.BlockSpec((tm, HD), lambda i, h: (i, h)), out_shape=jax.ShapeDtypeStruct((lp, val), BF16),
        compiler_params=_cp("parallel", "parallel"))(o, proj, w)


def _outnorm_bwd(o, proj, z_col0, w, dy, name):
    lp, val = o.shape
    tm, zb = _tile(lp, 1056), z_col0 // HD

    def body(o_ref, z_ref, w_ref, dy_ref, do_ref, dz_ref, dw_ref):
        xf, z, d = o_ref[...], z_ref[...], dy_ref[...]
        r = lax.rsqrt(jnp.mean(xf * xf, axis=-1, keepdims=True) + EPS)
        xhat = xf * r
        dn = d * _silu(z)
        dz_ref[...] = (d * xhat * w_ref[...] * _dsilu(z)).astype(BF16)
        dxhat = dn * w_ref[...]
        do_ref[...] = r * (dxhat - xhat * jnp.mean(dxhat * xhat, axis=-1, keepdims=True))

        @pl.when((pl.program_id(0) == 0) & (pl.program_id(1) == 0))
        def _():
            dw_ref[...] = jnp.zeros_like(dw_ref)

        dw_ref[...] += jnp.sum(dn * xhat, axis=0, keepdims=True)

    blk = pl.BlockSpec((tm, HD), lambda i, h: (i, h))
    vec = pl.BlockSpec((1, HD), lambda i, h: (0, 0))
    return pl.pallas_call(
        body, name=name, grid=(lp // tm, val // HD),
        in_specs=[blk, pl.BlockSpec((tm, HD), lambda i, h: (i, zb + h)), vec, blk], out_specs=(blk, blk, vec),
        out_shape=(jax.ShapeDtypeStruct((lp, val), F32), jax.ShapeDtypeStruct((lp, val), BF16),
                   jax.ShapeDtypeStruct((1, HD), F32)),
        compiler_params=_cp("arbitrary", "arbitrary"))(o, proj, w, dy)


def _qknorm_fwd(proj, qw, kw, width, name):
    lp = proj.shape[0]
    tm, nh = _tile(lp, 1056), width // HD

    def body(q_ref, k_ref, v_ref, qw_ref, kw_ref, qo_ref, ko_ref, vo_ref):
        for x_ref, w_ref, o_ref in ((q_ref, qw_ref, qo_ref), (k_ref, kw_ref, ko_ref)):
            xf = x_ref[...]
            r = lax.rsqrt(jnp.mean(xf * xf, axis=-1, keepdims=True) + EPS)
            o_ref[...] = (xf * r * w_ref[...]).astype(BF16)
        vo_ref[...] = v_ref[...].astype(BF16)

    blk = lambda off: pl.BlockSpec((tm, HD), lambda i, h: (i, off + h))
    vec = pl.BlockSpec((1, HD), lambda i, h: (0, 0))
    shp = jax.ShapeDtypeStruct((lp, width), BF16)
    return pl.pallas_call(
        body, name=name, grid=(lp // tm, nh), in_specs=[blk(0), blk(nh), blk(2 * nh), vec, vec],
        out_specs=(blk(0), blk(0), blk(0)), out_shape=(shp, shp, shp),
        compiler_params=_cp("parallel", "parallel"))(proj, proj, proj, qw, kw)


def _qknorm_bwd(proj, qw, kw, dqn, dkn, width, name):
    lp = proj.shape[0]
    tm, nh = _tile(lp, 1056), width // HD

    def body(q_ref, k_ref, qw_ref, kw_ref, dqn_ref, dkn_ref, dq_ref, dk_ref, dqw_ref, dkw_ref):
        first = (pl.program_id(0) == 0) & (pl.program_id(1) == 0)
        for x_ref, w_ref, dy_ref, dx_ref, dw_ref in ((q_ref, qw_ref, dqn_ref, dq_ref, dqw_ref),
                                                       (k_ref, kw_ref, dkn_ref, dk_ref, dkw_ref)):
            xf, dy = x_ref[...], dy_ref[...]
            r = lax.rsqrt(jnp.mean(xf * xf, axis=-1, keepdims=True) + EPS)
            xhat = xf * r
            dxhat = dy * w_ref[...]
            dx_ref[...] = (r * (dxhat - xhat * jnp.mean(dxhat * xhat, axis=-1, keepdims=True))).astype(BF16)

            @pl.when(first)
            def _():
                dw_ref[...] = jnp.zeros_like(dw_ref)

            dw_ref[...] += jnp.sum(dy * xhat, axis=0, keepdims=True)

    blk = lambda off: pl.BlockSpec((tm, HD), lambda i, h: (i, off + h))
    vec = pl.BlockSpec((1, HD), lambda i, h: (0, 0))
    shp = jax.ShapeDtypeStruct((lp, width), BF16)
    vshp = jax.ShapeDtypeStruct((1, HD), F32)
    return pl.pallas_call(
        body, name=name, grid=(lp // tm, nh), in_specs=[blk(0), blk(nh), vec, vec, blk(0), blk(0)],
        out_specs=(blk(0), blk(0), vec, vec), out_shape=(shp, shp, vshp, vshp),
        compiler_params=_cp("arbitrary", "arbitrary"))(proj, proj, qw, kw, dqn, dkn)


def _sb_tq(lp):
    return 3 * QB if lp % (3 * QB) == 0 else QB


def _sb_scores(qk, t_idx, kb):
    z = qk * (HD ** -0.5)
    s_idx = kb * QB + _iota((1, QB), 1)
    valid = (s_idx < t_idx) & (s_idx >= INERT)
    sp = jnp.log(1.0 + jnp.exp(-jnp.abs(z)))
    lsz = jnp.minimum(z, 0.0) - sp
    lk = jnp.where(valid, -jnp.maximum(z, 0.0) - sp, 0.0)
    return valid, lsz, lk


def _sb_fwd(qn, kn, vv, proj, gate_col0, name):
    lp, width = qn.shape
    tq = _sb_tq(lp)
    nh, nq, gb, nsub = width // HD, lp // tq, gate_col0 // HD, tq // QB

    def body(q_ref, k_ref, v_ref, g_ref, o_ref, og_ref, tot_ref):
        qb = pl.program_id(1)
        q = q_ref[...]
        t_idx = qb * tq + _iota((tq, 1), 0)
        upper = (_iota((QB, QB), 0) > _iota((QB, QB), 1)).astype(BF16)

        def step(i, carry):
            run, acc = carry
            kbs = [(qb - i) * nsub + sub for sub in reversed(range(nsub))]
            rows = [pl.ds(pl.multiple_of(kb * QB, QB), QB) for kb in kbs]
            qks = [_dot_nt(q, k_ref[r, :]) for r in rows]
            scores = [_sb_scores(qk, t_idx, kb) for qk, kb in zip(qks, kbs)]
            sums = [_dot_split(lk, upper) for _, _, lk in scores]
            probs = []
            for (valid, lsz, lk), part in zip(scores, sums):
                probs.append(jnp.where(valid, jnp.exp(lsz + part + run), 0.0).astype(BF16))
                run = run + jnp.sum(lk, axis=1, keepdims=True)
            for a, r in zip(probs, rows):
                acc = acc + _dot(a, v_ref[r, :])
            return run, acc

        run, acc = lax.fori_loop(0, qb + 1, step, (jnp.zeros((tq, 1), F32), jnp.zeros((tq, HD), F32)))
        o_ref[...] = acc
        og_ref[...] = (acc * _silu(g_ref[...])).astype(BF16)
        tot_ref[0, 0] = _col_to_row(run)

    full = pl.BlockSpec((lp, HD), lambda h, i: (0, h))
    blk = pl.BlockSpec((tq, HD), lambda h, i: (i, h))
    return pl.pallas_call(
        body, name=name, grid=(nh, nq),
        in_specs=[blk, full, full, pl.BlockSpec((tq, HD), lambda h, i: (i, gb + h))],
        out_specs=(blk, blk, pl.BlockSpec((1, 1, 1, tq), lambda h, i: (h, i, 0, 0))),
        out_shape=(jax.ShapeDtypeStruct((lp, width), F32), jax.ShapeDtypeStruct((lp, width), BF16),
                   jax.ShapeDtypeStruct((nh, nq, 1, tq), F32)),
        compiler_params=_cp("parallel", "parallel"))(qn, kn, vv, proj)


def _sb_bwd(qn, kn, vv, proj, gate_col0, att, tot, dog, name):
    lp, width = qn.shape
    tq = _sb_tq(lp)
    nh, nq, gb, nsub = width // HD, lp // tq, gate_col0 // HD, tq // QB
    scale = HD ** -0.5

    def body(q_ref, k_ref, v_ref, g_ref, att_ref, tot_ref, dog_ref, dq_ref, dk_ref, dv_ref, dg_ref, dk_acc, dv_acc):
        qb = pl.program_id(1)

        @pl.when(qb == 0)
        def _():
            dk_acc[...] = jnp.zeros_like(dk_acc)
            dv_acc[...] = jnp.zeros_like(dv_acc)

        q, gate, dg_out = q_ref[...], g_ref[...], dog_ref[...]
        d_o = (dg_out * _silu(gate)).astype(BF16)
        dg_ref[...] = (dg_out * att_ref[...] * _dsilu(gate)).astype(BF16)
        total = _row_to_col(tot_ref[0, 0])
        t_idx = qb * tq + _iota((tq, 1), 0)
        ri, ci = _iota((QB, QB), 0), _iota((QB, QB), 1)
        lower_incl = (ri <= ci).astype(BF16)
        lower_excl = (ri < ci).astype(BF16)

        def step(kg, carry):
            run, erun, dq = carry
            kbs = [kg * nsub + sub for sub in range(nsub)]
            rows = [pl.ds(pl.multiple_of(kb * QB, QB), QB) for kb in kbs]
            qks = [_dot_nt(q, k_ref[r, :]) for r in rows]
            dprobs = [_dot_nt(d_o, v_ref[r, :]) for r in rows]
            scores = [_sb_scores(qk, t_idx, kb) for qk, kb in zip(qks, kbs)]
            sums = [_dot_split(lk, lower_incl) for _, _, lk in scores]
            probs, es = [], []
            for (valid, lsz, lk), part, dprob in zip(scores, sums, dprobs):
                a = jnp.where(valid, jnp.exp(lsz + (total - run - part)), 0.0)
                probs.append(a.astype(BF16))
                es.append(a * dprob)
                run = run + jnp.sum(lk, axis=1, keepdims=True)
            esums = [_dot_split(e, lower_excl) for e in es]
            for a, r in zip(probs, rows):
                dv_acc[r, :] += _dot_tn(a, d_o)
            dzs = []
            for (valid, lsz, _), e, part in zip(scores, es, esums):
                sig = jnp.exp(lsz)
                dzs.append((jnp.where(valid, e * (1.0 - sig) - sig * (erun + part), 0.0) * scale).astype(BF16))
                erun = erun + jnp.sum(e, axis=1, keepdims=True)
            for dz, r in zip(dzs, rows):
                dk_acc[r, :] += _dot_tn(dz, q)
                dq = dq + _dot(dz, k_ref[r, :])
            return run, erun, dq

        zero = jnp.zeros((tq, 1), F32)
        _, _, dq = lax.fori_loop(0, qb + 1, step, (zero, zero, jnp.zeros((tq, HD), F32)))
        dq_ref[...] = dq

        @pl.when(qb == nq - 1)
        def _():
            dk_ref[...] = dk_acc[...]
            dv_ref[...] = dv_acc[...].astype(BF16)

    full = pl.BlockSpec((lp, HD), lambda h, i: (0, h))
    blk = pl.BlockSpec((tq, HD), lambda h, i: (i, h))
    return pl.pallas_call(
        body, name=name, grid=(nh, nq),
        in_specs=[blk, full, full, pl.BlockSpec((tq, HD), lambda h, i: (i, gb + h)), blk,
                  pl.BlockSpec((1, 1, 1, tq), lambda h, i: (h, i, 0, 0)), blk],
        out_specs=(blk, full, full, blk),
        out_shape=(jax.ShapeDtypeStruct((lp, width), F32), jax.ShapeDtypeStruct((lp, width), F32),
                   jax.ShapeDtypeStruct((lp, width), BF16), jax.ShapeDtypeStruct((lp, width), BF16)),
        scratch_shapes=[pltpu.VMEM((lp, HD), F32), pltpu.VMEM((lp, HD), F32)],
        compiler_params=_cp("parallel", "arbitrary"))(qn, kn, vv, proj, att, tot, dog)


def _loss_head(h, target, name):
    lp, d = h.shape
    tm = _tile(PAD, 128)
    nt = lp // tm
    npad = PAD // tm

    def body(h_ref, t_ref, l_ref, dh_ref, dhb_ref):
        i = pl.program_id(0)
        err = jnp.where(i >= npad, h_ref[...] - t_ref[...], 0.0)
        dh = err * (1.0 / d)
        dh_ref[...] = dh
        dhb_ref[...] = dh.astype(BF16)

        @pl.when(i == 0)
        def _():
            l_ref[...] = jnp.zeros_like(l_ref)

        l_ref[...] += (0.5 / d) * jnp.sum(err * err, keepdims=True)

    row = pl.BlockSpec((tm, d), lambda i: (i, 0))
    return pl.pallas_call(
        body, name=name, grid=(nt,),
        in_specs=[row, pl.BlockSpec((tm, d), lambda i: (jnp.maximum(i - npad, 0), 0))],
        out_specs=(pl.BlockSpec((1, 1), lambda i: (0, 0)), row, row),
        out_shape=(jax.ShapeDtypeStruct((1, 1), F32), jax.ShapeDtypeStruct((lp, d), F32),
                   jax.ShapeDtypeStruct((lp, d), BF16)),
        compiler_params=_cp("arbitrary"))(h, target)


def _adamw(parts, w, m, v, name, mine=None):
    r, c = w.shape
    tr = _tile(r, max(8, (1 << 18) // c // 8 * 8))
    nparts = parts.shape[0]

    def body(*refs):
        p_ref, w_ref, m_ref, v_ref = refs[:4]
        g_ref, d_ref, mo_ref, vo_ref = refs[-4:]
        g = p_ref[0].astype(F32)
        if mine is not None:
            g = refs[4][...].astype(F32) + g
        for k in range(1, nparts):
            g = g + p_ref[k].astype(F32)
        mn = ADAM_B1 * m_ref[...] + (1.0 - ADAM_B1) * g
        vn = ADAM_B2 * v_ref[...] + (1.0 - ADAM_B2) * jnp.square(g)
        m_hat = mn / (1.0 - ADAM_B1 ** ADAM_STEP)
        v_hat = vn / (1.0 - ADAM_B2 ** ADAM_STEP)
        g_ref[...] = g
        d_ref[...] = -ADAM_LR * (m_hat / (jnp.sqrt(v_hat) + ADAM_EPS) + ADAM_WD * w_ref[...])
        mo_ref[...] = mn
        vo_ref[...] = vn

    blk = pl.BlockSpec((tr, c), lambda i: (i, 0))
    shp = jax.ShapeDtypeStruct((r, c), F32)
    extra = [] if mine is None else [mine]
    return pl.pallas_call(
        body, name=name, grid=(r // tr,),
        in_specs=[pl.BlockSpec((nparts, tr, c), lambda i: (0, i, 0)), blk, blk, blk] + [blk] * len(extra),
        out_specs=(blk, blk, blk, blk), out_shape=(shp, shp, shp, shp), compiler_params=_cp("parallel"))(parts, w, m, v, *extra)


def _unshard_cols(g):
    return jnp.transpose(g, (1, 0, 2)).reshape(g.shape[1], NDEV * g.shape[2])


def _shard_cols(a):
    r, c = a.shape
    return jnp.transpose(a.reshape(r, NDEV, c // NDEV), (1, 0, 2))


def kernel(x, meta_tokens, dn_norm_w, dn_w_in, dn_conv_w, dn_a_log, dn_dt_bias, dn_out_norm_w, dn_w_out, sb_norm_w, sb_w_in, sb_q_norm_w, sb_k_norm_w, sb_w_out, loss_target, m_meta_tokens, m_dn_norm_w, m_dn_w_in, m_dn_conv_w, m_dn_a_log, m_dn_dt_bias, m_dn_out_norm_w, m_dn_w_out, m_sb_norm_w, m_sb_w_in, m_sb_q_norm_w, m_sb_k_norm_w, m_sb_w_out, v_meta_tokens, v_dn_norm_w, v_dn_w_in, v_dn_conv_w, v_dn_a_log, v_dn_dt_bias, v_dn_out_norm_w, v_dn_w_out, v_sb_norm_w, v_sb_w_in, v_sb_q_norm_w, v_sb_k_norm_w, v_sb_w_out):
    seq, d = x.shape[1], x.shape[2]
    lp = PAD + seq
    key_w = d
    val_w = 2 * d
    hv = val_w // HD
    conv_w_cols = 2 * key_w + val_w
    main_w = conv_w_cols + val_w
    sb_w = d
    nc = lp // CH
    hk = key_w // HD

    (g_meta, g_sbn, g_conv) = _exchange([meta_tokens, sb_norm_w, dn_conv_w[0]], True, "gather_vectors")
    st_a, tok_a = _exchange_start([dn_w_in[0].astype(BF16)], True, "gather_w_in0_start", after=g_meta)
    st_b, tok_b = _exchange_start([dn_w_out[0].astype(BF16), sb_w_in[0].astype(BF16), sb_w_out[0].astype(BF16)],
                                  True, "gather_w_rest_start", after=tok_a)
    meta = _unshard_cols(g_meta) + tok_b[:1, :1]
    sbn_w = _unshard_cols(g_sbn)
    conv_w = _unshard_cols(g_conv)
    h0 = jnp.concatenate([jnp.zeros((INERT, d), F32), meta, x[0]], axis=0)
    hn0 = _rms_fwd(h0, dn_norm_w, "dn_norm")
    lands_a, srcs_a = _exchange_wait(st_a, hn0, "gather_w_in0_wait")
    w_in0 = _unshard_cols(_with_own(lands_a[0], srcs_a[0]))
    w_in0_main, w_in0_gate = w_in0[:, :main_w], w_in0[:, main_w:]

    proj0 = _matmul(hn0, w_in0_main, mode="nn", out_dtype=F32, tm=1056, tn=512, tk=4096, name="dn_in_proj")
    gl0 = _matmul(hn0, w_in0_gate, mode="nn", out_dtype=F32, tm=1056, tn=512, tk=4096, name="dn_gate_proj")
    act0 = _conv_fwd(proj0, conv_w, conv_w_cols, "dn_conv")
    a_log2 = jnp.concatenate([jnp.zeros_like(dn_a_log), dn_a_log], axis=1)
    dt_bias2 = jnp.concatenate([jnp.zeros_like(dn_dt_bias), dn_dt_bias], axis=1)
    bg = _gates_fwd(gl0, a_log2, dt_bias2, "dn_gates")
    pack = lambda t: jnp.transpose(t.reshape(nc, CH, hk, 2), (2, 0, 3, 1))
    gates = jnp.concatenate([pack(bg[:, :hv]), pack(bg[:, hv:]), jnp.zeros((hk, nc, 4, CH), F32)], axis=2)
    u0, w0, qd0, kd0, p0, t0 = _delta_local(act0, gates, key_w, "dn_delta_local")
    o0, vn0, states = _delta_scan(u0, w0, qd0, kd0, p0, gates, "dn_delta_scan")
    o0g = _outnorm_fwd(o0, proj0, conv_w_cols, dn_out_norm_w, "dn_out_norm")
    lands_b, srcs_b = _exchange_wait(st_b, o0g, "gather_w_rest_wait")
    g_out0, g_in1, g_out1 = [_with_own(l, s) for l, s in zip(lands_b, srcs_b)]
    w_out0 = g_out0.reshape(val_w, d)
    w_in1 = _unshard_cols(g_in1)
    w_out1 = g_out1.reshape(sb_w, d)
    h1 = _matmul(o0g, w_out0, mode="nn", out_dtype=F32, tm=1056, tn=512, tk=4096, name="dn_out_proj", add=h0)
    hn1 = _rms_fwd(h1, sbn_w, "sb_norm")
    proj1 = _matmul(hn1, w_in1, mode="nn", out_dtype=F32, tm=1056, tn=512, tk=4096, name="sb_in_proj")
    qn1, kn1, vv1 = _qknorm_fwd(proj1, sb_q_norm_w, sb_k_norm_w, sb_w, "sb_qk_norm")
    att1, o1g, tot1 = _sb_fwd(qn1, kn1, vv1, proj1, 3 * sb_w, "sb_attn")
    h2 = _matmul(o1g, w_out1, mode="nn", out_dtype=F32, tm=1056, tn=512, tk=4096, name="sb_out_proj", add=h1)
    loss_part, dh2, dh2b = _loss_head(h2, loss_target[0], "loss_head")
    loss = lax.psum(loss_part[0, 0], ("x", "y", "c"))

    p_out1 = _matmul(o1g, dh2b, mode="tn", out_dtype=BF16, tm=1024, tn=512, tk=lp, name="sb_out_wgrad")
    do1g = _matmul(dh2b, w_out1, mode="nt", out_dtype=F32, tm=1056, tn=512, tk=4096, name="sb_out_dgrad")
    dqn1, dkn1, dv1, dgate1 = _sb_bwd(qn1, kn1, vv1, proj1, 3 * sb_w, att1, tot1, do1g, "sb_attn_bwd")
    dq1, dk1, d_qw, d_kw = _qknorm_bwd(proj1, sb_q_norm_w, sb_k_norm_w, dqn1, dkn1, sb_w, "sb_qk_norm_bwd")
    dproj1 = jnp.concatenate([dq1, dk1, dv1, dgate1], axis=1)
    p_in1 = _matmul(hn1, dproj1, mode="tn", out_dtype=BF16, tm=1024, tn=512, tk=lp, name="sb_in_wgrad")
    st_s1, tok_s1 = _exchange_start([p_out1.reshape(NDEV, sb_w // NDEV, d), _shard_cols(p_in1)], False, "scatter_sb_start")
    dhn1 = _matmul(dproj1, w_in1, mode="nt", out_dtype=F32, tm=1056, tn=512, tk=4096, name="sb_in_dgrad", after=tok_s1)
    dh1, dh1b, d_sbn = _rms_bwd(h1, sbn_w, dhn1, dh2, "sb_norm_bwd")

    p_out0 = _matmul(o0g, dh1b, mode="tn", out_dtype=BF16, tm=1024, tn=512, tk=lp, name="dn_out_wgrad")
    st_s2, tok_s2 = _exchange_start([p_out0.reshape(NDEV, val_w // NDEV, d)], False, "scatter_dn_out_start")
    do0g = _matmul(dh1b, w_out0, mode="nt", out_dtype=F32, tm=1056, tn=512, tk=4096, name="dn_out_dgrad", after=tok_s2)
    do0, dz0, d_onw = _outnorm_bwd(o0, proj0, conv_w_cols, dn_out_norm_w, do0g, "dn_out_norm_bwd")
    dvn0, dw0, dqd0, dkd0, dp0, sd0 = _delta_scan_bwd(do0, w0, qd0, kd0, p0, vn0, states, gates, "dn_delta_scan_bwd")
    sd_rows = jnp.transpose(jnp.transpose(sd0[..., 0], (0, 2, 1)).reshape(hk, 2, nc), (0, 2, 1))
    gates_b = jnp.concatenate([gates[:, :, :4], jnp.broadcast_to(sd_rows[..., None], (hk, nc, 2, CH)),
                               jnp.zeros((hk, nc, 2, CH), F32)], axis=2)
    dq_act, dk_act, dv_act, dgates = _delta_local_bwd(act0, gates_b, t0, dvn0, dw0, dqd0, dkd0, dp0, key_w,
                                                      "dn_delta_local_bwd")
    unpack = lambda t: jnp.transpose(t, (1, 3, 0, 2)).reshape(lp, hv)
    dbg = jnp.concatenate([unpack(dgates[:, :, 0:2]), unpack(dgates[:, :, 2:4])], axis=1)
    dgl0, d_alog2, d_dtb2 = _gates_bwd(gl0, a_log2, dt_bias2, dbg, "dn_gates_bwd")
    d_alog, d_dtb = d_alog2[:, hv:], d_dtb2[:, hv:]
    dxq, dcw_q = _conv_bwd(proj0, 0, conv_w[:, :key_w], dq_act, "dn_conv_bwd_q")
    dxk, dcw_k = _conv_bwd(proj0, key_w, conv_w[:, key_w:2 * key_w], dk_act, "dn_conv_bwd_k")
    dxv, dcw_v = _conv_bwd(proj0, 2 * key_w, conv_w[:, 2 * key_w:], dv_act, "dn_conv_bwd_v")
    dproj0 = jnp.concatenate([dxq, dxk, dxv, dz0], axis=1)
    p_in0_main = _matmul(hn0, dproj0, mode="tn", out_dtype=BF16, tm=1024, tn=512, tk=lp, name="dn_in_wgrad")
    p_in0_gate = _matmul(hn0, dgl0, mode="tn", out_dtype=BF16, tm=1024, tn=512, tk=lp, name="dn_gate_wgrad")
    p_in0 = _shard_cols(jnp.concatenate([p_in0_main, p_in0_gate], axis=1))
    st_s3, tok_s3 = _exchange_start([p_in0], False, "scatter_dn_in_start")
    dhn0 = _matmul(dgl0, w_in0_gate, mode="nt", out_dtype=F32, tm=1056, tn=512, tk=4096, name="dn_gate_dgrad", after=tok_s3)
    dhn0 = _matmul(dproj0, w_in0_main, mode="nt", out_dtype=F32, tm=1056, tn=512, tk=4096, name="dn_in_dgrad", add=dhn0)
    dh0, _, d_dnn = _rms_bwd(h0, dn_norm_w, dhn0, dh1, "dn_norm_bwd")
    grad_x = dh0[PAD:][None]

    p_conv = _shard_cols(jnp.concatenate([dcw_q, dcw_k, dcw_v], axis=1))
    (r_meta, r_sbn, r_conv) = _exchange([_shard_cols(dh0[INERT:PAD]), _shard_cols(d_sbn), p_conv], False, "scatter_vector_grads")
    small = jnp.concatenate([d_dnn, d_alog, d_dtb, d_onw, d_qw, d_kw], axis=1)
    (r_small,) = _exchange([small], True, "gather_replicated_grads")
    outs = {}
    outs["meta_tokens"] = _adamw(r_meta, meta_tokens, m_meta_tokens, v_meta_tokens, "adamw_meta")
    outs["dn_conv_w"] = _adamw(r_conv, dn_conv_w[0], m_dn_conv_w[0], v_dn_conv_w[0], "adamw_dn_conv")
    outs["sb_norm_w"] = _adamw(r_sbn, sb_norm_w, m_sb_norm_w, v_sb_norm_w, "adamw_sb_norm")
    me = _own_index()
    own = lambda src: lax.dynamic_index_in_dim(src, me, 0, keepdims=False)
    (r_out1, r_in1), (s_out1, s_in1) = _exchange_wait(st_s1, r_small, "scatter_sb_wait")
    outs["sb_w_in"] = _adamw(r_in1, sb_w_in[0], m_sb_w_in[0], v_sb_w_in[0], "adamw_sb_w_in", mine=own(s_in1))
    outs["sb_w_out"] = _adamw(r_out1, sb_w_out[0], m_sb_w_out[0], v_sb_w_out[0], "adamw_sb_w_out", mine=own(s_out1))
    (r_out0,), (s_out0,) = _exchange_wait(st_s2, outs["sb_w_in"][1], "scatter_dn_out_wait")
    outs["dn_w_out"] = _adamw(r_out0, dn_w_out[0], m_dn_w_out[0], v_dn_w_out[0], "adamw_dn_w_out", mine=own(s_out0))
    cat = lambda *a: jnp.concatenate(a, axis=1)
    rep = _adamw(r_small, cat(dn_norm_w, dn_a_log, dn_dt_bias, dn_out_norm_w, sb_q_norm_w, sb_k_norm_w),
                 cat(m_dn_norm_w, m_dn_a_log, m_dn_dt_bias, m_dn_out_norm_w, m_sb_q_norm_w, m_sb_k_norm_w),
                 cat(v_dn_norm_w, v_dn_a_log, v_dn_dt_bias, v_dn_out_norm_w, v_sb_q_norm_w, v_sb_k_norm_w),
                 "adamw_replicated")
    off = 0
    for nm, wd in (("dn_norm_w", d), ("dn_a_log", hv), ("dn_dt_bias", hv), ("dn_out_norm_w", HD),
                   ("sb_q_norm_w", HD), ("sb_k_norm_w", HD)):
        outs[nm] = tuple(t[:, off:off + wd] for t in rep)
        off += wd
    behind = jnp.broadcast_to(outs["dn_w_out"][1][0, 0] + rep[1][0, 0] + outs["sb_w_out"][1][0, 0] + outs["meta_tokens"][1][0, 0]
                              + outs["dn_conv_w"][1][0, 0] + outs["sb_norm_w"][1][0, 0], (8, 128))
    (r_in0,), (s_in0,) = _exchange_wait(st_s3, behind, "scatter_dn_in_wait")
    outs["dn_w_in"] = _adamw(r_in0, dn_w_in[0], m_dn_w_in[0], v_dn_w_in[0], "adamw_dn_w_in", mine=own(s_in0))
    lead =("dn_w_in", "dn_conv_w", "dn_w_out", "sb_w_in", "sb_w_out")
    order = ("meta_tokens", "dn_norm_w", "dn_w_in", "dn_conv_w", "dn_a_log", "dn_dt_bias", "dn_out_norm_w", "dn_w_out",
             "sb_norm_w", "sb_w_in", "sb_q_norm_w", "sb_k_norm_w", "sb_w_out")
    fix = lambda nm, t: t[None] if nm in lead else t
    result = [loss, grad_x]
    for kind in range(4):
        result += [fix(nm, outs[nm][kind]) for nm in order]
    return tuple(result)
```

```python
import functools

import jax
import jax.numpy as jnp
from jax import lax
from jax.experimental import pallas as pl
from jax.experimental.pallas import tpu as pltpu

F32 = jnp.float32
BF16 = jnp.bfloat16
HD = 128
CH = 64
QB = 128
N_META = 16
PAD = 128
INERT = PAD - N_META
NDEV = 8
CONV_K = 4
EPS = 1e-6
VMEM_LIMIT = 56 * 1024 * 1024

ADAM_LR, ADAM_B1, ADAM_B2, ADAM_EPS, ADAM_WD, ADAM_STEP = 0.001, 0.9, 0.999, 1e-08, 0.01, 10
MESH = pl.DeviceIdType.MESH


def _cp(*sem):
    return pltpu.CompilerParams(dimension_semantics=sem, vmem_limit_bytes=VMEM_LIMIT)


def _tile(n, pref, mult=8):
    if n <= pref:
        return n
    for t in range(pref - pref % mult, 0, -mult):
        if n % t == 0:
            return t
    return n


def _silu(x):
    return x * jax.nn.sigmoid(x)


def _dsilu(x):
    s = jax.nn.sigmoid(x)
    return s * (1.0 + x * (1.0 - s))


def _dot(a, b, dims=((1,), (0,))):
    return lax.dot_general(a.astype(BF16), b.astype(BF16), (dims, ((), ())), preferred_element_type=F32)


def _dot_nt(a, b):
    return _dot(a, b, ((1,), (1,)))


def _dot_tn(a, b):
    return _dot(a, b, ((0,), (0,)))


def _dot_f32(a, b):
    dn = (((1,), (0,)), ((), ()))
    ah, bh = a.astype(BF16), b.astype(BF16)
    al, bl = (a - ah.astype(F32)).astype(BF16), (b - bh.astype(F32)).astype(BF16)
    mm = lambda x, y: lax.dot_general(x, y, dn, preferred_element_type=F32)
    return mm(ah, bh) + (mm(ah, bl) + mm(al, bh))


def _dot_split(a, m):
    hi = a.astype(BF16)
    lo = (a - hi.astype(F32)).astype(BF16)
    dn = (((1,), (0,)), ((), ()))
    return (lax.dot_general(hi, m, dn, preferred_element_type=F32)
            + lax.dot_general(lo, m, dn, preferred_element_type=F32))


def _iota(shape, dim):
    return lax.broadcasted_iota(jnp.int32, shape, dim)


def _col_to_row(col):
    n = col.shape[0]
    eye = _iota((n, n), 0) == _iota((n, n), 1)
    return jnp.sum(jnp.where(eye, col, 0.0), axis=0, keepdims=True)


def _row_to_col(row):
    n = row.shape[1]
    eye = _iota((n, n), 0) == _iota((n, n), 1)
    return jnp.sum(jnp.where(eye, row, 0.0), axis=1, keepdims=True)


def _exchange(arrs, gather, name):
    n = len(arrs)

    def body(*refs):
        ins, outs = refs[:n], refs[n:2 * n]
        send_sems, recv_sems, local_sems = refs[2 * n:]
        x, y, c = lax.axis_index("x"), lax.axis_index("y"), lax.axis_index("c")
        me = 4 * x + 2 * y + c
        sends = []
        for i in range(n):
            mine = pltpu.make_async_copy(ins[i] if gather else ins[i].at[me], outs[i].at[me], local_sems.at[i])
            mine.start()
            sends.append(mine)
        for k in range(1, NDEV):
            px, py, pc = x ^ (k >> 2), y ^ ((k >> 1) & 1), c ^ (k & 1)
            peer = 4 * px + 2 * py + pc
            for i in range(n):
                cp = pltpu.make_async_remote_copy(
                    src_ref=ins[i] if gather else ins[i].at[peer], dst_ref=outs[i].at[me],
                    send_sem=send_sems.at[i * NDEV + k], recv_sem=recv_sems.at[i * NDEV + k],
                    device_id=(px, py, pc), device_id_type=MESH)
                cp.start()
                sends.append(cp)
        for k in range(1, NDEV):
            px, py, pc = x ^ (k >> 2), y ^ ((k >> 1) & 1), c ^ (k & 1)
            peer = 4 * px + 2 * py + pc
            for i in range(n):
                pltpu.make_async_remote_copy(
                    src_ref=outs[i].at[peer], dst_ref=outs[i].at[peer],
                    send_sem=send_sems.at[i * NDEV + k], recv_sem=recv_sems.at[i * NDEV + k],
                    device_id=(px, py, pc), device_id_type=MESH).wait_recv()
        for i in range(n):
            sends[i].wait()
        for cp in sends[n:]:
            cp.wait_send()

    hbm = pl.BlockSpec(memory_space=pltpu.HBM)
    out_shape = tuple(jax.ShapeDtypeStruct(((NDEV,) + a.shape) if gather else a.shape, a.dtype) for a in arrs)
    return pl.pallas_call(
        body, name=name, out_shape=out_shape, in_specs=[hbm] * n, out_specs=tuple([hbm] * n),
        scratch_shapes=[pltpu.SemaphoreType.DMA((n * NDEV,)), pltpu.SemaphoreType.DMA((n * NDEV,)),
                        pltpu.SemaphoreType.DMA((n,))],
        compiler_params=pltpu.CompilerParams(has_side_effects=True),
    )(*arrs)


_HBM_SPEC = pl.BlockSpec(memory_space=pltpu.HBM)
_SEM_SPEC = pl.BlockSpec(memory_space=pltpu.SEMAPHORE)
_EFFECT = pltpu.SideEffectType.DATAFLOW_SIDE_EFFECTING


def _place():
    return lax.axis_index("x"), lax.axis_index("y"), lax.axis_index("c")


def _dev(px, py, pc):
    return 4 * px + 2 * py + pc


def _plan_direct(n, gather):
    def plan(ins, lnd, for_wait):
        x, y, c = _place()
        copies = []
        for k in range(1, NDEV):
            px, py, pc = x ^ (k >> 2), y ^ ((k >> 1) & 1), c ^ (k & 1)
            for i in range(n):
                slot = (_dev(px, py, pc) if for_wait else _dev(x, y, c)) if gather else k - 1
                copies.append((ins[i] if gather else ins[i].at[_dev(px, py, pc)], lnd[i].at[slot], (px, py, pc)))
        return copies
    return plan


def _plan_gather_chips(ins, lnd, for_wait):
    x, y, c = _place()
    copies = []
    for k in range(4):
        px, py, pc = (x, y, 1 - c) if k == 0 else (x ^ (k >> 1), y ^ (k & 1), c)
        copies.append((ins[0], lnd[0].at[_dev(px, py, pc) if for_wait else _dev(x, y, c)], (px, py, pc)))
    return copies


def _plan_gather_forward(ins, lnd, for_wait):
    x, y, c = _place()
    copies = []
    for k in range(1, 4):
        px, py = x ^ (k >> 1), y ^ (k & 1)
        copies.append((lnd[0].at[_dev(px, py, c)], lnd[0].at[_dev(px, py, 1 - c if for_wait else c)], (x, y, 1 - c)))
    return copies


def _plan_scatter_core(ins, lnd, for_wait):
    x, y, c = _place()
    return [(ins[0].at[_dev(x ^ (k >> 1), y ^ (k & 1), 1 - c)], lnd[0].at[k], (x, y, 1 - c)) for k in range(4)]


def _plan_scatter_chips(ins, lnd, for_wait):
    x, y, c = _place()
    return [(ins[0].at[k], lnd[0].at[k - 1], (x ^ (k >> 1), y ^ (k & 1), c)) for k in range(1, 4)]


def _plan_descriptors(plan, ins, lnd, send_sems, recv_sems, for_wait):
    return [pltpu.make_async_remote_copy(src_ref=src, dst_ref=dst, send_sem=send_sems.at[j], recv_sem=recv_sems.at[j],
                                         device_id=dev, device_id_type=MESH)
            for j, (src, dst, dev) in enumerate(plan(ins, lnd, for_wait))]


def _split_start(srcs, lands, plan, ncopies, name, after=None):
    ns, nl = len(srcs), len(lands)
    extra = [] if after is None else [after]

    def body(*refs):
        ins, lnd = refs[:ns], refs[ns:ns + nl]
        send_sems, recv_sems = refs[ns + nl + len(extra)], refs[ns + nl + len(extra) + 1]
        token = refs[-1]
        for cp in _plan_descriptors(plan, ins, lnd, send_sems, recv_sems, False):
            cp.start()
        token[...] = jnp.zeros_like(token)

    sems = pltpu.SemaphoreType.DMA((ncopies,))
    both = list(srcs) + list(lands)
    outs = pl.pallas_call(
        body, name=name,
        out_shape=(sems, sems, *[pltpu.HBM(a.shape, a.dtype) for a in both], jax.ShapeDtypeStruct((8, 128), F32)),
        in_specs=[_HBM_SPEC] * (ns + nl) + [pl.BlockSpec(memory_space=pl.ANY)] * len(extra),
        out_specs=(_SEM_SPEC, _SEM_SPEC, *[_HBM_SPEC] * (ns + nl), pl.BlockSpec(memory_space=pltpu.VMEM)),
        input_output_aliases={i: 2 + i for i in range(ns + nl)},
        compiler_params=pltpu.CompilerParams(has_side_effects=_EFFECT),
    )(*[pltpu.with_memory_space_constraint(a, pltpu.HBM) for a in both], *extra)
    return (outs[0], outs[1], list(outs[2:2 + ns]), list(outs[2 + ns:2 + ns + nl]), plan), outs[-1]


def _split_wait(state, after, name):
    send_sems, recv_sems, srcs, lands, plan = state
    ns, nl = len(srcs), len(lands)

    def body(*refs):
        ins, lnd = refs[:ns], refs[ns:ns + nl]
        for cp in _plan_descriptors(plan, ins, lnd, refs[ns + nl], refs[ns + nl + 1], True):
            cp.wait_send()
            cp.wait_recv()

    outs = pl.pallas_call(
        body, name=name,
        out_shape=tuple(pltpu.HBM(a.shape, a.dtype) for a in srcs + lands),
        in_specs=[_HBM_SPEC] * (ns + nl) + [_SEM_SPEC, _SEM_SPEC, pl.BlockSpec(memory_space=pl.ANY)],
        out_specs=tuple([_HBM_SPEC] * (ns + nl)), input_output_aliases={i: i for i in range(ns + nl)},
        compiler_params=pltpu.CompilerParams(has_side_effects=_EFFECT),
    )(*srcs, *lands, send_sems, recv_sems, after)
    return list(outs[ns:]), list(outs[:ns])


def _exchange_start(arrs, gather, name, after=None):
    lands = [lax.empty(((NDEV,) + a.shape) if gather else ((NDEV - 1,) + a.shape[1:]), a.dtype) for a in arrs]
    return _split_start(arrs, lands, _plan_direct(len(arrs), gather), len(arrs) * (NDEV - 1), name, after)


_exchange_wait = _split_wait


def _chip_sums(parts, zone, name):
    _, r, c = parts.shape
    tr = _tile(r, max(16, (1 << 19) // c // 16 * 16), 16)
    x, y, core = _place()
    mine = jnp.stack([lax.dynamic_index_in_dim(parts, _dev(x ^ (k >> 1), y ^ (k & 1), core), 0, keepdims=False)
                      for k in range(4)])

    def body(p_ref, z_ref, o_ref):
        o_ref[...] = (p_ref[...].astype(F32) + z_ref[...].astype(F32)).astype(o_ref.dtype)

    blk = pl.BlockSpec((1, tr, c), lambda k, i: (k, i, 0))
    return pl.pallas_call(
        body, name=name, grid=(4, r // tr), in_specs=[blk, blk], out_specs=blk,
        out_shape=jax.ShapeDtypeStruct((4, r, c), parts.dtype),
        compiler_params=_cp("parallel", "parallel"))(mine, zone)


def _own_index():
    return 4 * lax.axis_index("x") + 2 * lax.axis_index("y") + lax.axis_index("c")


def _with_own(land, mine):
    return lax.dynamic_update_index_in_dim(land, mine, _own_index(), 0)


def _matmul(a, b, *, mode, out_dtype, tm, tn, tk, name, add=None, after=None):
    if mode == "nn":
        (m, kd), (_, n) = a.shape, b.shape
    elif mode == "nt":
        (m, kd), (n, _) = a.shape, b.shape
    else:
        (kd, m), (_, n) = a.shape, b.shape
    tm, tn, tk = _tile(m, tm, 16), _tile(n, tn, 128), _tile(kd, tk, 128)
    nk = kd // tk
    a_spec = pl.BlockSpec((tk, tm), lambda i, j, k: (k, i)) if mode == "tn" else pl.BlockSpec((tm, tk), lambda i, j, k: (i, k))
    b_spec = pl.BlockSpec((tn, tk), lambda i, j, k: (j, k)) if mode == "nt" else pl.BlockSpec((tk, tn), lambda i, j, k: (k, j))
    o_spec = pl.BlockSpec((tm, tn), lambda i, j, k: (i, j))
    dims = {"nn": ((1,), (0,)), "nt": ((1,), (1,)), "tn": ((0,), (0,))}[mode]

    def body(*refs, nk):
        a_ref, b_ref = refs[0], refs[1]
        o_ref, acc_ref = refs[-2], refs[-1]
        k = pl.program_id(2)

        @pl.when(k == 0)
        def _():
            acc_ref[...] = jnp.zeros_like(acc_ref)

        acc_ref[...] += lax.dot_general(a_ref[...], b_ref[...], (dims, ((), ())), preferred_element_type=F32)

        @pl.when(k == nk - 1)
        def _():
            r = acc_ref[...]
            if add is not None:
                r = r + refs[2][...]
            o_ref[...] = r.astype(o_ref.dtype)

    ins, specs = [a, b], [a_spec, b_spec]
    if add is not None:
        ins.append(add)
        specs.append(o_spec)
    if after is not None:
        ins.append(after)
        specs.append(pl.BlockSpec(after.shape, lambda i, j, k: (0, 0)))
    return pl.pallas_call(
        functools.partial(body, nk=nk), name=name, grid=(m // tm, n // tn, nk),
        in_specs=specs, out_specs=o_spec, out_shape=jax.ShapeDtypeStruct((m, n), out_dtype),
        scratch_shapes=[pltpu.VMEM((tm, tn), F32)], compiler_params=_cp("parallel", "parallel", "arbitrary"),
    )(*ins)


def _rms_fwd(h, w, name):
    lp, d = h.shape
    tm = _tile(lp, 384)

    def body(h_ref, w_ref, o_ref):
        xf = h_ref[...]
        r = lax.rsqrt(jnp.mean(xf * xf, axis=-1, keepdims=True) + EPS)
        o_ref[...] = (xf * r * w_ref[...]).astype(o_ref.dtype)

    return pl.pallas_call(
        body, name=name, grid=(lp // tm,),
        in_specs=[pl.BlockSpec((tm, d), lambda i: (i, 0)), pl.BlockSpec((1, d), lambda i: (0, 0))],
        out_specs=pl.BlockSpec((tm, d), lambda i: (i, 0)), out_shape=jax.ShapeDtypeStruct((lp, d), BF16),
        compiler_params=_cp("parallel"))(h, w)


def _rms_bwd(h, w, dhn, dres, name):
    lp, d = h.shape
    tm = _tile(lp, 192)

    def body(h_ref, w_ref, dy_ref, dres_ref, dh_ref, dhb_ref, dw_ref):
        xf = h_ref[...]
        r = lax.rsqrt(jnp.mean(xf * xf, axis=-1, keepdims=True) + EPS)
        xhat = xf * r
        dy = dy_ref[...]
        dxhat = dy * w_ref[...]
        dx = r * (dxhat - xhat * jnp.mean(dxhat * xhat, axis=-1, keepdims=True))
        dh = dres_ref[...] + dx
        dh_ref[...] = dh
        dhb_ref[...] = dh.astype(BF16)

        @pl.when(pl.program_id(0) == 0)
        def _():
            dw_ref[...] = jnp.zeros_like(dw_ref)

        dw_ref[...] += jnp.sum(dy * xhat, axis=0, keepdims=True)

    row = pl.BlockSpec((tm, d), lambda i: (i, 0))
    vec = pl.BlockSpec((1, d), lambda i: (0, 0))
    return pl.pallas_call(
        body, name=name, grid=(lp // tm,), in_specs=[row, vec, row, row], out_specs=(row, row, vec),
        out_shape=(jax.ShapeDtypeStruct((lp, d), F32), jax.ShapeDtypeStruct((lp, d), BF16),
                   jax.ShapeDtypeStruct((1, d), F32)),
        compiler_params=_cp("arbitrary"))(h, w, dhn, dres)


def _conv_pre(xx, w, rows, off):
    acc = None
    for j in range(CONV_K):
        sh = CONV_K - 1 - j
        term = (pltpu.roll(xx, sh, 0) if sh else xx)[off:off + rows] * w[j]
        acc = term if acc is None else acc + term
    return acc


def _conv_fwd(proj, conv_w, ncols, name):
    lp = proj.shape[0]
    tm, tc = _tile(lp, 384), _tile(ncols, 1024, 128)
    hb = tm // 8

    def body(x_ref, xb_ref, w_ref, o_ref):
        before = jnp.where(pl.program_id(0) > 0, xb_ref[...], 0.0)
        xx = jnp.concatenate([before, x_ref[...]], axis=0)
        o_ref[...] = _silu(_conv_pre(xx, [w_ref[j:j + 1, :] for j in range(CONV_K)], tm, 8))

    return pl.pallas_call(
        body, name=name, grid=(lp // tm, ncols // tc),
        in_specs=[pl.BlockSpec((tm, tc), lambda i, j: (i, j)),
                  pl.BlockSpec((8, tc), lambda i, j: (jnp.maximum(i * hb - 1, 0), j)),
                  pl.BlockSpec((CONV_K, tc), lambda i, j: (0, j))],
        out_specs=pl.BlockSpec((tm, tc), lambda i, j: (i, j)),
        out_shape=jax.ShapeDtypeStruct((lp, ncols), F32), compiler_params=_cp("parallel", "parallel"))(proj, proj, conv_w)


def _conv_bwd(proj, col0, conv_w, dact, name):
    lp, ncols = dact.shape
    tm, tc = _tile(lp, 384), _tile(ncols, 512, 128)
    hb, nt, cb0 = tm // 8, lp // tm, col0 // tc
    assert col0 % tc == 0

    def body(x_ref, xb_ref, xa_ref, d_ref, da_ref, w_ref, dx_ref, dw_ref):
        i = pl.program_id(1)
        before = jnp.where(i > 0, xb_ref[...], 0.0)
        last = i == nt - 1
        xx = jnp.concatenate([before, x_ref[...], jnp.where(last, 0.0, xa_ref[...])], axis=0)
        w = [w_ref[j:j + 1, :] for j in range(CONV_K)]
        pre = _conv_pre(xx, w, tm + 8, 8)
        dd = jnp.concatenate([d_ref[...], jnp.where(last, 0.0, da_ref[...])], axis=0)
        dpre = dd * _dsilu(pre)
        dx = None
        for j in range(CONV_K):
            sh = CONV_K - 1 - j
            term = (pltpu.roll(dpre, tm + 8 - sh, 0) if sh else dpre)[:tm] * w[j]
            dx = term if dx is None else dx + term
        dx_ref[...] = dx.astype(BF16)

        @pl.when(i == 0)
        def _():
            dw_ref[...] = jnp.zeros_like(dw_ref)

        for j in range(CONV_K):
            sh = CONV_K - 1 - j
            xs = (pltpu.roll(xx, sh, 0) if sh else xx)[8:8 + tm]
            dw_ref[j:j + 1, :] += jnp.sum(dpre[:tm] * xs, axis=0, keepdims=True)

    return pl.pallas_call(
        body, name=name, grid=(ncols // tc, nt),
        in_specs=[pl.BlockSpec((tm, tc), lambda j, i: (i, cb0 + j)),
                  pl.BlockSpec((8, tc), lambda j, i: (jnp.maximum(i * hb - 1, 0), cb0 + j)),
                  pl.BlockSpec((8, tc), lambda j, i: (jnp.minimum((i + 1) * hb, nt * hb - 1), cb0 + j)),
                  pl.BlockSpec((tm, tc), lambda j, i: (i, j)),
                  pl.BlockSpec((8, tc), lambda j, i: (jnp.minimum((i + 1) * hb, nt * hb - 1), j)),
                  pl.BlockSpec((CONV_K, tc), lambda j, i: (0, j))],
        out_specs=(pl.BlockSpec((tm, tc), lambda j, i: (i, j)), pl.BlockSpec((CONV_K, tc), lambda j, i: (0, j))),
        out_shape=(jax.ShapeDtypeStruct((lp, ncols), BF16), jax.ShapeDtypeStruct((CONV_K, ncols), F32)),
        compiler_params=_cp("parallel", "arbitrary"))(proj, proj, proj, dact, dact, conv_w)


def _softplus(x):
    return jnp.maximum(x, 0.0) + jnp.log(1.0 + jnp.exp(-jnp.abs(x)))


def _gates_fwd(gl, a_log2, dt_bias2, name):
    lp, w2 = gl.shape
    hv = w2 // 2

    def body(gl_ref, al_ref, dt_ref, o_ref):
        x = gl_ref[...]
        live = _iota((lp, 1), 0) >= INERT
        is_beta = _iota((1, w2), 1) < hv
        g = -jnp.exp(al_ref[...]) * _softplus(x + dt_ref[...])
        o_ref[...] = jnp.where(live, jnp.where(is_beta, jax.nn.sigmoid(x), g), 0.0)

    return pl.pallas_call(body, name=name, out_shape=jax.ShapeDtypeStruct((lp, w2), F32))(gl, a_log2, dt_bias2)


def _gates_bwd(gl, a_log2, dt_bias2, dbg, name):
    lp, w2 = gl.shape
    hv = w2 // 2

    def body(gl_ref, al_ref, dt_ref, d_ref, dl_ref, dal_ref, ddt_ref):
        x = gl_ref[...]
        live = _iota((lp, 1), 0) >= INERT
        is_beta = _iota((1, w2), 1) < hv
        d = jnp.where(live, d_ref[...], 0.0)
        beta = jax.nn.sigmoid(x)
        ea = jnp.exp(al_ref[...])
        u = x + dt_ref[...]
        dg = jnp.where(is_beta, 0.0, d)
        dal_ref[...] = jnp.sum(dg * (-ea) * _softplus(u), axis=0, keepdims=True)
        du = dg * (-ea) * jax.nn.sigmoid(u)
        ddt_ref[...] = jnp.sum(du, axis=0, keepdims=True)
        dl_ref[...] = jnp.where(is_beta, d * beta * (1.0 - beta), du).astype(BF16)

    vec = jax.ShapeDtypeStruct((1, w2), F32)
    return pl.pallas_call(
        body, name=name, out_shape=(jax.ShapeDtypeStruct((lp, w2), BF16), vec, vec))(gl, a_log2, dt_bias2, dbg)


def _l2n(x):
    r = lax.rsqrt(jnp.sum(x * x, axis=-1, keepdims=True) + EPS)
    return x * r, r


def _tri_inverse(mats):
    eye = (_iota((CH, CH), 0) == _iota((CH, CH), 1)).astype(F32)
    ts = [eye - a for a in mats]
    ps = [_dot_f32(a, a) for a in mats]
    n = 2
    while n < CH:
        ts = [t + _dot_f32(t, p) for t, p in zip(ts, ps)]
        n *= 2
        if n < CH:
            ps = [_dot_f32(p, p) for p in ps]
    return ts


def _chunk_local(qn, kn, v, b_row, g_row):
    ri, ci = _iota((CH, CH), 0), _iota((CH, CH), 1)
    incl, strict = ri >= ci, ri > ci
    gam_col = jnp.sum(jnp.where(incl, g_row, 0.0), axis=1, keepdims=True)
    gam_row = _col_to_row(gam_col)
    b_col = _row_to_col(b_row)
    dec = jnp.exp(jnp.where(incl, gam_col - gam_row, -jnp.inf))
    eg = jnp.exp(gam_col)
    gl = jnp.sum(g_row, axis=1, keepdims=True)
    ekd = jnp.exp(gl - gam_col)
    kb = kn * b_col
    a = jnp.where(strict, _dot_nt(kb, kn) * dec, 0.0)
    p = jnp.where(incl, _dot_nt(qn, kn) * dec, 0.0)
    return dict(dec=dec, eg=eg, ekd=ekd, kb=kb, vb=v * b_col, a=a, kbg=kb * eg, p=p, qd=qn * eg, kd=kn * ekd,
                b_col=b_col, incl=incl, strict=strict)


def _chunks_per_step(nc):
    return max(g for g in (1, 2, 3, 6) if nc % g == 0)


def _heads_per_step(hv):
    return min(hv, 8)


def _delta_local(act, gates, key_w, name):
    lp = act.shape[0]
    hk, nc = key_w // HD, lp // CH
    hv = 2 * hk
    g = _chunks_per_step(nc)
    tr = g * CH

    def body(q_ref, k_ref, v_ref, g_ref, u_ref, w_ref, qd_ref, kd_ref, p_ref, t_ref):
        items = []
        for j in range(g):
            rows = slice(j * CH, (j + 1) * CH)
            qn = _l2n(q_ref[rows, :])[0] * (HD ** -0.5)
            kn = _l2n(k_ref[rows, :])[0]
            for e in range(2):
                cols = slice(e * HD, (e + 1) * HD)
                r = _chunk_local(qn, kn, v_ref[rows, cols], g_ref[0, j, e:e + 1, :], g_ref[0, j, 2 + e:3 + e, :])
                qd_ref[rows, cols] = r["qd"].astype(BF16)
                kd_ref[rows, cols] = r["kd"].astype(BF16)
                p_ref[e, rows, :] = r["p"].astype(BF16)
                items.append((rows, cols, e, r["a"], r["vb"].astype(BF16), r["kbg"].astype(BF16)))
        ts = [t.astype(BF16) for t in _tri_inverse([it[3] for it in items])]
        us = [_dot(t, it[4]) for t, it in zip(ts, items)]
        ws = [_dot(t, it[5]) for t, it in zip(ts, items)]
        for (rows, cols, e, _, _, _), t, u, w in zip(items, ts, us, ws):
            u_ref[rows, cols] = u
            w_ref[rows, cols] = w.astype(BF16)
            t_ref[e, rows, :] = t

    wide = pl.BlockSpec((tr, 2 * HD), lambda h, c: (c, h))
    sq = pl.BlockSpec((2, tr, CH), lambda h, c: (h, c, 0))
    wshape = lambda dt: jax.ShapeDtypeStruct((lp, hv * HD), dt)
    sshape = jax.ShapeDtypeStruct((hv, lp, CH), BF16)
    return pl.pallas_call(
        body, name=name, grid=(hk, nc // g),
        in_specs=[pl.BlockSpec((tr, HD), lambda h, c: (c, h)),
                  pl.BlockSpec((tr, HD), lambda h, c: (c, hk + h)),
                  pl.BlockSpec((tr, 2 * HD), lambda h, c: (c, hk + h)),
                  pl.BlockSpec((1, g, 8, CH), lambda h, c: (h, c, 0, 0))],
        out_specs=(wide, wide, wide, wide, sq, sq),
        out_shape=(wshape(F32), wshape(BF16), wshape(BF16), wshape(BF16), sshape, sshape),
        compiler_params=_cp("parallel", "parallel"))(act, act, act, gates)


def _chunk_decay(g_ref, e):
    return jnp.exp(jnp.sum(g_ref[e // 2, 0, 2 + e % 2:3 + e % 2, :], axis=1, keepdims=True))


def _delta_scan(u, w, qd, kd, p, gates, name):
    lp, val = u.shape
    hv, nc = val // HD, lp // CH
    nh = _heads_per_step(hv)

    def body(u_ref, w_ref, qd_ref, kd_ref, p_ref, g_ref, o_ref, vn_ref, st_ref, s_scr):
        @pl.when(pl.program_id(1) == 0)
        def _():
            s_scr[...] = jnp.zeros_like(s_scr)

        heads = range(nh)
        col = lambda e: slice(e * HD, (e + 1) * HD)
        ss = [s_scr[e] for e in heads]
        sb = [s.astype(BF16) for s in ss]
        for e in heads:
            st_ref[0, e] = ss[e]
        ws = [_dot(w_ref[:, col(e)], sb[e]) for e in heads]
        qs = [_dot(qd_ref[:, col(e)], sb[e]) for e in heads]
        vns = [(u_ref[:, col(e)] - ws[e]).astype(BF16) for e in heads]
        pv = [_dot(p_ref[e], vns[e]) for e in heads]
        kv = [_dot_tn(kd_ref[:, col(e)], vns[e]) for e in heads]
        for e in heads:
            o_ref[:, col(e)] = qs[e] + pv[e]
            s_scr[e] = _chunk_decay(g_ref, e) * ss[e] + kv[e]
            vn_ref[:, col(e)] = vns[e]

    wide = pl.BlockSpec((CH, nh * HD), lambda h, c: (c, h))
    return pl.pallas_call(
        body, name=name, grid=(hv // nh, nc),
        in_specs=[wide, wide, wide, wide, pl.BlockSpec((nh, CH, CH), lambda h, c: (h, c, 0)),
                  pl.BlockSpec((nh // 2, 1, 8, CH), lambda h, c: (h, c, 0, 0))],
        out_specs=(wide, wide, pl.BlockSpec((1, nh, HD, HD), lambda h, c: (c, h, 0, 0))),
        out_shape=(jax.ShapeDtypeStruct((lp, val), F32), jax.ShapeDtypeStruct((lp, val), BF16),
                   jax.ShapeDtypeStruct((nc, hv, HD, HD), F32)),
        scratch_shapes=[pltpu.VMEM((nh, HD, HD), F32)],
        compiler_params=_cp("parallel", "arbitrary"))(u, w, qd, kd, p, gates)


def _delta_scan_bwd(do, w, qd, kd, p, vn, states, gates, name):
    lp, val = do.shape
    hv, nc = val // HD, lp // CH
    nh = _heads_per_step(hv)

    def body(do_ref, w_ref, qd_ref, kd_ref, p_ref, vn_ref, st_ref, g_ref,
             dvn_ref, dw_ref, dqd_ref, dkd_ref, dp_ref, sd_ref, ds_scr):
        @pl.when(pl.program_id(1) == 0)
        def _():
            ds_scr[...] = jnp.zeros_like(ds_scr)

        incl = _iota((CH, CH), 0) >= _iota((CH, CH), 1)
        heads = range(nh)
        col = lambda e: slice(e * HD, (e + 1) * HD)
        ss = [st_ref[0, e] for e in heads]
        dss = [ds_scr[e] for e in heads]
        sb = [s.astype(BF16) for s in ss]
        dsb = [d.astype(BF16) for d in dss]
        dos = [do_ref[:, col(e)].astype(BF16) for e in heads]
        egl = [_chunk_decay(g_ref, e) for e in heads]
        pdo = [_dot_tn(p_ref[e], dos[e]) for e in heads]
        kds = [_dot(kd_ref[:, col(e)], dsb[e]) for e in heads]
        qdo = [_dot_tn(qd_ref[:, col(e)], dos[e]) for e in heads]
        dqd = [_dot_nt(dos[e], sb[e]) for e in heads]
        dkd = [_dot_nt(vn_ref[:, col(e)], dsb[e]) for e in heads]
        dpp = [_dot_nt(dos[e], vn_ref[:, col(e)]) for e in heads]
        dvn = [(pdo[e] + kds[e]).astype(BF16) for e in heads]
        wdv = [_dot_tn(w_ref[:, col(e)], dvn[e]) for e in heads]
        dws = [_dot_nt(dvn[e], sb[e]) for e in heads]
        for e in heads:
            ds_scr[e] = qdo[e] + egl[e] * dss[e] - wdv[e]
            dvn_ref[:, col(e)] = dvn[e]
            dw_ref[:, col(e)] = (-dws[e]).astype(BF16)
            dqd_ref[:, col(e)] = dqd[e]
            dkd_ref[:, col(e)] = dkd[e]
            dp_ref[e] = jnp.where(incl, dpp[e], 0.0)
            sd_ref[0, 0, e:e + 1, :] = jnp.broadcast_to(egl[e] * jnp.sum(ss[e] * dss[e], keepdims=True), (1, HD))

    rev = lambda c: nc - 1 - c
    wide = pl.BlockSpec((CH, nh * HD), lambda h, c: (rev(c), h))
    sq = pl.BlockSpec((nh, CH, CH), lambda h, c: (h, rev(c), 0))
    wshape = lambda dt: jax.ShapeDtypeStruct((lp, val), dt)
    return pl.pallas_call(
        body, name=name, grid=(hv // nh, nc),
        in_specs=[wide, wide, wide, wide, sq, wide, pl.BlockSpec((1, nh, HD, HD), lambda h, c: (rev(c), h, 0, 0)),
                  pl.BlockSpec((nh // 2, 1, 8, CH), lambda h, c: (h, rev(c), 0, 0))],
        out_specs=(wide, wide, wide, wide, sq, pl.BlockSpec((1, 1, nh, HD), lambda h, c: (h, rev(c), 0, 0))),
        out_shape=(wshape(BF16), wshape(BF16), wshape(F32), wshape(F32), jax.ShapeDtypeStruct((hv, lp, CH), F32),
                   jax.ShapeDtypeStruct((hv // nh, nc, nh, HD), F32)),
        scratch_shapes=[pltpu.VMEM((nh, HD, HD), F32)],
        compiler_params=_cp("parallel", "arbitrary"))(do, w, qd, kd, p, vn, states, gates)


def _delta_local_bwd(act, gates, t, dvn, dw, dqd, dkd, dp, key_w, name):
    lp = act.shape[0]
    hk, nc = key_w // HD, lp // CH
    hv = 2 * hk
    g = _chunks_per_step(nc)
    tr = g * CH
    scale = HD ** -0.5

    def body(q_ref, k_ref, v_ref, g_ref, t_ref, dvn_ref, dw_ref, dqd_ref, dkd_ref, dp_ref, dq_ref, dk_ref, dv_ref, dg_ref):
        ri, ci = _iota((CH, CH), 0), _iota((CH, CH), 1)
        inner = lambda x, z: jnp.sum(x * z, axis=1, keepdims=True)
        norms, items = [], []
        for j in range(g):
            rows = slice(j * CH, (j + 1) * CH)
            qh, qr = _l2n(q_ref[rows, :])
            kn, kr = _l2n(k_ref[rows, :])
            qn = qh * scale
            norms.append((rows, qh, qr, kn, kr, qn))
            dg_ref[0, j, 4:8, :] = jnp.zeros((4, CH), F32)
            for e in range(2):
                cols = slice(e * HD, (e + 1) * HD)
                v = v_ref[rows, cols]
                r = _chunk_local(qn, kn, v, g_ref[0, j, e:e + 1, :], g_ref[0, j, 2 + e:3 + e, :])
                items.append(dict(r, j=j, e=e, rows=rows, cols=cols, v=v, kn=kn, qn=qn, t=t_ref[e, rows, :],
                                  dvn=dvn_ref[rows, cols], dw=dw_ref[rows, cols]))
        for it in items:
            it["dt"] = _dot_nt(it["dvn"], it["vb"]) + _dot_nt(it["dw"], it["kbg"])
            it["dvb"] = _dot_tn(it["t"], it["dvn"])
            it["dkbg"] = _dot_tn(it["t"], it["dw"])
        for it in items:
            it["x"] = _dot_tn(it["t"], it["dt"])
        for it in items:
            it["da"] = -jnp.where(it["strict"], _dot_nt(it["x"], it["t"]), 0.0)
        for it in items:
            dp = dp_ref[it["e"], it["rows"], :]
            it["gmat"] = it["da"] * it["a"] + dp * it["p"]
            mm, nn = (it["da"] * it["dec"]).astype(BF16), (dp * it["dec"]).astype(BF16)
            it["dkb"] = _dot(mm, it["kn"]) + it["dkbg"] * it["eg"]
            it["dkn"] = _dot_tn(mm, it["kb"]) + _dot_tn(nn, it["qn"])
            it["dqn"] = _dot(nn, it["kn"])
        for it in items:
            j, e, rows, cols = it["j"], it["e"], it["rows"], it["cols"]
            dqd, dkd, gmat, dkb = dqd_ref[rows, cols], dkd_ref[rows, cols], it["gmat"], it["dkb"]
            it["dkn"] = it["dkn"] + dkd * it["ekd"] + it["b_col"] * dkb
            it["dqn"] = it["dqn"] + dqd * it["eg"]
            dkd_kd = inner(dkd, it["kd"])
            dgam = (jnp.sum(gmat, axis=1, keepdims=True) - _row_to_col(jnp.sum(gmat, axis=0, keepdims=True))
                    + inner(dqd, it["qd"]) + inner(it["dkbg"], it["kbg"]) - dkd_kd)
            dgl = jnp.max(g_ref[0, j, 4 + e:5 + e, :], axis=1, keepdims=True) + jnp.sum(dkd_kd, keepdims=True)
            dgam = dgam + jnp.where(_iota((CH, 1), 0) == CH - 1, dgl, 0.0)
            dg_ref[0, j, 2 + e:3 + e, :] = jnp.sum(jnp.where(ri >= ci, dgam, 0.0), axis=0, keepdims=True)
            dg_ref[0, j, e:e + 1, :] = _col_to_row(inner(dkb, it["kn"]) + inner(it["dvb"], it["v"]))
            dv_ref[rows, cols] = it["b_col"] * it["dvb"]
        for j, (rows, qh, qr, kn, kr, _) in enumerate(norms):
            dqh = (items[2 * j]["dqn"] + items[2 * j + 1]["dqn"]) * scale
            dkn = items[2 * j]["dkn"] + items[2 * j + 1]["dkn"]
            dq_ref[rows, :] = qr * (dqh - qh * jnp.sum(dqh * qh, axis=1, keepdims=True))
            dk_ref[rows, :] = kr * (dkn - kn * jnp.sum(dkn * kn, axis=1, keepdims=True))

    narrow = pl.BlockSpec((tr, HD), lambda h, c: (c, h))
    wide = pl.BlockSpec((tr, 2 * HD), lambda h, c: (c, h))
    sq = pl.BlockSpec((2, tr, CH), lambda h, c: (h, c, 0))
    gate = pl.BlockSpec((1, g, 8, CH), lambda h, c: (h, c, 0, 0))
    return pl.pallas_call(
        body, name=name, grid=(hk, nc // g),
        in_specs=[narrow, pl.BlockSpec((tr, HD), lambda h, c: (c, hk + h)),
                  pl.BlockSpec((tr, 2 * HD), lambda h, c: (c, hk + h)), gate, sq, wide, wide, wide, wide, sq],
        out_specs=(narrow, narrow, wide, gate),
        out_shape=(jax.ShapeDtypeStruct((lp, key_w), F32), jax.ShapeDtypeStruct((lp, key_w), F32),
                   jax.ShapeDtypeStruct((lp, hv * HD), F32), jax.ShapeDtypeStruct((hk, nc, 8, CH), F32)),
        compiler_params=_cp("parallel", "parallel"))(act, act, act, gates, t, dvn, dw, dqd, dkd, dp)


def _outnorm_fwd(o, proj, z_col0, w, name):
    lp, val = o.shape
    tm, zb = _tile(lp, 1056), z_col0 // HD

    def body(o_ref, z_ref, w_ref, y_ref):
        xf = o_ref[...]
        r = lax.rsqrt(jnp.mean(xf * xf, axis=-1, keepdims=True) + EPS)
        y_ref[...] = (xf * r * w_ref[...] * _silu(z_ref[...])).astype(BF16)

    return pl.pallas_call(
        body, name=name, grid=(lp // tm, val // HD),
        in_specs=[pl.BlockSpec((tm, HD), lambda i, h: (i, h)), pl.BlockSpec((tm, HD), lambda i, h: (i, zb + h)),
                  pl.BlockSpec((1, HD), lambda i, h: (0, 0))],
        out_specs=pl.BlockSpec((tm, HD), lambda i, h: (i, h)), out_shape=jax.ShapeDtypeStruct((lp, val), BF16),
        compiler_params=_cp("parallel", "parallel"))(o, proj, w)


def _outnorm_bwd(o, proj, z_col0, w, dy, name):
    lp, val = o.shape
    tm, zb = _tile(lp, 1056), z_col0 // HD

    def body(o_ref, z_ref, w_ref, dy_ref, do_ref, dz_ref, dw_ref):
        xf, z, d = o_ref[...], z_ref[...], dy_ref[...]
        r = lax.rsqrt(jnp.mean(xf * xf, axis=-1, keepdims=True) + EPS)
        xhat = xf * r
        dn = d * _silu(z)
        dz_ref[...] = (d * xhat * w_ref[...] * _dsilu(z)).astype(BF16)
        dxhat = dn * w_ref[...]
        do_ref[...] = r * (dxhat - xhat * jnp.mean(dxhat * xhat, axis=-1, keepdims=True))

        @pl.when((pl.program_id(0) == 0) & (pl.program_id(1) == 0))
        def _():
            dw_ref[...] = jnp.zeros_like(dw_ref)

        dw_ref[...] += jnp.sum(dn * xhat, axis=0, keepdims=True)

    blk = pl.BlockSpec((tm, HD), lambda i, h: (i, h))
    vec = pl.BlockSpec((1, HD), lambda i, h: (0, 0))
    return pl.pallas_call(
        body, name=name, grid=(lp // tm, val // HD),
        in_specs=[blk, pl.BlockSpec((tm, HD), lambda i, h: (i, zb + h)), vec, blk], out_specs=(blk, blk, vec),
        out_shape=(jax.ShapeDtypeStruct((lp, val), F32), jax.ShapeDtypeStruct((lp, val), BF16),
                   jax.ShapeDtypeStruct((1, HD), F32)),
        compiler_params=_cp("arbitrary", "arbitrary"))(o, proj, w, dy)


def _qknorm_fwd(proj, qw, kw, width, name):
    lp = proj.shape[0]
    tm, nh = _tile(lp, 1056), width // HD

    def body(q_ref, k_ref, v_ref, qw_ref, kw_ref, qo_ref, ko_ref, vo_ref):
        for x_ref, w_ref, o_ref in ((q_ref, qw_ref, qo_ref), (k_ref, kw_ref, ko_ref)):
            xf = x_ref[...]
            r = lax.rsqrt(jnp.mean(xf * xf, axis=-1, keepdims=True) + EPS)
            o_ref[...] = (xf * r * w_ref[...]).astype(BF16)
        vo_ref[...] = v_ref[...].astype(BF16)

    blk = lambda off: pl.BlockSpec((tm, HD), lambda i, h: (i, off + h))
    vec = pl.BlockSpec((1, HD), lambda i, h: (0, 0))
    shp = jax.ShapeDtypeStruct((lp, width), BF16)
    return pl.pallas_call(
        body, name=name, grid=(lp // tm, nh), in_specs=[blk(0), blk(nh), blk(2 * nh), vec, vec],
        out_specs=(blk(0), blk(0), blk(0)), out_shape=(shp, shp, shp),
        compiler_params=_cp("parallel", "parallel"))(proj, proj, proj, qw, kw)


def _qknorm_bwd(proj, qw, kw, dqn, dkn, width, name):
    lp = proj.shape[0]
    tm, nh = _tile(lp, 1056), width // HD

    def body(q_ref, k_ref, qw_ref, kw_ref, dqn_ref, dkn_ref, dq_ref, dk_ref, dqw_ref, dkw_ref):
        first = (pl.program_id(0) == 0) & (pl.program_id(1) == 0)
        for x_ref, w_ref, dy_ref, dx_ref, dw_ref in ((q_ref, qw_ref, dqn_ref, dq_ref, dqw_ref),
                                                       (k_ref, kw_ref, dkn_ref, dk_ref, dkw_ref)):
            xf, dy = x_ref[...], dy_ref[...]
            r = lax.rsqrt(jnp.mean(xf * xf, axis=-1, keepdims=True) + EPS)
            xhat = xf * r
            dxhat = dy * w_ref[...]
            dx_ref[...] = (r * (dxhat - xhat * jnp.mean(dxhat * xhat, axis=-1, keepdims=True))).astype(BF16)

            @pl.when(first)
            def _():
                dw_ref[...] = jnp.zeros_like(dw_ref)

            dw_ref[...] += jnp.sum(dy * xhat, axis=0, keepdims=True)

    blk = lambda off: pl.BlockSpec((tm, HD), lambda i, h: (i, off + h))
    vec = pl.BlockSpec((1, HD), lambda i, h: (0, 0))
    shp = jax.ShapeDtypeStruct((lp, width), BF16)
    vshp = jax.ShapeDtypeStruct((1, HD), F32)
    return pl.pallas_call(
        body, name=name, grid=(lp // tm, nh), in_specs=[blk(0), blk(nh), vec, vec, blk(0), blk(0)],
        out_specs=(blk(0), blk(0), vec, vec), out_shape=(shp, shp, vshp, vshp),
        compiler_params=_cp("arbitrary", "arbitrary"))(proj, proj, qw, kw, dqn, dkn)


def _sb_tq(lp):
    return 3 * QB if lp % (3 * QB) == 0 else QB


def _sb_rows(kb):
    return pl.ds(kb * QB if isinstance(kb, int) else pl.multiple_of(kb * QB, QB), QB)


def _sb_scores(qk, t_idx, kb, masked):
    z = qk * (HD ** -0.5)
    sp = jnp.log(1.0 + jnp.exp(-jnp.abs(z)))
    lsz = jnp.minimum(z, 0.0) - sp
    lk = -jnp.maximum(z, 0.0) - sp
    if not masked:
        return None, lsz, lk
    s_idx = kb * QB + _iota((1, QB), 1)
    valid = (s_idx < t_idx) & (s_idx >= INERT)
    return valid, lsz, jnp.where(valid, lk, 0.0)


def _sb_fwd(qn, kn, vv, proj, gate_col0, name):
    lp, width = qn.shape
    tq = _sb_tq(lp)
    nh, nq, gb, nsub = width // HD, lp // tq, gate_col0 // HD, tq // QB

    def body(q_ref, k_ref, v_ref, g_ref, o_ref, og_ref, tot_ref):
        qb = pl.program_id(1)
        q = q_ref[...]
        t_idx = qb * tq + _iota((tq, 1), 0)
        upper = (_iota((QB, QB), 0) > _iota((QB, QB), 1)).astype(BF16)

        def step(kg, carry, masked):
            run, acc = carry
            kbs = [kg * nsub + sub for sub in reversed(range(nsub))]
            rows = [_sb_rows(kb) for kb in kbs]
            qks = [_dot_nt(q, k_ref[r, :]) for r in rows]
            scores = [_sb_scores(qk, t_idx, kb, masked) for qk, kb in zip(qks, kbs)]
            sums = [_dot_split(lk, upper) for _, _, lk in scores]
            probs = []
            for (valid, lsz, lk), part in zip(scores, sums):
                a = jnp.exp(lsz + part + run)
                probs.append((jnp.where(valid, a, 0.0) if masked else a).astype(BF16))
                run = run + jnp.sum(lk, axis=1, keepdims=True)
            for a, r in zip(probs, rows):
                acc = acc + _dot(a, v_ref[r, :])
            return run, acc

        carry = step(qb, (jnp.zeros((tq, 1), F32), jnp.zeros((tq, HD), F32)), True)
        carry = lax.fori_loop(1, qb, lambda i, cr: step(qb - i, cr, False), carry)
        run, acc = lax.fori_loop(0, jnp.minimum(qb, 1), lambda _, cr: step(0, cr, True), carry)
        o_ref[...] = acc
        og_ref[...] = (acc * _silu(g_ref[...])).astype(BF16)
        tot_ref[0, 0] = _col_to_row(run)

    full = pl.BlockSpec((lp, HD), lambda h, i: (0, h))
    blk = pl.BlockSpec((tq, HD), lambda h, i: (i, h))
    return pl.pallas_call(
        body, name=name, grid=(nh, nq),
        in_specs=[blk, full, full, pl.BlockSpec((tq, HD), lambda h, i: (i, gb + h))],
        out_specs=(blk, blk, pl.BlockSpec((1, 1, 1, tq), lambda h, i: (h, i, 0, 0))),
        out_shape=(jax.ShapeDtypeStruct((lp, width), F32), jax.ShapeDtypeStruct((lp, width), BF16),
                   jax.ShapeDtypeStruct((nh, nq, 1, tq), F32)),
        compiler_params=_cp("parallel", "parallel"))(qn, kn, vv, proj)


def _sb_bwd(qn, kn, vv, proj, gate_col0, att, tot, dog, name):
    lp, width = qn.shape
    tq = _sb_tq(lp)
    nh, nq, gb, nsub = width // HD, lp // tq, gate_col0 // HD, tq // QB
    scale = HD ** -0.5

    def body(q_ref, k_ref, v_ref, g_ref, att_ref, tot_ref, dog_ref, dq_ref, dk_ref, dv_ref, dg_ref, dk_acc, dv_acc):
        qb = pl.program_id(1)

        @pl.when(qb == 0)
        def _():
            dk_acc[...] = jnp.zeros_like(dk_acc)
            dv_acc[...] = jnp.zeros_like(dv_acc)

        q, gate, dg_out = q_ref[...], g_ref[...], dog_ref[...]
        d_o = (dg_out * _silu(gate)).astype(BF16)
        dg_ref[...] = (dg_out * att_ref[...] * _dsilu(gate)).astype(BF16)
        total = _row_to_col(tot_ref[0, 0])
        t_idx = qb * tq + _iota((tq, 1), 0)
        ri, ci = _iota((QB, QB), 0), _iota((QB, QB), 1)
        lower_incl = (ri <= ci).astype(BF16)
        lower_excl = (ri < ci).astype(BF16)

        def step(kg, carry, masked):
            run, erun, dq = carry
            kbs = [kg * nsub + sub for sub in range(nsub)]
            rows = [_sb_rows(kb) for kb in kbs]
            qks = [_dot_nt(q, k_ref[r, :]) for r in rows]
            dprobs = [_dot_nt(d_o, v_ref[r, :]) for r in rows]
            scores = [_sb_scores(qk, t_idx, kb, masked) for qk, kb in zip(qks, kbs)]
            sums = [_dot_split(lk, lower_incl) for _, _, lk in scores]
            probs, es = [], []
            for (valid, lsz, lk), part, dprob in zip(scores, sums, dprobs):
                a = jnp.exp(lsz + (total - run - part))
                a = jnp.where(valid, a, 0.0) if masked else a
                probs.append(a.astype(BF16))
                es.append(a * dprob)
                run = run + jnp.sum(lk, axis=1, keepdims=True)
            esums = [_dot_split(e, lower_excl) for e in es]
            for a, r in zip(probs, rows):
                dv_acc[r, :] += _dot_tn(a, d_o)
            dzs = []
            for (valid, lsz, _), e, part in zip(scores, es, esums):
                sig = jnp.exp(lsz)
                dz = e * (1.0 - sig) - sig * (erun + part)
                dz = jnp.where(valid, dz, 0.0) if masked else dz
                dzs.append((dz * scale).astype(BF16))
                erun = erun + jnp.sum(e, axis=1, keepdims=True)
            for dz, r in zip(dzs, rows):
                dk_acc[r, :] += _dot_tn(dz, q)
                dq = dq + _dot(dz, k_ref[r, :])
            return run, erun, dq

        zero = jnp.zeros((tq, 1), F32)
        carry = step(0, (zero, zero, jnp.zeros((tq, HD), F32)), True)
        carry = lax.fori_loop(1, qb, lambda kg, cr: step(kg, cr, False), carry)
        _, _, dq = lax.fori_loop(0, jnp.minimum(qb, 1), lambda _, cr: step(qb, cr, True), carry)
        dq_ref[...] = dq

        @pl.when(qb == nq - 1)
        def _():
            dk_ref[...] = dk_acc[...]
            dv_ref[...] = dv_acc[...].astype(BF16)

    full = pl.BlockSpec((lp, HD), lambda h, i: (0, h))
    blk = pl.BlockSpec((tq, HD), lambda h, i: (i, h))
    return pl.pallas_call(
        body, name=name, grid=(nh, nq),
        in_specs=[blk, full, full, pl.BlockSpec((tq, HD), lambda h, i: (i, gb + h)), blk,
                  pl.BlockSpec((1, 1, 1, tq), lambda h, i: (h, i, 0, 0)), blk],
        out_specs=(blk, full, full, blk),
        out_shape=(jax.ShapeDtypeStruct((lp, width), F32), jax.ShapeDtypeStruct((lp, width), F32),
                   jax.ShapeDtypeStruct((lp, width), BF16), jax.ShapeDtypeStruct((lp, width), BF16)),
        scratch_shapes=[pltpu.VMEM((lp, HD), F32), pltpu.VMEM((lp, HD), F32)],
        compiler_params=_cp("parallel", "arbitrary"))(qn, kn, vv, proj, att, tot, dog)


def _loss_head(h, target, name):
    lp, d = h.shape
    tm = _tile(PAD, 128)
    nt = lp // tm
    npad = PAD // tm

    def body(h_ref, t_ref, l_ref, dh_ref, dhb_ref):
        i = pl.program_id(0)
        err = jnp.where(i >= npad, h_ref[...] - t_ref[...], 0.0)
        dh = err * (1.0 / d)
        dh_ref[...] = dh
        dhb_ref[...] = dh.astype(BF16)

        @pl.when(i == 0)
        def _():
            l_ref[...] = jnp.zeros_like(l_ref)

        l_ref[...] += (0.5 / d) * jnp.sum(err * err, keepdims=True)

    row = pl.BlockSpec((tm, d), lambda i: (i, 0))
    return pl.pallas_call(
        body, name=name, grid=(nt,),
        in_specs=[row, pl.BlockSpec((tm, d), lambda i: (jnp.maximum(i - npad, 0), 0))],
        out_specs=(pl.BlockSpec((1, 1), lambda i: (0, 0)), row, row),
        out_shape=(jax.ShapeDtypeStruct((1, 1), F32), jax.ShapeDtypeStruct((lp, d), F32),
                   jax.ShapeDtypeStruct((lp, d), BF16)),
        compiler_params=_cp("arbitrary"))(h, target)


def _adamw(parts, w, m, v, name, mine=None):
    r, c = w.shape
    tr = _tile(r, max(8, (1 << 18) // c // 8 * 8))
    nparts = parts.shape[0]

    def body(*refs):
        p_ref, w_ref, m_ref, v_ref = refs[:4]
        g_ref, d_ref, mo_ref, vo_ref = refs[-4:]
        g = p_ref[0].astype(F32)
        if mine is not None:
            g = refs[4][...].astype(F32) + g
        for k in range(1, nparts):
            g = g + p_ref[k].astype(F32)
        mn = ADAM_B1 * m_ref[...] + (1.0 - ADAM_B1) * g
        vn = ADAM_B2 * v_ref[...] + (1.0 - ADAM_B2) * jnp.square(g)
        m_hat = mn / (1.0 - ADAM_B1 ** ADAM_STEP)
        v_hat = vn / (1.0 - ADAM_B2 ** ADAM_STEP)
        g_ref[...] = g
        d_ref[...] = -ADAM_LR * (m_hat / (jnp.sqrt(v_hat) + ADAM_EPS) + ADAM_WD * w_ref[...])
        mo_ref[...] = mn
        vo_ref[...] = vn

    blk = pl.BlockSpec((tr, c), lambda i: (i, 0))
    shp = jax.ShapeDtypeStruct((r, c), F32)
    extra = [] if mine is None else [mine]
    return pl.pallas_call(
        body, name=name, grid=(r // tr,),
        in_specs=[pl.BlockSpec((nparts, tr, c), lambda i: (0, i, 0)), blk, blk, blk] + [blk] * len(extra),
        out_specs=(blk, blk, blk, blk), out_shape=(shp, shp, shp, shp), compiler_params=_cp("parallel"))(parts, w, m, v, *extra)


def _unshard_cols(g):
    return jnp.transpose(g, (1, 0, 2)).reshape(g.shape[1], NDEV * g.shape[2])


def _shard_cols(a):
    r, c = a.shape
    return jnp.transpose(a.reshape(r, NDEV, c // NDEV), (1, 0, 2))


def kernel(x, meta_tokens, dn_norm_w, dn_w_in, dn_conv_w, dn_a_log, dn_dt_bias, dn_out_norm_w, dn_w_out, sb_norm_w, sb_w_in, sb_q_norm_w, sb_k_norm_w, sb_w_out, loss_target, m_meta_tokens, m_dn_norm_w, m_dn_w_in, m_dn_conv_w, m_dn_a_log, m_dn_dt_bias, m_dn_out_norm_w, m_dn_w_out, m_sb_norm_w, m_sb_w_in, m_sb_q_norm_w, m_sb_k_norm_w, m_sb_w_out, v_meta_tokens, v_dn_norm_w, v_dn_w_in, v_dn_conv_w, v_dn_a_log, v_dn_dt_bias, v_dn_out_norm_w, v_dn_w_out, v_sb_norm_w, v_sb_w_in, v_sb_q_norm_w, v_sb_k_norm_w, v_sb_w_out):
    seq, d = x.shape[1], x.shape[2]
    lp = PAD + seq
    key_w = d
    val_w = 2 * d
    hv = val_w // HD
    conv_w_cols = 2 * key_w + val_w
    main_w = conv_w_cols + val_w
    sb_w = d
    nc = lp // CH
    hk = key_w // HD

    (g_meta, g_sbn, g_conv) = _exchange([meta_tokens, sb_norm_w, dn_conv_w[0]], True, "gather_vectors")
    w_in0_mine = dn_w_in[0].astype(BF16)
    st_a, tok_a = _split_start([w_in0_mine], [lax.empty((NDEV,) + w_in0_mine.shape, BF16)], _plan_gather_chips, 4,
                               "gather_w_in0_chips_start", after=g_meta)
    meta = _unshard_cols(g_meta) + tok_a[:1, :1]
    sbn_w = _unshard_cols(g_sbn)
    conv_w = _unshard_cols(g_conv)
    h0 = jnp.concatenate([jnp.zeros((INERT, d), F32), meta, x[0]], axis=0)
    hn0 = _rms_fwd(h0, dn_norm_w, "dn_norm")
    lands_a, srcs_a = _split_wait(st_a, hn0, "gather_w_in0_chips_wait")
    st_a2, tok_a2 = _split_start([], lands_a, _plan_gather_forward, 3, "gather_w_in0_forward_start")
    st_b, tok_b = _exchange_start([dn_w_out[0].astype(BF16), sb_w_in[0].astype(BF16), sb_w_out[0].astype(BF16)],
                                  True, "gather_w_rest_start", after=tok_a2)
    lands_a, _ = _split_wait(st_a2, tok_b, "gather_w_in0_forward_wait")
    w_in0 = _unshard_cols(_with_own(lands_a[0], srcs_a[0]))
    w_in0_main, w_in0_gate = w_in0[:, :main_w], w_in0[:, main_w:]

    proj0 = _matmul(hn0, w_in0_main, mode="nn", out_dtype=F32, tm=1056, tn=512, tk=4096, name="dn_in_proj")
    gl0 = _matmul(hn0, w_in0_gate, mode="nn", out_dtype=F32, tm=1056, tn=512, tk=4096, name="dn_gate_proj")
    act0 = _conv_fwd(proj0, conv_w, conv_w_cols, "dn_conv")
    a_log2 = jnp.concatenate([jnp.zeros_like(dn_a_log), dn_a_log], axis=1)
    dt_bias2 = jnp.concatenate([jnp.zeros_like(dn_dt_bias), dn_dt_bias], axis=1)
    bg = _gates_fwd(gl0, a_log2, dt_bias2, "dn_gates")
    pack = lambda t: jnp.transpose(t.reshape(nc, CH, hk, 2), (2, 0, 3, 1))
    gates = jnp.concatenate([pack(bg[:, :hv]), pack(bg[:, hv:]), jnp.zeros((hk, nc, 4, CH), F32)], axis=2)
    u0, w0, qd0, kd0, p0, t0 = _delta_local(act0, gates, key_w, "dn_delta_local")
    o0, vn0, states = _delta_scan(u0, w0, qd0, kd0, p0, gates, "dn_delta_scan")
    o0g = _outnorm_fwd(o0, proj0, conv_w_cols, dn_out_norm_w, "dn_out_norm")
    lands_b, srcs_b = _exchange_wait(st_b, o0g, "gather_w_rest_wait")
    g_out0, g_in1, g_out1 = [_with_own(l, s) for l, s in zip(lands_b, srcs_b)]
    w_out0 = g_out0.reshape(val_w, d)
    w_in1 = _unshard_cols(g_in1)
    w_out1 = g_out1.reshape(sb_w, d)
    h1 = _matmul(o0g, w_out0, mode="nn", out_dtype=F32, tm=1056, tn=512, tk=4096, name="dn_out_proj", add=h0)
    hn1 = _rms_fwd(h1, sbn_w, "sb_norm")
    proj1 = _matmul(hn1, w_in1, mode="nn", out_dtype=F32, tm=1056, tn=512, tk=4096, name="sb_in_proj")
    qn1, kn1, vv1 = _qknorm_fwd(proj1, sb_q_norm_w, sb_k_norm_w, sb_w, "sb_qk_norm")
    att1, o1g, tot1 = _sb_fwd(qn1, kn1, vv1, proj1, 3 * sb_w, "sb_attn")
    h2 = _matmul(o1g, w_out1, mode="nn", out_dtype=F32, tm=1056, tn=512, tk=4096, name="sb_out_proj", add=h1)
    loss_part, dh2, dh2b = _loss_head(h2, loss_target[0], "loss_head")
    loss = lax.psum(loss_part[0, 0], ("x", "y", "c"))

    p_out1 = _matmul(o1g, dh2b, mode="tn", out_dtype=BF16, tm=1024, tn=512, tk=lp, name="sb_out_wgrad")
    do1g = _matmul(dh2b, w_out1, mode="nt", out_dtype=F32, tm=1056, tn=512, tk=4096, name="sb_out_dgrad")
    dqn1, dkn1, dv1, dgate1 = _sb_bwd(qn1, kn1, vv1, proj1, 3 * sb_w, att1, tot1, do1g, "sb_attn_bwd")
    dq1, dk1, d_qw, d_kw = _qknorm_bwd(proj1, sb_q_norm_w, sb_k_norm_w, dqn1, dkn1, sb_w, "sb_qk_norm_bwd")
    dproj1 = jnp.concatenate([dq1, dk1, dv1, dgate1], axis=1)
    p_in1 = _matmul(hn1, dproj1, mode="tn", out_dtype=BF16, tm=1024, tn=512, tk=lp, name="sb_in_wgrad")
    st_s1, tok_s1 = _exchange_start([p_out1.reshape(NDEV, sb_w // NDEV, d), _shard_cols(p_in1)], False, "scatter_sb_start")
    dhn1 = _matmul(dproj1, w_in1, mode="nt", out_dtype=F32, tm=1056, tn=512, tk=4096, name="sb_in_dgrad", after=tok_s1)
    dh1, dh1b, d_sbn = _rms_bwd(h1, sbn_w, dhn1, dh2, "sb_norm_bwd")

    p_out0 = _matmul(o0g, dh1b, mode="tn", out_dtype=BF16, tm=1024, tn=512, tk=lp, name="dn_out_wgrad")
    st_s2, tok_s2 = _exchange_start([p_out0.reshape(NDEV, val_w // NDEV, d)], False, "scatter_dn_out_start")
    do0g = _matmul(dh1b, w_out0, mode="nt", out_dtype=F32, tm=1056, tn=512, tk=4096, name="dn_out_dgrad", after=tok_s2)
    do0, dz0, d_onw = _outnorm_bwd(o0, proj0, conv_w_cols, dn_out_norm_w, do0g, "dn_out_norm_bwd")
    dvn0, dw0, dqd0, dkd0, dp0, sd0 = _delta_scan_bwd(do0, w0, qd0, kd0, p0, vn0, states, gates, "dn_delta_scan_bwd")
    sd_rows = jnp.transpose(jnp.transpose(sd0[..., 0], (0, 2, 1)).reshape(hk, 2, nc), (0, 2, 1))
    gates_b = jnp.concatenate([gates[:, :, :4], jnp.broadcast_to(sd_rows[..., None], (hk, nc, 2, CH)),
                               jnp.zeros((hk, nc, 2, CH), F32)], axis=2)
    dq_act, dk_act, dv_act, dgates = _delta_local_bwd(act0, gates_b, t0, dvn0, dw0, dqd0, dkd0, dp0, key_w,
                                                      "dn_delta_local_bwd")
    unpack = lambda t: jnp.transpose(t, (1, 3, 0, 2)).reshape(lp, hv)
    dbg = jnp.concatenate([unpack(dgates[:, :, 0:2]), unpack(dgates[:, :, 2:4])], axis=1)
    dgl0, d_alog2, d_dtb2 = _gates_bwd(gl0, a_log2, dt_bias2, dbg, "dn_gates_bwd")
    d_alog, d_dtb = d_alog2[:, hv:], d_dtb2[:, hv:]
    dxq, dcw_q = _conv_bwd(proj0, 0, conv_w[:, :key_w], dq_act, "dn_conv_bwd_q")
    dxk, dcw_k = _conv_bwd(proj0, key_w, conv_w[:, key_w:2 * key_w], dk_act, "dn_conv_bwd_k")
    dxv, dcw_v = _conv_bwd(proj0, 2 * key_w, conv_w[:, 2 * key_w:], dv_act, "dn_conv_bwd_v")
    dproj0 = jnp.concatenate([dxq, dxk, dxv, dz0], axis=1)
    p_in0_main = _matmul(hn0, dproj0, mode="tn", out_dtype=BF16, tm=1024, tn=512, tk=lp, name="dn_in_wgrad")
    p_in0_gate = _matmul(hn0, dgl0, mode="tn", out_dtype=BF16, tm=1024, tn=512, tk=lp, name="dn_gate_wgrad")
    p_in0 = _shard_cols(jnp.concatenate([p_in0_main, p_in0_gate], axis=1))
    st_c, tok_c = _split_start([p_in0], [lax.empty((4,) + p_in0.shape[1:], BF16)], _plan_scatter_core, 4,
                               "scatter_dn_in_core_start")
    dhn0 = _matmul(dgl0, w_in0_gate, mode="nt", out_dtype=F32, tm=1056, tn=512, tk=4096, name="dn_gate_dgrad", after=tok_c)
    (zone_c,), (p_in0,) = _split_wait(st_c, dhn0, "scatter_dn_in_core_wait")
    sums_in0 = _chip_sums(p_in0, zone_c, "dn_in_chip_sums")
    st_s3, tok_s3 = _split_start([sums_in0], [lax.empty((3,) + p_in0.shape[1:], BF16)], _plan_scatter_chips, 3,
                                 "scatter_dn_in_chips_start")
    dhn0 = _matmul(dproj0, w_in0_main, mode="nt", out_dtype=F32, tm=1056, tn=512, tk=4096, name="dn_in_dgrad", add=dhn0,
                   after=tok_s3)
    dh0, _, d_dnn = _rms_bwd(h0, dn_norm_w, dhn0, dh1, "dn_norm_bwd")
    grad_x = dh0[PAD:][None]

    p_conv = _shard_cols(jnp.concatenate([dcw_q, dcw_k, dcw_v], axis=1))
    (r_meta, r_sbn, r_conv) = _exchange([_shard_cols(dh0[INERT:PAD]), _shard_cols(d_sbn), p_conv], False, "scatter_vector_grads")
    small = jnp.concatenate([d_dnn, d_alog, d_dtb, d_onw, d_qw, d_kw], axis=1)
    (r_small,) = _exchange([small], True, "gather_replicated_grads")
    outs = {}
    outs["meta_tokens"] = _adamw(r_meta, meta_tokens, m_meta_tokens, v_meta_tokens, "adamw_meta")
    outs["dn_conv_w"] = _adamw(r_conv, dn_conv_w[0], m_dn_conv_w[0], v_dn_conv_w[0], "adamw_dn_conv")
    outs["sb_norm_w"] = _adamw(r_sbn, sb_norm_w, m_sb_norm_w, v_sb_norm_w, "adamw_sb_norm")
    me = _own_index()
    own = lambda src: lax.dynamic_index_in_dim(src, me, 0, keepdims=False)
    (r_out1, r_in1), (s_out1, s_in1) = _exchange_wait(st_s1, r_small, "scatter_sb_wait")
    outs["sb_w_in"] = _adamw(r_in1, sb_w_in[0], m_sb_w_in[0], v_sb_w_in[0], "adamw_sb_w_in", mine=own(s_in1))
    outs["sb_w_out"] = _adamw(r_out1, sb_w_out[0], m_sb_w_out[0], v_sb_w_out[0], "adamw_sb_w_out", mine=own(s_out1))
    (r_out0,), (s_out0,) = _exchange_wait(st_s2, outs["sb_w_in"][1], "scatter_dn_out_wait")
    outs["dn_w_out"] = _adamw(r_out0, dn_w_out[0], m_dn_w_out[0], v_dn_w_out[0], "adamw_dn_w_out", mine=own(s_out0))
    cat = lambda *a: jnp.concatenate(a, axis=1)
    rep = _adamw(r_small, cat(dn_norm_w, dn_a_log, dn_dt_bias, dn_out_norm_w, sb_q_norm_w, sb_k_norm_w),
                 cat(m_dn_norm_w, m_dn_a_log, m_dn_dt_bias, m_dn_out_norm_w, m_sb_q_norm_w, m_sb_k_norm_w),
                 cat(v_dn_norm_w, v_dn_a_log, v_dn_dt_bias, v_dn_out_norm_w, v_sb_q_norm_w, v_sb_k_norm_w),
                 "adamw_replicated")
    off = 0
    for nm, wd in (("dn_norm_w", d), ("dn_a_log", hv), ("dn_dt_bias", hv), ("dn_out_norm_w", HD),
                   ("sb_q_norm_w", HD), ("sb_k_norm_w", HD)):
        outs[nm] = tuple(t[:, off:off + wd] for t in rep)
        off += wd
    behind = jnp.broadcast_to(outs["dn_w_out"][1][0, 0] + rep[1][0, 0] + outs["sb_w_out"][1][0, 0] + outs["meta_tokens"][1][0, 0]
                              + outs["dn_conv_w"][1][0, 0] + outs["sb_norm_w"][1][0, 0], (8, 128))
    (r_in0,), (sums_in0,) = _split_wait(st_s3, behind, "scatter_dn_in_chips_wait")
    outs["dn_w_in"] = _adamw(r_in0, dn_w_in[0], m_dn_w_in[0], v_dn_w_in[0], "adamw_dn_w_in", mine=sums_in0[0])
    lead =("dn_w_in", "dn_conv_w", "dn_w_out", "sb_w_in", "sb_w_out")
    order = ("meta_tokens", "dn_norm_w", "dn_w_in", "dn_conv_w", "dn_a_log", "dn_dt_bias", "dn_out_norm_w", "dn_w_out",
             "sb_norm_w", "sb_w_in", "sb_q_norm_w", "sb_k_norm_w", "sb_w_out")
    fix = lambda nm, t: t[None] if nm in lead else t
    result = [loss, grad_x]
    for kind in range(4):
        result += [fix(nm, outs[nm][kind]) for nm in order]
    return tuple(result)
```

```python
import functools

import jax
import jax.numpy as jnp
from jax import lax
from jax.experimental import pallas as pl
from jax.experimental.pallas import tpu as pltpu

F32 = jnp.float32
BF16 = jnp.bfloat16
HD = 128
CH = 64
QB = 128
N_META = 16
PAD = 128
INERT = PAD - N_META
NDEV = 8
CONV_K = 4
EPS = 1e-6
VMEM_LIMIT = 56 * 1024 * 1024

ADAM_LR, ADAM_B1, ADAM_B2, ADAM_EPS, ADAM_WD, ADAM_STEP = 0.001, 0.9, 0.999, 1e-08, 0.01, 10
MESH = pl.DeviceIdType.MESH


def _cp(*sem):
    return pltpu.CompilerParams(dimension_semantics=sem, vmem_limit_bytes=VMEM_LIMIT)


def _tile(n, pref, mult=8):
    if n <= pref:
        return n
    for t in range(pref - pref % mult, 0, -mult):
        if n % t == 0:
            return t
    return n


def _silu(x):
    return x * jax.nn.sigmoid(x)


def _dsilu(x):
    s = jax.nn.sigmoid(x)
    return s * (1.0 + x * (1.0 - s))


def _dot(a, b, dims=((1,), (0,))):
    return lax.dot_general(a.astype(BF16), b.astype(BF16), (dims, ((), ())), preferred_element_type=F32)


def _dot_nt(a, b):
    return _dot(a, b, ((1,), (1,)))


def _dot_tn(a, b):
    return _dot(a, b, ((0,), (0,)))


def _dot_f32(a, b):
    dn = (((1,), (0,)), ((), ()))
    ah, bh = a.astype(BF16), b.astype(BF16)
    al, bl = (a - ah.astype(F32)).astype(BF16), (b - bh.astype(F32)).astype(BF16)
    mm = lambda x, y: lax.dot_general(x, y, dn, preferred_element_type=F32)
    return mm(ah, bh) + (mm(ah, bl) + mm(al, bh))


def _dot_split(a, m):
    hi = a.astype(BF16)
    lo = (a - hi.astype(F32)).astype(BF16)
    dn = (((1,), (0,)), ((), ()))
    return (lax.dot_general(hi, m, dn, preferred_element_type=F32)
            + lax.dot_general(lo, m, dn, preferred_element_type=F32))


def _iota(shape, dim):
    return lax.broadcasted_iota(jnp.int32, shape, dim)


def _col_to_row(col):
    n = col.shape[0]
    eye = _iota((n, n), 0) == _iota((n, n), 1)
    return jnp.sum(jnp.where(eye, col, 0.0), axis=0, keepdims=True)


def _row_to_col(row):
    n = row.shape[1]
    eye = _iota((n, n), 0) == _iota((n, n), 1)
    return jnp.sum(jnp.where(eye, row, 0.0), axis=1, keepdims=True)


def _exchange(arrs, gather, name):
    n = len(arrs)

    def body(*refs):
        ins, outs = refs[:n], refs[n:2 * n]
        send_sems, recv_sems, local_sems = refs[2 * n:]
        x, y, c = lax.axis_index("x"), lax.axis_index("y"), lax.axis_index("c")
        me = 4 * x + 2 * y + c
        sends = []
        for i in range(n):
            mine = pltpu.make_async_copy(ins[i] if gather else ins[i].at[me], outs[i].at[me], local_sems.at[i])
            mine.start()
            sends.append(mine)
        for k in range(1, NDEV):
            px, py, pc = x ^ (k >> 2), y ^ ((k >> 1) & 1), c ^ (k & 1)
            peer = 4 * px + 2 * py + pc
            for i in range(n):
                cp = pltpu.make_async_remote_copy(
                    src_ref=ins[i] if gather else ins[i].at[peer], dst_ref=outs[i].at[me],
                    send_sem=send_sems.at[i * NDEV + k], recv_sem=recv_sems.at[i * NDEV + k],
                    device_id=(px, py, pc), device_id_type=MESH)
                cp.start()
                sends.append(cp)
        for k in range(1, NDEV):
            px, py, pc = x ^ (k >> 2), y ^ ((k >> 1) & 1), c ^ (k & 1)
            peer = 4 * px + 2 * py + pc
            for i in range(n):
                pltpu.make_async_remote_copy(
                    src_ref=outs[i].at[peer], dst_ref=outs[i].at[peer],
                    send_sem=send_sems.at[i * NDEV + k], recv_sem=recv_sems.at[i * NDEV + k],
                    device_id=(px, py, pc), device_id_type=MESH).wait_recv()
        for i in range(n):
            sends[i].wait()
        for cp in sends[n:]:
            cp.wait_send()

    hbm = pl.BlockSpec(memory_space=pltpu.HBM)
    out_shape = tuple(jax.ShapeDtypeStruct(((NDEV,) + a.shape) if gather else a.shape, a.dtype) for a in arrs)
    return pl.pallas_call(
        body, name=name, out_shape=out_shape, in_specs=[hbm] * n, out_specs=tuple([hbm] * n),
        scratch_shapes=[pltpu.SemaphoreType.DMA((n * NDEV,)), pltpu.SemaphoreType.DMA((n * NDEV,)),
                        pltpu.SemaphoreType.DMA((n,))],
        compiler_params=pltpu.CompilerParams(has_side_effects=True),
    )(*arrs)


_HBM_SPEC = pl.BlockSpec(memory_space=pltpu.HBM)
_SEM_SPEC = pl.BlockSpec(memory_space=pltpu.SEMAPHORE)
_EFFECT = pltpu.SideEffectType.DATAFLOW_SIDE_EFFECTING


def _place():
    return lax.axis_index("x"), lax.axis_index("y"), lax.axis_index("c")


def _dev(px, py, pc):
    return 4 * px + 2 * py + pc


def _plan_direct(n, gather):
    def plan(ins, lnd, for_wait):
        x, y, c = _place()
        copies = []
        for k in range(1, NDEV):
            px, py, pc = x ^ (k >> 2), y ^ ((k >> 1) & 1), c ^ (k & 1)
            for i in range(n):
                slot = (_dev(px, py, pc) if for_wait else _dev(x, y, c)) if gather else k - 1
                copies.append((ins[i] if gather else ins[i].at[_dev(px, py, pc)], lnd[i].at[slot], (px, py, pc)))
        return copies
    return plan


def _plan_gather_chips(ins, lnd, for_wait):
    x, y, c = _place()
    copies = []
    for k in range(4):
        px, py, pc = (x, y, 1 - c) if k == 0 else (x ^ (k >> 1), y ^ (k & 1), c)
        copies.append((ins[0], lnd[0].at[_dev(px, py, pc) if for_wait else _dev(x, y, c)], (px, py, pc)))
    return copies


def _plan_gather_forward(ins, lnd, for_wait):
    x, y, c = _place()
    copies = []
    for k in range(1, 4):
        px, py = x ^ (k >> 1), y ^ (k & 1)
        copies.append((lnd[0].at[_dev(px, py, c)], lnd[0].at[_dev(px, py, 1 - c if for_wait else c)], (x, y, 1 - c)))
    return copies


def _plan_scatter_core(ins, lnd, for_wait):
    x, y, c = _place()
    return [(ins[0].at[_dev(x ^ (k >> 1), y ^ (k & 1), 1 - c)], lnd[0].at[k], (x, y, 1 - c)) for k in range(4)]


def _plan_scatter_chips(ins, lnd, for_wait):
    x, y, c = _place()
    return [(ins[0].at[k], lnd[0].at[k - 1], (x ^ (k >> 1), y ^ (k & 1), c)) for k in range(1, 4)]


def _plan_descriptors(plan, ins, lnd, send_sems, recv_sems, for_wait):
    return [pltpu.make_async_remote_copy(src_ref=src, dst_ref=dst, send_sem=send_sems.at[j], recv_sem=recv_sems.at[j],
                                         device_id=dev, device_id_type=MESH)
            for j, (src, dst, dev) in enumerate(plan(ins, lnd, for_wait))]


def _split_start(srcs, lands, plan, ncopies, name, after=None):
    ns, nl = len(srcs), len(lands)
    extra = [] if after is None else [after]

    def body(*refs):
        ins, lnd = refs[:ns], refs[ns:ns + nl]
        send_sems, recv_sems = refs[ns + nl + len(extra)], refs[ns + nl + len(extra) + 1]
        token = refs[-1]
        for cp in _plan_descriptors(plan, ins, lnd, send_sems, recv_sems, False):
            cp.start()
        token[...] = jnp.zeros_like(token)

    sems = pltpu.SemaphoreType.DMA((ncopies,))
    both = list(srcs) + list(lands)
    outs = pl.pallas_call(
        body, name=name,
        out_shape=(sems, sems, *[pltpu.HBM(a.shape, a.dtype) for a in both], jax.ShapeDtypeStruct((8, 128), F32)),
        in_specs=[_HBM_SPEC] * (ns + nl) + [pl.BlockSpec(memory_space=pl.ANY)] * len(extra),
        out_specs=(_SEM_SPEC, _SEM_SPEC, *[_HBM_SPEC] * (ns + nl), pl.BlockSpec(memory_space=pltpu.VMEM)),
        input_output_aliases={i: 2 + i for i in range(ns + nl)},
        compiler_params=pltpu.CompilerParams(has_side_effects=_EFFECT),
    )(*[pltpu.with_memory_space_constraint(a, pltpu.HBM) for a in both], *extra)
    return (outs[0], outs[1], list(outs[2:2 + ns]), list(outs[2 + ns:2 + ns + nl]), plan), outs[-1]


def _split_wait(state, after, name):
    send_sems, recv_sems, srcs, lands, plan = state
    ns, nl = len(srcs), len(lands)

    def body(*refs):
        ins, lnd = refs[:ns], refs[ns:ns + nl]
        for cp in _plan_descriptors(plan, ins, lnd, refs[ns + nl], refs[ns + nl + 1], True):
            cp.wait_send()
            cp.wait_recv()

    outs = pl.pallas_call(
        body, name=name,
        out_shape=tuple(pltpu.HBM(a.shape, a.dtype) for a in srcs + lands),
        in_specs=[_HBM_SPEC] * (ns + nl) + [_SEM_SPEC, _SEM_SPEC, pl.BlockSpec(memory_space=pl.ANY)],
        out_specs=tuple([_HBM_SPEC] * (ns + nl)), input_output_aliases={i: i for i in range(ns + nl)},
        compiler_params=pltpu.CompilerParams(has_side_effects=_EFFECT),
    )(*srcs, *lands, send_sems, recv_sems, after)
    return list(outs[ns:]), list(outs[:ns])


def _exchange_start(arrs, gather, name, after=None):
    lands = [lax.empty(((NDEV,) + a.shape) if gather else ((NDEV - 1,) + a.shape[1:]), a.dtype) for a in arrs]
    return _split_start(arrs, lands, _plan_direct(len(arrs), gather), len(arrs) * (NDEV - 1), name, after)


_exchange_wait = _split_wait


def _chip_sums(parts, zone, name):
    _, r, c = parts.shape
    tr = _tile(r, max(16, (1 << 19) // c // 16 * 16), 16)
    x, y, core = _place()
    mine = jnp.stack([lax.dynamic_index_in_dim(parts, _dev(x ^ (k >> 1), y ^ (k & 1), core), 0, keepdims=False)
                      for k in range(4)])

    def body(p_ref, z_ref, o_ref):
        o_ref[...] = (p_ref[...].astype(F32) + z_ref[...].astype(F32)).astype(o_ref.dtype)

    blk = pl.BlockSpec((1, tr, c), lambda k, i: (k, i, 0))
    return pl.pallas_call(
        body, name=name, grid=(4, r // tr), in_specs=[blk, blk], out_specs=blk,
        out_shape=jax.ShapeDtypeStruct((4, r, c), parts.dtype),
        compiler_params=_cp("parallel", "parallel"))(mine, zone)


def _own_index():
    return 4 * lax.axis_index("x") + 2 * lax.axis_index("y") + lax.axis_index("c")


def _with_own(land, mine):
    return lax.dynamic_update_index_in_dim(land, mine, _own_index(), 0)


def _matmul(a, b, *, mode, out_dtype, tm, tn, tk, name, add=None, after=None, b_cols=None):
    if mode == "nn":
        (m, kd), (_, n) = a.shape, b.shape
    elif mode == "nt":
        (m, kd), (n, _) = a.shape, b.shape
    else:
        (kd, m), (_, n) = a.shape, b.shape
    if b_cols is not None and mode == "nn":
        n = b_cols[1]
    tm, tn, tk = _tile(m, tm, 16), _tile(n, tn, 128), _tile(kd, tk, 128)
    nk = kd // tk
    jb = kb = 0
    if b_cols is not None:
        assert mode in ("nn", "nt") and b_cols[0] % (tn if mode == "nn" else tk) == 0 and (mode == "nn" or b_cols[1] == kd)
        jb, kb = (b_cols[0] // tn, 0) if mode == "nn" else (0, b_cols[0] // tk)
    a_spec = pl.BlockSpec((tk, tm), lambda i, j, k: (k, i)) if mode == "tn" else pl.BlockSpec((tm, tk), lambda i, j, k: (i, k))
    b_spec = (pl.BlockSpec((tn, tk), lambda i, j, k: (j, kb + k)) if mode == "nt"
              else pl.BlockSpec((tk, tn), lambda i, j, k: (k, jb + j)))
    o_spec = pl.BlockSpec((tm, tn), lambda i, j, k: (i, j))
    dims = {"nn": ((1,), (0,)), "nt": ((1,), (1,)), "tn": ((0,), (0,))}[mode]

    def body(*refs, nk):
        a_ref, b_ref = refs[0], refs[1]
        o_ref, acc_ref = refs[-2], refs[-1]
        k = pl.program_id(2)

        @pl.when(k == 0)
        def _():
            acc_ref[...] = jnp.zeros_like(acc_ref)

        acc_ref[...] += lax.dot_general(a_ref[...], b_ref[...], (dims, ((), ())), preferred_element_type=F32)

        @pl.when(k == nk - 1)
        def _():
            r = acc_ref[...]
            if add is not None:
                r = r + refs[2][...]
            o_ref[...] = r.astype(o_ref.dtype)

    ins, specs = [a, b], [a_spec, b_spec]
    if add is not None:
        ins.append(add)
        specs.append(o_spec)
    if after is not None:
        ins.append(after)
        specs.append(pl.BlockSpec(after.shape, lambda i, j, k: (0, 0)))
    return pl.pallas_call(
        functools.partial(body, nk=nk), name=name, grid=(m // tm, n // tn, nk),
        in_specs=specs, out_specs=o_spec, out_shape=jax.ShapeDtypeStruct((m, n), out_dtype),
        scratch_shapes=[pltpu.VMEM((tm, tn), F32)], compiler_params=_cp("parallel", "parallel", "arbitrary"),
    )(*ins)


def _rms_fwd(h, w, name):
    lp, d = h.shape
    tm = _tile(lp, 384)

    def body(h_ref, w_ref, o_ref):
        xf = h_ref[...]
        r = lax.rsqrt(jnp.mean(xf * xf, axis=-1, keepdims=True) + EPS)
        o_ref[...] = (xf * r * w_ref[...]).astype(o_ref.dtype)

    return pl.pallas_call(
        body, name=name, grid=(lp // tm,),
        in_specs=[pl.BlockSpec((tm, d), lambda i: (i, 0)), pl.BlockSpec((1, d), lambda i: (0, 0))],
        out_specs=pl.BlockSpec((tm, d), lambda i: (i, 0)), out_shape=jax.ShapeDtypeStruct((lp, d), BF16),
        compiler_params=_cp("parallel"))(h, w)


def _rms_bwd(h, w, dhn, dres, name):
    lp, d = h.shape
    tm = _tile(lp, 192)

    def body(h_ref, w_ref, dy_ref, dres_ref, dh_ref, dhb_ref, dw_ref):
        xf = h_ref[...]
        r = lax.rsqrt(jnp.mean(xf * xf, axis=-1, keepdims=True) + EPS)
        xhat = xf * r
        dy = dy_ref[...]
        dxhat = dy * w_ref[...]
        dx = r * (dxhat - xhat * jnp.mean(dxhat * xhat, axis=-1, keepdims=True))
        dh = dres_ref[...] + dx
        dh_ref[...] = dh
        dhb_ref[...] = dh.astype(BF16)

        @pl.when(pl.program_id(0) == 0)
        def _():
            dw_ref[...] = jnp.zeros_like(dw_ref)

        dw_ref[...] += jnp.sum(dy * xhat, axis=0, keepdims=True)

    row = pl.BlockSpec((tm, d), lambda i: (i, 0))
    vec = pl.BlockSpec((1, d), lambda i: (0, 0))
    return pl.pallas_call(
        body, name=name, grid=(lp // tm,), in_specs=[row, vec, row, row], out_specs=(row, row, vec),
        out_shape=(jax.ShapeDtypeStruct((lp, d), F32), jax.ShapeDtypeStruct((lp, d), BF16),
                   jax.ShapeDtypeStruct((1, d), F32)),
        compiler_params=_cp("arbitrary"))(h, w, dhn, dres)


def _conv_pre(xx, w, rows, off):
    acc = None
    for j in range(CONV_K):
        sh = CONV_K - 1 - j
        term = (pltpu.roll(xx, sh, 0) if sh else xx)[off:off + rows] * w[j]
        acc = term if acc is None else acc + term
    return acc


def _conv_fwd(proj, conv_w, ncols, name):
    lp = proj.shape[0]
    tm, tc = _tile(lp, 384), _tile(ncols, 1024, 128)
    hb = tm // 8

    def body(x_ref, xb_ref, w_ref, o_ref):
        before = jnp.where(pl.program_id(0) > 0, xb_ref[...], 0.0)
        xx = jnp.concatenate([before, x_ref[...]], axis=0)
        o_ref[...] = _silu(_conv_pre(xx, [w_ref[j:j + 1, :] for j in range(CONV_K)], tm, 8))

    return pl.pallas_call(
        body, name=name, grid=(lp // tm, ncols // tc),
        in_specs=[pl.BlockSpec((tm, tc), lambda i, j: (i, j)),
                  pl.BlockSpec((8, tc), lambda i, j: (jnp.maximum(i * hb - 1, 0), j)),
                  pl.BlockSpec((CONV_K, tc), lambda i, j: (0, j))],
        out_specs=pl.BlockSpec((tm, tc), lambda i, j: (i, j)),
        out_shape=jax.ShapeDtypeStruct((lp, ncols), F32), compiler_params=_cp("parallel", "parallel"))(proj, proj, conv_w)


def _conv_bwd(proj, col0, conv_w, dact, name):
    lp, ncols = dact.shape
    tm, tc = _tile(lp, 384), _tile(ncols, 512, 128)
    hb, nt, cb0 = tm // 8, lp // tm, col0 // tc
    assert col0 % tc == 0

    def body(x_ref, xb_ref, xa_ref, d_ref, da_ref, w_ref, dx_ref, dw_ref):
        i = pl.program_id(1)
        before = jnp.where(i > 0, xb_ref[...], 0.0)
        last = i == nt - 1
        xx = jnp.concatenate([before, x_ref[...], jnp.where(last, 0.0, xa_ref[...])], axis=0)
        w = [w_ref[j:j + 1, :] for j in range(CONV_K)]
        pre = _conv_pre(xx, w, tm + 8, 8)
        dd = jnp.concatenate([d_ref[...], jnp.where(last, 0.0, da_ref[...])], axis=0)
        dpre = dd * _dsilu(pre)
        dx = None
        for j in range(CONV_K):
            sh = CONV_K - 1 - j
            term = (pltpu.roll(dpre, tm + 8 - sh, 0) if sh else dpre)[:tm] * w[j]
            dx = term if dx is None else dx + term
        dx_ref[...] = dx.astype(BF16)

        @pl.when(i == 0)
        def _():
            dw_ref[...] = jnp.zeros_like(dw_ref)

        for j in range(CONV_K):
            sh = CONV_K - 1 - j
            xs = (pltpu.roll(xx, sh, 0) if sh else xx)[8:8 + tm]
            dw_ref[j:j + 1, :] += jnp.sum(dpre[:tm] * xs, axis=0, keepdims=True)

    return pl.pallas_call(
        body, name=name, grid=(ncols // tc, nt),
        in_specs=[pl.BlockSpec((tm, tc), lambda j, i: (i, cb0 + j)),
                  pl.BlockSpec((8, tc), lambda j, i: (jnp.maximum(i * hb - 1, 0), cb0 + j)),
                  pl.BlockSpec((8, tc), lambda j, i: (jnp.minimum((i + 1) * hb, nt * hb - 1), cb0 + j)),
                  pl.BlockSpec((tm, tc), lambda j, i: (i, j)),
                  pl.BlockSpec((8, tc), lambda j, i: (jnp.minimum((i + 1) * hb, nt * hb - 1), j)),
                  pl.BlockSpec((CONV_K, tc), lambda j, i: (0, j))],
        out_specs=(pl.BlockSpec((tm, tc), lambda j, i: (i, j)), pl.BlockSpec((CONV_K, tc), lambda j, i: (0, j))),
        out_shape=(jax.ShapeDtypeStruct((lp, ncols), BF16), jax.ShapeDtypeStruct((CONV_K, ncols), F32)),
        compiler_params=_cp("parallel", "arbitrary"))(proj, proj, proj, dact, dact, conv_w)


def _softplus(x):
    return jnp.maximum(x, 0.0) + jnp.log(1.0 + jnp.exp(-jnp.abs(x)))


def _gates_fwd(gl, a_log2, dt_bias2, name):
    lp, w2 = gl.shape
    hv = w2 // 2

    def body(gl_ref, al_ref, dt_ref, o_ref):
        x = gl_ref[...]
        live = _iota((lp, 1), 0) >= INERT
        is_beta = _iota((1, w2), 1) < hv
        g = -jnp.exp(al_ref[...]) * _softplus(x + dt_ref[...])
        o_ref[...] = jnp.where(live, jnp.where(is_beta, jax.nn.sigmoid(x), g), 0.0)

    return pl.pallas_call(body, name=name, out_shape=jax.ShapeDtypeStruct((lp, w2), F32))(gl, a_log2, dt_bias2)


def _gates_bwd(gl, a_log2, dt_bias2, dbg, name):
    lp, w2 = gl.shape
    hv = w2 // 2

    def body(gl_ref, al_ref, dt_ref, d_ref, dl_ref, dal_ref, ddt_ref):
        x = gl_ref[...]
        live = _iota((lp, 1), 0) >= INERT
        is_beta = _iota((1, w2), 1) < hv
        d = jnp.where(live, d_ref[...], 0.0)
        beta = jax.nn.sigmoid(x)
        ea = jnp.exp(al_ref[...])
        u = x + dt_ref[...]
        dg = jnp.where(is_beta, 0.0, d)
        dal_ref[...] = jnp.sum(dg * (-ea) * _softplus(u), axis=0, keepdims=True)
        du = dg * (-ea) * jax.nn.sigmoid(u)
        ddt_ref[...] = jnp.sum(du, axis=0, keepdims=True)
        dl_ref[...] = jnp.where(is_beta, d * beta * (1.0 - beta), du).astype(BF16)

    vec = jax.ShapeDtypeStruct((1, w2), F32)
    return pl.pallas_call(
        body, name=name, out_shape=(jax.ShapeDtypeStruct((lp, w2), BF16), vec, vec))(gl, a_log2, dt_bias2, dbg)


def _l2n(x):
    r = lax.rsqrt(jnp.sum(x * x, axis=-1, keepdims=True) + EPS)
    return x * r, r


def _tri_inverse(mats):
    eye = (_iota((CH, CH), 0) == _iota((CH, CH), 1)).astype(F32)
    ts = [eye - a for a in mats]
    ps = [_dot_f32(a, a) for a in mats]
    n = 2
    while n < CH:
        ts = [t + _dot_f32(t, p) for t, p in zip(ts, ps)]
        n *= 2
        if n < CH:
            ps = [_dot_f32(p, p) for p in ps]
    return ts


def _chunk_local(qn, kn, v, b_row, g_row):
    ri, ci = _iota((CH, CH), 0), _iota((CH, CH), 1)
    incl, strict = ri >= ci, ri > ci
    gam_col = jnp.sum(jnp.where(incl, g_row, 0.0), axis=1, keepdims=True)
    gam_row = _col_to_row(gam_col)
    b_col = _row_to_col(b_row)
    dec = jnp.exp(jnp.where(incl, gam_col - gam_row, -jnp.inf))
    eg = jnp.exp(gam_col)
    gl = jnp.sum(g_row, axis=1, keepdims=True)
    ekd = jnp.exp(gl - gam_col)
    kb = kn * b_col
    a = jnp.where(strict, _dot_nt(kb, kn) * dec, 0.0)
    p = jnp.where(incl, _dot_nt(qn, kn) * dec, 0.0)
    return dict(dec=dec, eg=eg, ekd=ekd, kb=kb, vb=v * b_col, a=a, kbg=kb * eg, p=p, qd=qn * eg, kd=kn * ekd,
                b_col=b_col, incl=incl, strict=strict)


def _chunks_per_step(nc):
    return max(g for g in (1, 2, 3, 6, 11) if nc % g == 0)


def _heads_per_step(hv):
    return min(hv, 32)


def _delta_local(act, gates, key_w, name):
    lp = act.shape[0]
    hk, nc = key_w // HD, lp // CH
    hv = 2 * hk
    g = _chunks_per_step(nc)
    tr = g * CH

    def body(q_ref, k_ref, v_ref, g_ref, u_ref, w_ref, qd_ref, kd_ref, p_ref, t_ref):
        items = []
        for j in range(g):
            rows = slice(j * CH, (j + 1) * CH)
            qn = _l2n(q_ref[rows, :])[0] * (HD ** -0.5)
            kn = _l2n(k_ref[rows, :])[0]
            for e in range(2):
                cols = slice(e * HD, (e + 1) * HD)
                r = _chunk_local(qn, kn, v_ref[rows, cols], g_ref[0, j, e:e + 1, :], g_ref[0, j, 2 + e:3 + e, :])
                qd_ref[rows, cols] = r["qd"].astype(BF16)
                kd_ref[rows, cols] = r["kd"].astype(BF16)
                p_ref[e, rows, :] = r["p"].astype(BF16)
                items.append((rows, cols, e, r["a"], r["vb"].astype(BF16), r["kbg"].astype(BF16)))
        ts = [t.astype(BF16) for t in _tri_inverse([it[3] for it in items])]
        us = [_dot(t, it[4]) for t, it in zip(ts, items)]
        ws = [_dot(t, it[5]) for t, it in zip(ts, items)]
        for (rows, cols, e, _, _, _), t, u, w in zip(items, ts, us, ws):
            u_ref[rows, cols] = u
            w_ref[rows, cols] = w.astype(BF16)
            t_ref[e, rows, :] = t

    wide = pl.BlockSpec((tr, 2 * HD), lambda h, c: (c, h))
    sq = pl.BlockSpec((2, tr, CH), lambda h, c: (h, c, 0))
    wshape = lambda dt: jax.ShapeDtypeStruct((lp, hv * HD), dt)
    sshape = jax.ShapeDtypeStruct((hv, lp, CH), BF16)
    return pl.pallas_call(
        body, name=name, grid=(hk, nc // g),
        in_specs=[pl.BlockSpec((tr, HD), lambda h, c: (c, h)),
                  pl.BlockSpec((tr, HD), lambda h, c: (c, hk + h)),
                  pl.BlockSpec((tr, 2 * HD), lambda h, c: (c, hk + h)),
                  pl.BlockSpec((1, g, 8, CH), lambda h, c: (h, c, 0, 0))],
        out_specs=(wide, wide, wide, wide, sq, sq),
        out_shape=(wshape(F32), wshape(BF16), wshape(BF16), wshape(BF16), sshape, sshape),
        compiler_params=_cp("parallel", "parallel"))(act, act, act, gates)


def _chunk_decay(g_ref, e):
    return jnp.exp(jnp.sum(g_ref[e // 2, 0, 2 + e % 2:3 + e % 2, :], axis=1, keepdims=True))


def _delta_scan(u, w, qd, kd, p, gates, name):
    lp, val = u.shape
    hv, nc = val // HD, lp // CH
    nh = _heads_per_step(hv)

    def body(u_ref, w_ref, qd_ref, kd_ref, p_ref, g_ref, o_ref, vn_ref, st_ref, s_scr):
        @pl.when(pl.program_id(1) == 0)
        def _():
            s_scr[...] = jnp.zeros_like(s_scr)

        heads = range(nh)
        col = lambda e: slice(e * HD, (e + 1) * HD)
        ss = [s_scr[e] for e in heads]
        sb = [s.astype(BF16) for s in ss]
        for e in heads:
            st_ref[0, e] = ss[e]
        ws = [_dot(w_ref[:, col(e)], sb[e]) for e in heads]
        qs = [_dot(qd_ref[:, col(e)], sb[e]) for e in heads]
        vns = [(u_ref[:, col(e)] - ws[e]).astype(BF16) for e in heads]
        pv = [_dot(p_ref[e], vns[e]) for e in heads]
        kv = [_dot_tn(kd_ref[:, col(e)], vns[e]) for e in heads]
        for e in heads:
            o_ref[:, col(e)] = qs[e] + pv[e]
            s_scr[e] = _chunk_decay(g_ref, e) * ss[e] + kv[e]
            vn_ref[:, col(e)] = vns[e]

    wide = pl.BlockSpec((CH, nh * HD), lambda h, c: (c, h))
    return pl.pallas_call(
        body, name=name, grid=(hv // nh, nc),
        in_specs=[wide, wide, wide, wide, pl.BlockSpec((nh, CH, CH), lambda h, c: (h, c, 0)),
                  pl.BlockSpec((nh // 2, 1, 8, CH), lambda h, c: (h, c, 0, 0))],
        out_specs=(wide, wide, pl.BlockSpec((1, nh, HD, HD), lambda h, c: (c, h, 0, 0))),
        out_shape=(jax.ShapeDtypeStruct((lp, val), F32), jax.ShapeDtypeStruct((lp, val), BF16),
                   jax.ShapeDtypeStruct((nc, hv, HD, HD), F32)),
        scratch_shapes=[pltpu.VMEM((nh, HD, HD), F32)],
        compiler_params=_cp("parallel", "arbitrary"))(u, w, qd, kd, p, gates)


def _delta_scan_bwd(do, w, qd, kd, p, vn, states, gates, name):
    lp, val = do.shape
    hv, nc = val // HD, lp // CH
    nh = _heads_per_step(hv)

    def body(do_ref, w_ref, qd_ref, kd_ref, p_ref, vn_ref, st_ref, g_ref,
             dvn_ref, dw_ref, dqd_ref, dkd_ref, dp_ref, sd_ref, ds_scr):
        @pl.when(pl.program_id(1) == 0)
        def _():
            ds_scr[...] = jnp.zeros_like(ds_scr)

        incl = _iota((CH, CH), 0) >= _iota((CH, CH), 1)
        heads = range(nh)
        col = lambda e: slice(e * HD, (e + 1) * HD)
        ss = [st_ref[0, e] for e in heads]
        dss = [ds_scr[e] for e in heads]
        sb = [s.astype(BF16) for s in ss]
        dsb = [d.astype(BF16) for d in dss]
        dos = [do_ref[:, col(e)].astype(BF16) for e in heads]
        egl = [_chunk_decay(g_ref, e) for e in heads]
        pdo = [_dot_tn(p_ref[e], dos[e]) for e in heads]
        kds = [_dot(kd_ref[:, col(e)], dsb[e]) for e in heads]
        qdo = [_dot_tn(qd_ref[:, col(e)], dos[e]) for e in heads]
        dqd = [_dot_nt(dos[e], sb[e]) for e in heads]
        dkd = [_dot_nt(vn_ref[:, col(e)], dsb[e]) for e in heads]
        dpp = [_dot_nt(dos[e], vn_ref[:, col(e)]) for e in heads]
        dvn = [(pdo[e] + kds[e]).astype(BF16) for e in heads]
        wdv = [_dot_tn(w_ref[:, col(e)], dvn[e]) for e in heads]
        dws = [_dot_nt(dvn[e], sb[e]) for e in heads]
        for e in heads:
            ds_scr[e] = qdo[e] + egl[e] * dss[e] - wdv[e]
            dvn_ref[:, col(e)] = dvn[e]
            dw_ref[:, col(e)] = (-dws[e]).astype(BF16)
            dqd_ref[:, col(e)] = dqd[e]
            dkd_ref[:, col(e)] = dkd[e]
            dp_ref[e] = jnp.where(incl, dpp[e], 0.0)
            sd_ref[0, 0, e:e + 1, :] = jnp.broadcast_to(egl[e] * jnp.sum(ss[e] * dss[e], keepdims=True), (1, HD))

    rev = lambda c: nc - 1 - c
    wide = pl.BlockSpec((CH, nh * HD), lambda h, c: (rev(c), h))
    sq = pl.BlockSpec((nh, CH, CH), lambda h, c: (h, rev(c), 0))
    wshape = lambda dt: jax.ShapeDtypeStruct((lp, val), dt)
    return pl.pallas_call(
        body, name=name, grid=(hv // nh, nc),
        in_specs=[wide, wide, wide, wide, sq, wide, pl.BlockSpec((1, nh, HD, HD), lambda h, c: (rev(c), h, 0, 0)),
                  pl.BlockSpec((nh // 2, 1, 8, CH), lambda h, c: (h, rev(c), 0, 0))],
        out_specs=(wide, wide, wide, wide, sq, pl.BlockSpec((1, 1, nh, HD), lambda h, c: (h, rev(c), 0, 0))),
        out_shape=(wshape(BF16), wshape(BF16), wshape(F32), wshape(F32), jax.ShapeDtypeStruct((hv, lp, CH), F32),
                   jax.ShapeDtypeStruct((hv // nh, nc, nh, HD), F32)),
        scratch_shapes=[pltpu.VMEM((nh, HD, HD), F32)],
        compiler_params=_cp("parallel", "arbitrary"))(do, w, qd, kd, p, vn, states, gates)


def _delta_local_bwd(act, gates, t, dvn, dw, dqd, dkd, dp, key_w, name):
    lp = act.shape[0]
    hk, nc = key_w // HD, lp // CH
    hv = 2 * hk
    g = _chunks_per_step(nc)
    tr = g * CH
    scale = HD ** -0.5

    def body(q_ref, k_ref, v_ref, g_ref, t_ref, dvn_ref, dw_ref, dqd_ref, dkd_ref, dp_ref, dq_ref, dk_ref, dv_ref, dg_ref):
        ri, ci = _iota((CH, CH), 0), _iota((CH, CH), 1)
        inner = lambda x, z: jnp.sum(x * z, axis=1, keepdims=True)
        norms, items = [], []
        for j in range(g):
            rows = slice(j * CH, (j + 1) * CH)
            qh, qr = _l2n(q_ref[rows, :])
            kn, kr = _l2n(k_ref[rows, :])
            qn = qh * scale
            norms.append((rows, qh, qr, kn, kr, qn))
            dg_ref[0, j, 4:8, :] = jnp.zeros((4, CH), F32)
            for e in range(2):
                cols = slice(e * HD, (e + 1) * HD)
                v = v_ref[rows, cols]
                r = _chunk_local(qn, kn, v, g_ref[0, j, e:e + 1, :], g_ref[0, j, 2 + e:3 + e, :])
                items.append(dict(r, j=j, e=e, rows=rows, cols=cols, v=v, kn=kn, qn=qn, t=t_ref[e, rows, :],
                                  dvn=dvn_ref[rows, cols], dw=dw_ref[rows, cols]))
        for it in items:
            it["dt"] = _dot_nt(it["dvn"], it["vb"]) + _dot_nt(it["dw"], it["kbg"])
            it["dvb"] = _dot_tn(it["t"], it["dvn"])
            it["dkbg"] = _dot_tn(it["t"], it["dw"])
        for it in items:
            it["x"] = _dot_tn(it["t"], it["dt"])
        for it in items:
            it["da"] = -jnp.where(it["strict"], _dot_nt(it["x"], it["t"]), 0.0)
        for it in items:
            dp = dp_ref[it["e"], it["rows"], :]
            it["gmat"] = it["da"] * it["a"] + dp * it["p"]
            mm, nn = (it["da"] * it["dec"]).astype(BF16), (dp * it["dec"]).astype(BF16)
            it["dkb"] = _dot(mm, it["kn"]) + it["dkbg"] * it["eg"]
            it["dkn"] = _dot_tn(mm, it["kb"]) + _dot_tn(nn, it["qn"])
            it["dqn"] = _dot(nn, it["kn"])
        for it in items:
            j, e, rows, cols = it["j"], it["e"], it["rows"], it["cols"]
            dqd, dkd, gmat, dkb = dqd_ref[rows, cols], dkd_ref[rows, cols], it["gmat"], it["dkb"]
            it["dkn"] = it["dkn"] + dkd * it["ekd"] + it["b_col"] * dkb
            it["dqn"] = it["dqn"] + dqd * it["eg"]
            dkd_kd = inner(dkd, it["kd"])
            dgam = (jnp.sum(gmat, axis=1, keepdims=True) - _row_to_col(jnp.sum(gmat, axis=0, keepdims=True))
                    + inner(dqd, it["qd"]) + inner(it["dkbg"], it["kbg"]) - dkd_kd)
            dgl = jnp.max(g_ref[0, j, 4 + e:5 + e, :], axis=1, keepdims=True) + jnp.sum(dkd_kd, keepdims=True)
            dgam = dgam + jnp.where(_iota((CH, 1), 0) == CH - 1, dgl, 0.0)
            dg_ref[0, j, 2 + e:3 + e, :] = jnp.sum(jnp.where(ri >= ci, dgam, 0.0), axis=0, keepdims=True)
            dg_ref[0, j, e:e + 1, :] = _col_to_row(inner(dkb, it["kn"]) + inner(it["dvb"], it["v"]))
            dv_ref[rows, cols] = it["b_col"] * it["dvb"]
        for j, (rows, qh, qr, kn, kr, _) in enumerate(norms):
            dqh = (items[2 * j]["dqn"] + items[2 * j + 1]["dqn"]) * scale
            dkn = items[2 * j]["dkn"] + items[2 * j + 1]["dkn"]
            dq_ref[rows, :] = qr * (dqh - qh * jnp.sum(dqh * qh, axis=1, keepdims=True))
            dk_ref[rows, :] = kr * (dkn - kn * jnp.sum(dkn * kn, axis=1, keepdims=True))

    narrow = pl.BlockSpec((tr, HD), lambda h, c: (c, h))
    wide = pl.BlockSpec((tr, 2 * HD), lambda h, c: (c, h))
    sq = pl.BlockSpec((2, tr, CH), lambda h, c: (h, c, 0))
    gate = pl.BlockSpec((1, g, 8, CH), lambda h, c: (h, c, 0, 0))
    return pl.pallas_call(
        body, name=name, grid=(hk, nc // g),
        in_specs=[narrow, pl.BlockSpec((tr, HD), lambda h, c: (c, hk + h)),
                  pl.BlockSpec((tr, 2 * HD), lambda h, c: (c, hk + h)), gate, sq, wide, wide, wide, wide, sq],
        out_specs=(narrow, narrow, wide, gate),
        out_shape=(jax.ShapeDtypeStruct((lp, key_w), F32), jax.ShapeDtypeStruct((lp, key_w), F32),
                   jax.ShapeDtypeStruct((lp, hv * HD), F32), jax.ShapeDtypeStruct((hk, nc, 8, CH), F32)),
        compiler_params=_cp("parallel", "parallel"))(act, act, act, gates, t, dvn, dw, dqd, dkd, dp)


def _head_group(nheads):
    return 4 if nheads % 4 == 0 else 1


def _outnorm_fwd(o, proj, z_col0, w, name):
    lp, val = o.shape
    hg = _head_group(val // HD)
    bw = hg * HD
    tm, zb = _tile(lp, 1056), z_col0 // bw

    def body(o_ref, z_ref, w_ref, y_ref):
        for j in range(hg):
            cols = slice(j * HD, (j + 1) * HD)
            xf = o_ref[:, cols]
            r = lax.rsqrt(jnp.mean(xf * xf, axis=-1, keepdims=True) + EPS)
            y_ref[:, cols] = (xf * r * w_ref[...] * _silu(z_ref[:, cols])).astype(BF16)

    return pl.pallas_call(
        body, name=name, grid=(lp // tm, val // bw),
        in_specs=[pl.BlockSpec((tm, bw), lambda i, h: (i, h)), pl.BlockSpec((tm, bw), lambda i, h: (i, zb + h)),
                  pl.BlockSpec((1, HD), lambda i, h: (0, 0))],
        out_specs=pl.BlockSpec((tm, bw), lambda i, h: (i, h)), out_shape=jax.ShapeDtypeStruct((lp, val), BF16),
        compiler_params=_cp("parallel", "parallel"))(o, proj, w)


def _outnorm_bwd(o, proj, z_col0, w, dy, name):
    lp, val = o.shape
    hg = _head_group(val // HD)
    bw = hg * HD
    tm, zb = _tile(lp, 1056), z_col0 // bw

    def body(o_ref, z_ref, w_ref, dy_ref, do_ref, dz_ref, dw_ref):
        @pl.when((pl.program_id(0) == 0) & (pl.program_id(1) == 0))
        def _():
            dw_ref[...] = jnp.zeros_like(dw_ref)

        for j in range(hg):
            cols = slice(j * HD, (j + 1) * HD)
            xf, z, d = o_ref[:, cols], z_ref[:, cols], dy_ref[:, cols]
            r = lax.rsqrt(jnp.mean(xf * xf, axis=-1, keepdims=True) + EPS)
            xhat = xf * r
            dn = d * _silu(z)
            dz_ref[:, cols] = (d * xhat * w_ref[...] * _dsilu(z)).astype(BF16)
            dxhat = dn * w_ref[...]
            do_ref[:, cols] = r * (dxhat - xhat * jnp.mean(dxhat * xhat, axis=-1, keepdims=True))
            dw_ref[...] += jnp.sum(dn * xhat, axis=0, keepdims=True)

    blk = pl.BlockSpec((tm, bw), lambda i, h: (i, h))
    vec = pl.BlockSpec((1, HD), lambda i, h: (0, 0))
    return pl.pallas_call(
        body, name=name, grid=(lp // tm, val // bw),
        in_specs=[blk, pl.BlockSpec((tm, bw), lambda i, h: (i, zb + h)), vec, blk], out_specs=(blk, blk, vec),
        out_shape=(jax.ShapeDtypeStruct((lp, val), F32), jax.ShapeDtypeStruct((lp, val), BF16),
                   jax.ShapeDtypeStruct((1, HD), F32)),
        compiler_params=_cp("arbitrary", "arbitrary"))(o, proj, w, dy)


def _qknorm_fwd(proj, qw, kw, width, name):
    lp = proj.shape[0]
    hg = _head_group(width // HD)
    bw = hg * HD
    tm, nh = _tile(lp, 1056), width // bw

    def body(q_ref, k_ref, v_ref, qw_ref, kw_ref, qo_ref, ko_ref, vo_ref):
        for x_ref, w_ref, o_ref in ((q_ref, qw_ref, qo_ref), (k_ref, kw_ref, ko_ref)):
            for j in range(hg):
                cols = slice(j * HD, (j + 1) * HD)
                xf = x_ref[:, cols]
                r = lax.rsqrt(jnp.mean(xf * xf, axis=-1, keepdims=True) + EPS)
                o_ref[:, cols] = (xf * r * w_ref[...]).astype(BF16)
        vo_ref[...] = v_ref[...].astype(BF16)

    blk = lambda off: pl.BlockSpec((tm, bw), lambda i, h: (i, off + h))
    vec = pl.BlockSpec((1, HD), lambda i, h: (0, 0))
    shp = jax.ShapeDtypeStruct((lp, width), BF16)
    return pl.pallas_call(
        body, name=name, grid=(lp // tm, nh), in_specs=[blk(0), blk(nh), blk(2 * nh), vec, vec],
        out_specs=(blk(0), blk(0), blk(0)), out_shape=(shp, shp, shp),
        compiler_params=_cp("parallel", "parallel"))(proj, proj, proj, qw, kw)


def _qknorm_bwd(proj, qw, kw, dqn, dkn, width, name):
    lp = proj.shape[0]
    hg = _head_group(width // HD)
    bw = hg * HD
    tm, nh = _tile(lp, 1056), width // bw

    def body(q_ref, k_ref, qw_ref, kw_ref, dqn_ref, dkn_ref, dq_ref, dk_ref, dqw_ref, dkw_ref):
        first = (pl.program_id(0) == 0) & (pl.program_id(1) == 0)
        for x_ref, w_ref, dy_ref, dx_ref, dw_ref in ((q_ref, qw_ref, dqn_ref, dq_ref, dqw_ref),
                                                       (k_ref, kw_ref, dkn_ref, dk_ref, dkw_ref)):
            @pl.when(first)
            def _():
                dw_ref[...] = jnp.zeros_like(dw_ref)

            for j in range(hg):
                cols = slice(j * HD, (j + 1) * HD)
                xf, dy = x_ref[:, cols], dy_ref[:, cols]
                r = lax.rsqrt(jnp.mean(xf * xf, axis=-1, keepdims=True) + EPS)
                xhat = xf * r
                dxhat = dy * w_ref[...]
                dx_ref[:, cols] = (r * (dxhat - xhat * jnp.mean(dxhat * xhat, axis=-1, keepdims=True))).astype(BF16)
                dw_ref[...] += jnp.sum(dy * xhat, axis=0, keepdims=True)

    blk = lambda off: pl.BlockSpec((tm, bw), lambda i, h: (i, off + h))
    vec = pl.BlockSpec((1, HD), lambda i, h: (0, 0))
    shp = jax.ShapeDtypeStruct((lp, width), BF16)
    vshp = jax.ShapeDtypeStruct((1, HD), F32)
    return pl.pallas_call(
        body, name=name, grid=(lp // tm, nh), in_specs=[blk(0), blk(nh), vec, vec, blk(0), blk(0)],
        out_specs=(blk(0), blk(0), vec, vec), out_shape=(shp, shp, vshp, vshp),
        compiler_params=_cp("arbitrary", "arbitrary"))(proj, proj, qw, kw, dqn, dkn)


def _sb_tq(lp):
    return 3 * QB if lp % (3 * QB) == 0 else QB


def _sb_rows(kb):
    return pl.ds(kb * QB if isinstance(kb, int) else pl.multiple_of(kb * QB, QB), QB)


def _sb_scores(qk, t_idx, kb, masked):
    z = qk * (HD ** -0.5)
    sp = jnp.log(1.0 + jnp.exp(-jnp.abs(z)))
    lsz = jnp.minimum(z, 0.0) - sp
    lk = -jnp.maximum(z, 0.0) - sp
    if not masked:
        return None, lsz, lk
    s_idx = kb * QB + _iota((1, QB), 1)
    valid = (s_idx < t_idx) & (s_idx >= INERT)
    return valid, lsz, jnp.where(valid, lk, 0.0)


def _sb_fwd(qn, kn, vv, proj, gate_col0, name):
    lp, width = qn.shape
    tq = _sb_tq(lp)
    nh, nq, gb, nsub = width // HD, lp // tq, gate_col0 // HD, tq // QB

    def body(q_ref, k_ref, v_ref, g_ref, o_ref, og_ref, tot_ref):
        qb = pl.program_id(1)
        q = q_ref[...]
        t_idx = qb * tq + _iota((tq, 1), 0)
        upper = (_iota((QB, QB), 0) > _iota((QB, QB), 1)).astype(BF16)

        def step(kg, carry, masked):
            run, acc = carry
            kbs = [kg * nsub + sub for sub in reversed(range(nsub))]
            rows = [_sb_rows(kb) for kb in kbs]
            qks = [_dot_nt(q, k_ref[r, :]) for r in rows]
            scores = [_sb_scores(qk, t_idx, kb, masked) for qk, kb in zip(qks, kbs)]
            sums = [_dot_split(lk, upper) for _, _, lk in scores]
            probs = []
            for (valid, lsz, lk), part in zip(scores, sums):
                a = jnp.exp(lsz + part + run)
                probs.append((jnp.where(valid, a, 0.0) if masked else a).astype(BF16))
                run = run + jnp.sum(lk, axis=1, keepdims=True)
            for a, r in zip(probs, rows):
                acc = acc + _dot(a, v_ref[r, :])
            return run, acc

        carry = step(qb, (jnp.zeros((tq, 1), F32), jnp.zeros((tq, HD), F32)), True)
        carry = lax.fori_loop(1, qb, lambda i, cr: step(qb - i, cr, False), carry)
        run, acc = lax.fori_loop(0, jnp.minimum(qb, 1), lambda _, cr: step(0, cr, True), carry)
        o_ref[...] = acc
        og_ref[...] = (acc * _silu(g_ref[...])).astype(BF16)
        tot_ref[0, 0] = _col_to_row(run)

    full = pl.BlockSpec((lp, HD), lambda h, i: (0, h))
    blk = pl.BlockSpec((tq, HD), lambda h, i: (i, h))
    return pl.pallas_call(
        body, name=name, grid=(nh, nq),
        in_specs=[blk, full, full, pl.BlockSpec((tq, HD), lambda h, i: (i, gb + h))],
        out_specs=(blk, blk, pl.BlockSpec((1, 1, 1, tq), lambda h, i: (h, i, 0, 0))),
        out_shape=(jax.ShapeDtypeStruct((lp, width), F32), jax.ShapeDtypeStruct((lp, width), BF16),
                   jax.ShapeDtypeStruct((nh, nq, 1, tq), F32)),
        compiler_params=_cp("parallel", "parallel"))(qn, kn, vv, proj)


def _sb_bwd(qn, kn, vv, proj, gate_col0, att, tot, dog, name):
    lp, width = qn.shape
    tq = _sb_tq(lp)
    nh, nq, gb, nsub = width // HD, lp // tq, gate_col0 // HD, tq // QB
    scale = HD ** -0.5

    def body(q_ref, k_ref, v_ref, g_ref, att_ref, tot_ref, dog_ref, dq_ref, dk_ref, dv_ref, dg_ref, dk_acc, dv_acc):
        qb = pl.program_id(1)

        @pl.when(qb == 0)
        def _():
            dk_acc[...] = jnp.zeros_like(dk_acc)
            dv_acc[...] = jnp.zeros_like(dv_acc)

        q, gate, dg_out = q_ref[...], g_ref[...], dog_ref[...]
        d_o = (dg_out * _silu(gate)).astype(BF16)
        dg_ref[...] = (dg_out * att_ref[...] * _dsilu(gate)).astype(BF16)
        total = _row_to_col(tot_ref[0, 0])
        t_idx = qb * tq + _iota((tq, 1), 0)
        ri, ci = _iota((QB, QB), 0), _iota((QB, QB), 1)
        lower_incl = (ri <= ci).astype(BF16)
        lower_excl = (ri < ci).astype(BF16)

        def step(kg, carry, masked):
            run, erun, dq = carry
            kbs = [kg * nsub + sub for sub in range(nsub)]
            rows = [_sb_rows(kb) for kb in kbs]
            qks = [_dot_nt(q, k_ref[r, :]) for r in rows]
            dprobs = [_dot_nt(d_o, v_ref[r, :]) for r in rows]
            scores = [_sb_scores(qk, t_idx, kb, masked) for qk, kb in zip(qks, kbs)]
            sums = [_dot_split(lk, lower_incl) for _, _, lk in scores]
            probs, es = [], []
            for (valid, lsz, lk), part, dprob in zip(scores, sums, dprobs):
                a = jnp.exp(lsz + (total - run - part))
                a = jnp.where(valid, a, 0.0) if masked else a
                probs.append(a.astype(BF16))
                es.append(a * dprob)
                run = run + jnp.sum(lk, axis=1, keepdims=True)
            esums = [_dot_split(e, lower_excl) for e in es]
            for a, r in zip(probs, rows):
                dv_acc[r, :] += _dot_tn(a, d_o)
            dzs = []
            for (valid, lsz, _), e, part in zip(scores, es, esums):
                sig = jnp.exp(lsz)
                dz = e * (1.0 - sig) - sig * (erun + part)
                dz = jnp.where(valid, dz, 0.0) if masked else dz
                dzs.append((dz * scale).astype(BF16))
                erun = erun + jnp.sum(e, axis=1, keepdims=True)
            for dz, r in zip(dzs, rows):
                dk_acc[r, :] += _dot_tn(dz, q)
                dq = dq + _dot(dz, k_ref[r, :])
            return run, erun, dq

        zero = jnp.zeros((tq, 1), F32)
        carry = step(0, (zero, zero, jnp.zeros((tq, HD), F32)), True)
        carry = lax.fori_loop(1, qb, lambda kg, cr: step(kg, cr, False), carry)
        _, _, dq = lax.fori_loop(0, jnp.minimum(qb, 1), lambda _, cr: step(qb, cr, True), carry)
        dq_ref[...] = dq

        @pl.when(qb == nq - 1)
        def _():
            dk_ref[...] = dk_acc[...]
            dv_ref[...] = dv_acc[...].astype(BF16)

    full = pl.BlockSpec((lp, HD), lambda h, i: (0, h))
    blk = pl.BlockSpec((tq, HD), lambda h, i: (i, h))
    return pl.pallas_call(
        body, name=name, grid=(nh, nq),
        in_specs=[blk, full, full, pl.BlockSpec((tq, HD), lambda h, i: (i, gb + h)), blk,
                  pl.BlockSpec((1, 1, 1, tq), lambda h, i: (h, i, 0, 0)), blk],
        out_specs=(blk, full, full, blk),
        out_shape=(jax.ShapeDtypeStruct((lp, width), F32), jax.ShapeDtypeStruct((lp, width), F32),
                   jax.ShapeDtypeStruct((lp, width), BF16), jax.ShapeDtypeStruct((lp, width), BF16)),
        scratch_shapes=[pltpu.VMEM((lp, HD), F32), pltpu.VMEM((lp, HD), F32)],
        compiler_params=_cp("parallel", "arbitrary"))(qn, kn, vv, proj, att, tot, dog)


def _loss_head(h, target, name):
    lp, d = h.shape
    tm = _tile(PAD, 128)
    nt = lp // tm
    npad = PAD // tm

    def body(h_ref, t_ref, l_ref, dh_ref, dhb_ref):
        i = pl.program_id(0)
        err = jnp.where(i >= npad, h_ref[...] - t_ref[...], 0.0)
        dh = err * (1.0 / d)
        dh_ref[...] = dh
        dhb_ref[...] = dh.astype(BF16)

        @pl.when(i == 0)
        def _():
            l_ref[...] = jnp.zeros_like(l_ref)

        l_ref[...] += (0.5 / d) * jnp.sum(err * err, keepdims=True)

    row = pl.BlockSpec((tm, d), lambda i: (i, 0))
    return pl.pallas_call(
        body, name=name, grid=(nt,),
        in_specs=[row, pl.BlockSpec((tm, d), lambda i: (jnp.maximum(i - npad, 0), 0))],
        out_specs=(pl.BlockSpec((1, 1), lambda i: (0, 0)), row, row),
        out_shape=(jax.ShapeDtypeStruct((1, 1), F32), jax.ShapeDtypeStruct((lp, d), F32),
                   jax.ShapeDtypeStruct((lp, d), BF16)),
        compiler_params=_cp("arbitrary"))(h, target)


def _adamw(parts, w, m, v, name, mine=None):
    r, c = w.shape
    tr = _tile(r, max(8, (1 << 18) // c // 8 * 8))
    nparts = parts.shape[0]

    def body(*refs):
        p_ref, w_ref, m_ref, v_ref = refs[:4]
        g_ref, d_ref, mo_ref, vo_ref = refs[-4:]
        g = p_ref[0].astype(F32)
        if mine is not None:
            g = refs[4][...].astype(F32) + g
        for k in range(1, nparts):
            g = g + p_ref[k].astype(F32)
        mn = ADAM_B1 * m_ref[...] + (1.0 - ADAM_B1) * g
        vn = ADAM_B2 * v_ref[...] + (1.0 - ADAM_B2) * jnp.square(g)
        m_hat = mn / (1.0 - ADAM_B1 ** ADAM_STEP)
        v_hat = vn / (1.0 - ADAM_B2 ** ADAM_STEP)
        g_ref[...] = g
        d_ref[...] = -ADAM_LR * (m_hat / (jnp.sqrt(v_hat) + ADAM_EPS) + ADAM_WD * w_ref[...])
        mo_ref[...] = mn
        vo_ref[...] = vn

    blk = pl.BlockSpec((tr, c), lambda i: (i, 0))
    shp = jax.ShapeDtypeStruct((r, c), F32)
    extra = [] if mine is None else [mine]
    return pl.pallas_call(
        body, name=name, grid=(r // tr,),
        in_specs=[pl.BlockSpec((nparts, tr, c), lambda i: (0, i, 0)), blk, blk, blk] + [blk] * len(extra),
        out_specs=(blk, blk, blk, blk), out_shape=(shp, shp, shp, shp), compiler_params=_cp("parallel"))(parts, w, m, v, *extra)


def _unshard_cols(g):
    return jnp.transpose(g, (1, 0, 2)).reshape(g.shape[1], NDEV * g.shape[2])


def _shard_cols(a):
    r, c = a.shape
    return jnp.transpose(a.reshape(r, NDEV, c // NDEV), (1, 0, 2))


def kernel(x, meta_tokens, dn_norm_w, dn_w_in, dn_conv_w, dn_a_log, dn_dt_bias, dn_out_norm_w, dn_w_out, sb_norm_w, sb_w_in, sb_q_norm_w, sb_k_norm_w, sb_w_out, loss_target, m_meta_tokens, m_dn_norm_w, m_dn_w_in, m_dn_conv_w, m_dn_a_log, m_dn_dt_bias, m_dn_out_norm_w, m_dn_w_out, m_sb_norm_w, m_sb_w_in, m_sb_q_norm_w, m_sb_k_norm_w, m_sb_w_out, v_meta_tokens, v_dn_norm_w, v_dn_w_in, v_dn_conv_w, v_dn_a_log, v_dn_dt_bias, v_dn_out_norm_w, v_dn_w_out, v_sb_norm_w, v_sb_w_in, v_sb_q_norm_w, v_sb_k_norm_w, v_sb_w_out):
    seq, d = x.shape[1], x.shape[2]
    lp = PAD + seq
    key_w = d
    val_w = 2 * d
    hv = val_w // HD
    conv_w_cols = 2 * key_w + val_w
    main_w = conv_w_cols + val_w
    sb_w = d
    nc = lp // CH
    hk = key_w // HD

    (g_meta, g_sbn, g_conv) = _exchange([meta_tokens, sb_norm_w, dn_conv_w[0]], True, "gather_vectors")
    w_in0_mine = dn_w_in[0].astype(BF16)
    st_a, tok_a = _split_start([w_in0_mine], [lax.empty((NDEV,) + w_in0_mine.shape, BF16)], _plan_gather_chips, 4,
                               "gather_w_in0_chips_start", after=g_meta)
    meta = _unshard_cols(g_meta) + tok_a[:1, :1]
    sbn_w = _unshard_cols(g_sbn)
    conv_w = _unshard_cols(g_conv)
    h0 = jnp.concatenate([jnp.zeros((INERT, d), F32), meta, x[0]], axis=0)
    hn0 = _rms_fwd(h0, dn_norm_w, "dn_norm")
    lands_a, srcs_a = _split_wait(st_a, hn0, "gather_w_in0_chips_wait")
    st_a2, tok_a2 = _split_start([], lands_a, _plan_gather_forward, 3, "gather_w_in0_forward_start")
    st_b, tok_b = _exchange_start([dn_w_out[0].astype(BF16), sb_w_in[0].astype(BF16), sb_w_out[0].astype(BF16)],
                                  True, "gather_w_rest_start", after=tok_a2)
    lands_a, _ = _split_wait(st_a2, tok_b, "gather_w_in0_forward_wait")
    w_in0 = _unshard_cols(_with_own(lands_a[0], srcs_a[0]))
    main_cols, gate_cols = (0, main_w), (main_w, 2 * hv)

    proj0 = _matmul(hn0, w_in0, mode="nn", out_dtype=F32, tm=1056, tn=512, tk=4096, name="dn_in_proj", b_cols=main_cols)
    gl0 = _matmul(hn0, w_in0, mode="nn", out_dtype=F32, tm=1056, tn=512, tk=4096, name="dn_gate_proj", b_cols=gate_cols)
    act0 = _conv_fwd(proj0, conv_w, conv_w_cols, "dn_conv")
    a_log2 = jnp.concatenate([jnp.zeros_like(dn_a_log), dn_a_log], axis=1)
    dt_bias2 = jnp.concatenate([jnp.zeros_like(dn_dt_bias), dn_dt_bias], axis=1)
    bg = _gates_fwd(gl0, a_log2, dt_bias2, "dn_gates")
    pack = lambda t: jnp.transpose(t.reshape(nc, CH, hk, 2), (2, 0, 3, 1))
    gates = jnp.concatenate([pack(bg[:, :hv]), pack(bg[:, hv:]), jnp.zeros((hk, nc, 4, CH), F32)], axis=2)
    u0, w0, qd0, kd0, p0, t0 = _delta_local(act0, gates, key_w, "dn_delta_local")
    o0, vn0, states = _delta_scan(u0, w0, qd0, kd0, p0, gates, "dn_delta_scan")
    o0g = _outnorm_fwd(o0, proj0, conv_w_cols, dn_out_norm_w, "dn_out_norm")
    lands_b, srcs_b = _exchange_wait(st_b, o0g, "gather_w_rest_wait")
    g_out0, g_in1, g_out1 = [_with_own(l, s) for l, s in zip(lands_b, srcs_b)]
    w_out0 = g_out0.reshape(val_w, d)
    w_in1 = _unshard_cols(g_in1)
    w_out1 = g_out1.reshape(sb_w, d)
    h1 = _matmul(o0g, w_out0, mode="nn", out_dtype=F32, tm=1056, tn=512, tk=4096, name="dn_out_proj", add=h0)
    hn1 = _rms_fwd(h1, sbn_w, "sb_norm")
    proj1 = _matmul(hn1, w_in1, mode="nn", out_dtype=F32, tm=1056, tn=512, tk=4096, name="sb_in_proj")
    qn1, kn1, vv1 = _qknorm_fwd(proj1, sb_q_norm_w, sb_k_norm_w, sb_w, "sb_qk_norm")
    att1, o1g, tot1 = _sb_fwd(qn1, kn1, vv1, proj1, 3 * sb_w, "sb_attn")
    h2 = _matmul(o1g, w_out1, mode="nn", out_dtype=F32, tm=1056, tn=512, tk=4096, name="sb_out_proj", add=h1)
    loss_part, dh2, dh2b = _loss_head(h2, loss_target[0], "loss_head")
    loss = lax.psum(loss_part[0, 0], ("x", "y", "c"))

    p_out1 = _matmul(o1g, dh2b, mode="tn", out_dtype=BF16, tm=1024, tn=512, tk=lp, name="sb_out_wgrad")
    do1g = _matmul(dh2b, w_out1, mode="nt", out_dtype=F32, tm=1056, tn=512, tk=4096, name="sb_out_dgrad")
    dqn1, dkn1, dv1, dgate1 = _sb_bwd(qn1, kn1, vv1, proj1, 3 * sb_w, att1, tot1, do1g, "sb_attn_bwd")
    dq1, dk1, d_qw, d_kw = _qknorm_bwd(proj1, sb_q_norm_w, sb_k_norm_w, dqn1, dkn1, sb_w, "sb_qk_norm_bwd")
    dproj1 = jnp.concatenate([dq1, dk1, dv1, dgate1], axis=1)
    p_in1 = _matmul(hn1, dproj1, mode="tn", out_dtype=BF16, tm=1024, tn=512, tk=lp, name="sb_in_wgrad")
    st_s1, tok_s1 = _exchange_start([p_out1.reshape(NDEV, sb_w // NDEV, d), _shard_cols(p_in1)], False, "scatter_sb_start")
    dhn1 = _matmul(dproj1, w_in1, mode="nt", out_dtype=F32, tm=1056, tn=512, tk=4096, name="sb_in_dgrad", after=tok_s1)
    dh1, dh1b, d_sbn = _rms_bwd(h1, sbn_w, dhn1, dh2, "sb_norm_bwd")

    p_out0 = _matmul(o0g, dh1b, mode="tn", out_dtype=BF16, tm=1024, tn=512, tk=lp, name="dn_out_wgrad")
    st_s2, tok_s2 = _exchange_start([p_out0.reshape(NDEV, val_w // NDEV, d)], False, "scatter_dn_out_start")
    do0g = _matmul(dh1b, w_out0, mode="nt", out_dtype=F32, tm=1056, tn=512, tk=4096, name="dn_out_dgrad", after=tok_s2)
    do0, dz0, d_onw = _outnorm_bwd(o0, proj0, conv_w_cols, dn_out_norm_w, do0g, "dn_out_norm_bwd")
    dvn0, dw0, dqd0, dkd0, dp0, sd0 = _delta_scan_bwd(do0, w0, qd0, kd0, p0, vn0, states, gates, "dn_delta_scan_bwd")
    sd_rows = jnp.transpose(jnp.transpose(sd0[..., 0], (0, 2, 1)).reshape(hk, 2, nc), (0, 2, 1))
    gates_b = jnp.concatenate([gates[:, :, :4], jnp.broadcast_to(sd_rows[..., None], (hk, nc, 2, CH)),
                               jnp.zeros((hk, nc, 2, CH), F32)], axis=2)
    dq_act, dk_act, dv_act, dgates = _delta_local_bwd(act0, gates_b, t0, dvn0, dw0, dqd0, dkd0, dp0, key_w,
                                                      "dn_delta_local_bwd")
    unpack = lambda t: jnp.transpose(t, (1, 3, 0, 2)).reshape(lp, hv)
    dbg = jnp.concatenate([unpack(dgates[:, :, 0:2]), unpack(dgates[:, :, 2:4])], axis=1)
    dgl0, d_alog2, d_dtb2 = _gates_bwd(gl0, a_log2, dt_bias2, dbg, "dn_gates_bwd")
    d_alog, d_dtb = d_alog2[:, hv:], d_dtb2[:, hv:]
    dxq, dcw_q = _conv_bwd(proj0, 0, conv_w[:, :key_w], dq_act, "dn_conv_bwd_q")
    dxk, dcw_k = _conv_bwd(proj0, key_w, conv_w[:, key_w:2 * key_w], dk_act, "dn_conv_bwd_k")
    dxv, dcw_v = _conv_bwd(proj0, 2 * key_w, conv_w[:, 2 * key_w:], dv_act, "dn_conv_bwd_v")
    dproj0 = jnp.concatenate([dxq, dxk, dxv, dz0], axis=1)
    p_in0_main = _matmul(hn0, dproj0, mode="tn", out_dtype=BF16, tm=1024, tn=512, tk=lp, name="dn_in_wgrad")
    p_in0_gate = _matmul(hn0, dgl0, mode="tn", out_dtype=BF16, tm=1024, tn=512, tk=lp, name="dn_gate_wgrad")
    p_in0 = _shard_cols(jnp.concatenate([p_in0_main, p_in0_gate], axis=1))
    st_c, tok_c = _split_start([p_in0], [lax.empty((4,) + p_in0.shape[1:], BF16)], _plan_scatter_core, 4,
                               "scatter_dn_in_core_start")
    dhn0 = _matmul(dgl0, w_in0, mode="nt", out_dtype=F32, tm=1056, tn=512, tk=4096, name="dn_gate_dgrad", after=tok_c,
                   b_cols=gate_cols)
    (zone_c,), (p_in0,) = _split_wait(st_c, dhn0, "scatter_dn_in_core_wait")
    sums_in0 = _chip_sums(p_in0, zone_c, "dn_in_chip_sums")
    st_s3, tok_s3 = _split_start([sums_in0], [lax.empty((3,) + p_in0.shape[1:], BF16)], _plan_scatter_chips, 3,
                                 "scatter_dn_in_chips_start")
    dhn0 = _matmul(dproj0, w_in0, mode="nt", out_dtype=F32, tm=1056, tn=512, tk=4096, name="dn_in_dgrad", add=dhn0,
                   after=tok_s3, b_cols=main_cols)
    dh0, _, d_dnn = _rms_bwd(h0, dn_norm_w, dhn0, dh1, "dn_norm_bwd")
    grad_x = dh0[PAD:][None]

    p_conv = _shard_cols(jnp.concatenate([dcw_q, dcw_k, dcw_v], axis=1))
    (r_meta, r_sbn, r_conv) = _exchange([_shard_cols(dh0[INERT:PAD]), _shard_cols(d_sbn), p_conv], False, "scatter_vector_grads")
    small = jnp.concatenate([d_dnn, d_alog, d_dtb, d_onw, d_qw, d_kw], axis=1)
    (r_small,) = _exchange([small], True, "gather_replicated_grads")
    outs = {}
    outs["meta_tokens"] = _adamw(r_meta, meta_tokens, m_meta_tokens, v_meta_tokens, "adamw_meta")
    outs["dn_conv_w"] = _adamw(r_conv, dn_conv_w[0], m_dn_conv_w[0], v_dn_conv_w[0], "adamw_dn_conv")
    outs["sb_norm_w"] = _adamw(r_sbn, sb_norm_w, m_sb_norm_w, v_sb_norm_w, "adamw_sb_norm")
    me = _own_index()
    own = lambda src: lax.dynamic_index_in_dim(src, me, 0, keepdims=False)
    (r_out1, r_in1), (s_out1, s_in1) = _exchange_wait(st_s1, r_small, "scatter_sb_wait")
    outs["sb_w_in"] = _adamw(r_in1, sb_w_in[0], m_sb_w_in[0], v_sb_w_in[0], "adamw_sb_w_in", mine=own(s_in1))
    outs["sb_w_out"] = _adamw(r_out1, sb_w_out[0], m_sb_w_out[0], v_sb_w_out[0], "adamw_sb_w_out", mine=own(s_out1))
    (r_out0,), (s_out0,) = _exchange_wait(st_s2, outs["sb_w_in"][1], "scatter_dn_out_wait")
    outs["dn_w_out"] = _adamw(r_out0, dn_w_out[0], m_dn_w_out[0], v_dn_w_out[0], "adamw_dn_w_out", mine=own(s_out0))
    cat = lambda *a: jnp.concatenate(a, axis=1)
    rep = _adamw(r_small, cat(dn_norm_w, dn_a_log, dn_dt_bias, dn_out_norm_w, sb_q_norm_w, sb_k_norm_w),
                 cat(m_dn_norm_w, m_dn_a_log, m_dn_dt_bias, m_dn_out_norm_w, m_sb_q_norm_w, m_sb_k_norm_w),
                 cat(v_dn_norm_w, v_dn_a_log, v_dn_dt_bias, v_dn_out_norm_w, v_sb_q_norm_w, v_sb_k_norm_w),
                 "adamw_replicated")
    off = 0
    for nm, wd in (("dn_norm_w", d), ("dn_a_log", hv), ("dn_dt_bias", hv), ("dn_out_norm_w", HD),
                   ("sb_q_norm_w", HD), ("sb_k_norm_w", HD)):
        outs[nm] = tuple(t[:, off:off + wd] for t in rep)
        off += wd
    behind = jnp.broadcast_to(outs["dn_w_out"][1][0, 0] + rep[1][0, 0] + outs["sb_w_out"][1][0, 0] + outs["meta_tokens"][1][0, 0]
                              + outs["dn_conv_w"][1][0, 0] + outs["sb_norm_w"][1][0, 0], (8, 128))
    (r_in0,), (sums_in0,) = _split_wait(st_s3, behind, "scatter_dn_in_chips_wait")
    outs["dn_w_in"] = _adamw(r_in0, dn_w_in[0], m_dn_w_in[0], v_dn_w_in[0], "adamw_dn_w_in", mine=sums_in0[0])
    lead =("dn_w_in", "dn_conv_w", "dn_w_out", "sb_w_in", "sb_w_out")
    order = ("meta_tokens", "dn_norm_w", "dn_w_in", "dn_conv_w", "dn_a_log", "dn_dt_bias", "dn_out_norm_w", "dn_w_out",
             "sb_norm_w", "sb_w_in", "sb_q_norm_w", "sb_k_norm_w", "sb_w_out")
    fix = lambda nm, t: t[None] if nm in lead else t
    result = [loss, grad_x]
    for kind in range(4):
        result += [fix(nm, outs[nm][kind]) for nm in order]
    return tuple(result)
```

```python
import functools

import jax
import jax.numpy as jnp
from jax import lax
from jax.experimental import pallas as pl
from jax.experimental.pallas import tpu as pltpu

F32 = jnp.float32
BF16 = jnp.bfloat16
HD = 128
CH = 64
QB = 128
N_META = 16
PAD = 128
INERT = PAD - N_META
NDEV = 8
CONV_K = 4
EPS = 1e-6
VMEM_LIMIT = 56 * 1024 * 1024

ADAM_LR, ADAM_B1, ADAM_B2, ADAM_EPS, ADAM_WD, ADAM_STEP = 0.001, 0.9, 0.999, 1e-08, 0.01, 10
MESH = pl.DeviceIdType.MESH


def _cp(*sem):
    return pltpu.CompilerParams(dimension_semantics=sem, vmem_limit_bytes=VMEM_LIMIT)


def _tile(n, pref, mult=8):
    if n <= pref:
        return n
    for t in range(pref - pref % mult, 0, -mult):
        if n % t == 0:
            return t
    return n


def _silu(x):
    return x * jax.nn.sigmoid(x)


def _dsilu(x):
    s = jax.nn.sigmoid(x)
    return s * (1.0 + x * (1.0 - s))


def _dot(a, b, dims=((1,), (0,))):
    return lax.dot_general(a.astype(BF16), b.astype(BF16), (dims, ((), ())), preferred_element_type=F32)


def _dot_nt(a, b):
    return _dot(a, b, ((1,), (1,)))


def _dot_tn(a, b):
    return _dot(a, b, ((0,), (0,)))


def _dot_f32(a, b):
    dn = (((1,), (0,)), ((), ()))
    ah, bh = a.astype(BF16), b.astype(BF16)
    al, bl = (a - ah.astype(F32)).astype(BF16), (b - bh.astype(F32)).astype(BF16)
    mm = lambda x, y: lax.dot_general(x, y, dn, preferred_element_type=F32)
    return mm(ah, bh) + (mm(ah, bl) + mm(al, bh))


def _dot_split(a, m):
    hi = a.astype(BF16)
    lo = (a - hi.astype(F32)).astype(BF16)
    dn = (((1,), (0,)), ((), ()))
    return (lax.dot_general(hi, m, dn, preferred_element_type=F32)
            + lax.dot_general(lo, m, dn, preferred_element_type=F32))


def _iota(shape, dim):
    return lax.broadcasted_iota(jnp.int32, shape, dim)


def _col_to_row(col):
    n = col.shape[0]
    eye = _iota((n, n), 0) == _iota((n, n), 1)
    return jnp.sum(jnp.where(eye, col, 0.0), axis=0, keepdims=True)


def _row_to_col(row):
    n = row.shape[1]
    eye = _iota((n, n), 0) == _iota((n, n), 1)
    return jnp.sum(jnp.where(eye, row, 0.0), axis=1, keepdims=True)


def _exchange(arrs, gather, name):
    n = len(arrs)

    def body(*refs):
        ins, outs = refs[:n], refs[n:2 * n]
        send_sems, recv_sems, local_sems = refs[2 * n:]
        x, y, c = lax.axis_index("x"), lax.axis_index("y"), lax.axis_index("c")
        me = 4 * x + 2 * y + c
        sends = []
        for i in range(n):
            mine = pltpu.make_async_copy(ins[i] if gather else ins[i].at[me], outs[i].at[me], local_sems.at[i])
            mine.start()
            sends.append(mine)
        for k in range(1, NDEV):
            px, py, pc = x ^ (k >> 2), y ^ ((k >> 1) & 1), c ^ (k & 1)
            peer = 4 * px + 2 * py + pc
            for i in range(n):
                cp = pltpu.make_async_remote_copy(
                    src_ref=ins[i] if gather else ins[i].at[peer], dst_ref=outs[i].at[me],
                    send_sem=send_sems.at[i * NDEV + k], recv_sem=recv_sems.at[i * NDEV + k],
                    device_id=(px, py, pc), device_id_type=MESH)
                cp.start()
                sends.append(cp)
        for k in range(1, NDEV):
            px, py, pc = x ^ (k >> 2), y ^ ((k >> 1) & 1), c ^ (k & 1)
            peer = 4 * px + 2 * py + pc
            for i in range(n):
                pltpu.make_async_remote_copy(
                    src_ref=outs[i].at[peer], dst_ref=outs[i].at[peer],
                    send_sem=send_sems.at[i * NDEV + k], recv_sem=recv_sems.at[i * NDEV + k],
                    device_id=(px, py, pc), device_id_type=MESH).wait_recv()
        for i in range(n):
            sends[i].wait()
        for cp in sends[n:]:
            cp.wait_send()

    hbm = pl.BlockSpec(memory_space=pltpu.HBM)
    out_shape = tuple(jax.ShapeDtypeStruct(((NDEV,) + a.shape) if gather else a.shape, a.dtype) for a in arrs)
    return pl.pallas_call(
        body, name=name, out_shape=out_shape, in_specs=[hbm] * n, out_specs=tuple([hbm] * n),
        scratch_shapes=[pltpu.SemaphoreType.DMA((n * NDEV,)), pltpu.SemaphoreType.DMA((n * NDEV,)),
                        pltpu.SemaphoreType.DMA((n,))],
        compiler_params=pltpu.CompilerParams(has_side_effects=True),
    )(*arrs)


_HBM_SPEC = pl.BlockSpec(memory_space=pltpu.HBM)
_SEM_SPEC = pl.BlockSpec(memory_space=pltpu.SEMAPHORE)
_EFFECT = pltpu.SideEffectType.DATAFLOW_SIDE_EFFECTING


def _place():
    return lax.axis_index("x"), lax.axis_index("y"), lax.axis_index("c")


def _dev(px, py, pc):
    return 4 * px + 2 * py + pc


def _plan_direct(n, gather):
    def plan(ins, lnd, for_wait):
        x, y, c = _place()
        copies = []
        for k in range(1, NDEV):
            px, py, pc = x ^ (k >> 2), y ^ ((k >> 1) & 1), c ^ (k & 1)
            for i in range(n):
                slot = (_dev(px, py, pc) if for_wait else _dev(x, y, c)) if gather else k - 1
                copies.append((ins[i] if gather else ins[i].at[_dev(px, py, pc)], lnd[i].at[slot], (px, py, pc)))
        return copies
    return plan


def _plan_gather_chips(ins, lnd, for_wait):
    x, y, c = _place()
    copies = []
    for k in range(4):
        px, py, pc = (x, y, 1 - c) if k == 0 else (x ^ (k >> 1), y ^ (k & 1), c)
        copies.append((ins[0], lnd[0].at[_dev(px, py, pc) if for_wait else _dev(x, y, c)], (px, py, pc)))
    return copies


def _plan_gather_forward(ins, lnd, for_wait):
    x, y, c = _place()
    copies = []
    for k in range(1, 4):
        px, py = x ^ (k >> 1), y ^ (k & 1)
        copies.append((lnd[0].at[_dev(px, py, c)], lnd[0].at[_dev(px, py, 1 - c if for_wait else c)], (x, y, 1 - c)))
    return copies


def _plan_scatter_core(ins, lnd, for_wait):
    x, y, c = _place()
    return [(ins[0].at[_dev(x ^ (k >> 1), y ^ (k & 1), 1 - c)], lnd[0].at[k], (x, y, 1 - c)) for k in range(4)]


def _plan_scatter_chips(ins, lnd, for_wait):
    x, y, c = _place()
    return [(ins[0].at[k], lnd[0].at[k - 1], (x ^ (k >> 1), y ^ (k & 1), c)) for k in range(1, 4)]


def _plan_descriptors(plan, ins, lnd, send_sems, recv_sems, for_wait):
    return [pltpu.make_async_remote_copy(src_ref=src, dst_ref=dst, send_sem=send_sems.at[j], recv_sem=recv_sems.at[j],
                                         device_id=dev, device_id_type=MESH)
            for j, (src, dst, dev) in enumerate(plan(ins, lnd, for_wait))]


def _split_start(srcs, lands, plan, ncopies, name, after=None):
    ns, nl = len(srcs), len(lands)
    extra = [] if after is None else [after]

    def body(*refs):
        ins, lnd = refs[:ns], refs[ns:ns + nl]
        send_sems, recv_sems = refs[ns + nl + len(extra)], refs[ns + nl + len(extra) + 1]
        token = refs[-1]
        for cp in _plan_descriptors(plan, ins, lnd, send_sems, recv_sems, False):
            cp.start()
        token[...] = jnp.zeros_like(token)

    sems = pltpu.SemaphoreType.DMA((ncopies,))
    both = list(srcs) + list(lands)
    outs = pl.pallas_call(
        body, name=name,
        out_shape=(sems, sems, *[pltpu.HBM(a.shape, a.dtype) for a in both], jax.ShapeDtypeStruct((8, 128), F32)),
        in_specs=[_HBM_SPEC] * (ns + nl) + [pl.BlockSpec(memory_space=pl.ANY)] * len(extra),
        out_specs=(_SEM_SPEC, _SEM_SPEC, *[_HBM_SPEC] * (ns + nl), pl.BlockSpec(memory_space=pltpu.VMEM)),
        input_output_aliases={i: 2 + i for i in range(ns + nl)},
        compiler_params=pltpu.CompilerParams(has_side_effects=_EFFECT),
    )(*[pltpu.with_memory_space_constraint(a, pltpu.HBM) for a in both], *extra)
    return (outs[0], outs[1], list(outs[2:2 + ns]), list(outs[2 + ns:2 + ns + nl]), plan), outs[-1]


def _split_wait(state, after, name):
    send_sems, recv_sems, srcs, lands, plan = state
    ns, nl = len(srcs), len(lands)

    def body(*refs):
        ins, lnd = refs[:ns], refs[ns:ns + nl]
        for cp in _plan_descriptors(plan, ins, lnd, refs[ns + nl], refs[ns + nl + 1], True):
            cp.wait_send()
            cp.wait_recv()

    outs = pl.pallas_call(
        body, name=name,
        out_shape=tuple(pltpu.HBM(a.shape, a.dtype) for a in srcs + lands),
        in_specs=[_HBM_SPEC] * (ns + nl) + [_SEM_SPEC, _SEM_SPEC, pl.BlockSpec(memory_space=pl.ANY)],
        out_specs=tuple([_HBM_SPEC] * (ns + nl)), input_output_aliases={i: i for i in range(ns + nl)},
        compiler_params=pltpu.CompilerParams(has_side_effects=_EFFECT),
    )(*srcs, *lands, send_sems, recv_sems, after)
    return list(outs[ns:]), list(outs[:ns])


def _exchange_start(arrs, gather, name, after=None):
    lands = [lax.empty(((NDEV,) + a.shape) if gather else ((NDEV - 1,) + a.shape[1:]), a.dtype) for a in arrs]
    return _split_start(arrs, lands, _plan_direct(len(arrs), gather), len(arrs) * (NDEV - 1), name, after)


_exchange_wait = _split_wait


def _chip_sums(parts, zone, name):
    _, r, c = parts.shape
    tr = _tile(r, max(16, (1 << 19) // c // 16 * 16), 16)
    x, y, core = _place()
    mine = jnp.stack([lax.dynamic_index_in_dim(parts, _dev(x ^ (k >> 1), y ^ (k & 1), core), 0, keepdims=False)
                      for k in range(4)])

    def body(p_ref, z_ref, o_ref):
        o_ref[...] = (p_ref[...].astype(F32) + z_ref[...].astype(F32)).astype(o_ref.dtype)

    blk = pl.BlockSpec((1, tr, c), lambda k, i: (k, i, 0))
    return pl.pallas_call(
        body, name=name, grid=(4, r // tr), in_specs=[blk, blk], out_specs=blk,
        out_shape=jax.ShapeDtypeStruct((4, r, c), parts.dtype),
        compiler_params=_cp("parallel", "parallel"))(mine, zone)


def _own_index():
    return 4 * lax.axis_index("x") + 2 * lax.axis_index("y") + lax.axis_index("c")


def _with_own(land, mine):
    return lax.dynamic_update_index_in_dim(land, mine, _own_index(), 0)


def _matmul(a, b, *, mode, out_dtype, tm, tn, tk, name, add=None, after=None, b_cols=None, sharded=False):
    shard_c = None
    if mode == "nn":
        (m, kd), (_, n) = a.shape, b.shape[-2:]
    elif mode == "nt":
        (m, kd), (n, _) = a.shape, b.shape[-2:]
    else:
        (kd, m), (_, n) = a.shape, b.shape
    if sharded:
        shard_c = b.shape[2] if mode in ("nn", "nt") else n // NDEV
        n = NDEV * shard_c if mode == "nn" else n
    if b_cols is not None and mode == "nn":
        n = b_cols[1]
    tm, tn, tk = _tile(m, tm, 16), _tile(shard_c if sharded and mode != "nt" else n, tn, 128), _tile(
        shard_c if sharded and mode == "nt" else kd, tk, 128)
    nk = kd // tk
    jb = kb = 0
    if b_cols is not None:
        assert mode in ("nn", "nt") and b_cols[0] % (tn if mode == "nn" else tk) == 0 and (mode == "nn" or b_cols[1] == kd)
        jb, kb = (b_cols[0] // tn, 0) if mode == "nn" else (0, b_cols[0] // tk)
    a_spec = pl.BlockSpec((tk, tm), lambda i, j, k: (k, i)) if mode == "tn" else pl.BlockSpec((tm, tk), lambda i, j, k: (i, k))
    b_spec = (pl.BlockSpec((tn, tk), lambda i, j, k: (j, kb + k)) if mode == "nt"
              else pl.BlockSpec((tk, tn), lambda i, j, k: (k, jb + j)))
    o_spec = pl.BlockSpec((tm, tn), lambda i, j, k: (i, j))
    out_shape = jax.ShapeDtypeStruct((m, n), out_dtype)
    if sharded:
        assert b_cols is None
        per = shard_c // (tk if mode == "nt" else tn)
        if mode == "nn":
            b_spec = pl.BlockSpec((None, tk, tn), lambda i, j, k: (j // per, k, j % per))
        elif mode == "nt":
            b_spec = pl.BlockSpec((None, tn, tk), lambda i, j, k: (k // per, j, k % per))
        else:
            o_spec = pl.BlockSpec((None, tm, tn), lambda i, j, k: (j // per, i, j % per))
            out_shape = jax.ShapeDtypeStruct((NDEV, m, shard_c), out_dtype)
    dims = {"nn": ((1,), (0,)), "nt": ((1,), (1,)), "tn": ((0,), (0,))}[mode]

    def body(*refs, nk):
        a_ref, b_ref = refs[0], refs[1]
        o_ref, acc_ref = refs[-2], refs[-1]
        k = pl.program_id(2)

        @pl.when(k == 0)
        def _():
            acc_ref[...] = jnp.zeros_like(acc_ref)

        acc_ref[...] += lax.dot_general(a_ref[...], b_ref[...], (dims, ((), ())), preferred_element_type=F32)

        @pl.when(k == nk - 1)
        def _():
            r = acc_ref[...]
            if add is not None:
                r = r + refs[2][...]
            o_ref[...] = r.astype(o_ref.dtype)

    ins, specs = [a, b], [a_spec, b_spec]
    if add is not None:
        ins.append(add)
        specs.append(o_spec)
    if after is not None:
        ins.append(after)
        specs.append(pl.BlockSpec(after.shape, lambda i, j, k: (0, 0)))
    return pl.pallas_call(
        functools.partial(body, nk=nk), name=name, grid=(m // tm, n // tn, nk),
        in_specs=specs, out_specs=o_spec, out_shape=out_shape,
        scratch_shapes=[pltpu.VMEM((tm, tn), F32)], compiler_params=_cp("parallel", "parallel", "arbitrary"),
    )(*ins)


def _rms_fwd(h, w, name):
    lp, d = h.shape
    tm = _tile(lp, 384)

    def body(h_ref, w_ref, o_ref):
        xf = h_ref[...]
        r = lax.rsqrt(jnp.mean(xf * xf, axis=-1, keepdims=True) + EPS)
        o_ref[...] = (xf * r * w_ref[...]).astype(o_ref.dtype)

    return pl.pallas_call(
        body, name=name, grid=(lp // tm,),
        in_specs=[pl.BlockSpec((tm, d), lambda i: (i, 0)), pl.BlockSpec((1, d), lambda i: (0, 0))],
        out_specs=pl.BlockSpec((tm, d), lambda i: (i, 0)), out_shape=jax.ShapeDtypeStruct((lp, d), BF16),
        compiler_params=_cp("parallel"))(h, w)


def _rms_bwd(h, w, dhn, dres, name):
    lp, d = h.shape
    tm = _tile(lp, 192)

    def body(h_ref, w_ref, dy_ref, dres_ref, dh_ref, dhb_ref, dw_ref):
        xf = h_ref[...]
        r = lax.rsqrt(jnp.mean(xf * xf, axis=-1, keepdims=True) + EPS)
        xhat = xf * r
        dy = dy_ref[...]
        dxhat = dy * w_ref[...]
        dx = r * (dxhat - xhat * jnp.mean(dxhat * xhat, axis=-1, keepdims=True))
        dh = dres_ref[...] + dx
        dh_ref[...] = dh
        dhb_ref[...] = dh.astype(BF16)

        @pl.when(pl.program_id(0) == 0)
        def _():
            dw_ref[...] = jnp.zeros_like(dw_ref)

        dw_ref[...] += jnp.sum(dy * xhat, axis=0, keepdims=True)

    row = pl.BlockSpec((tm, d), lambda i: (i, 0))
    vec = pl.BlockSpec((1, d), lambda i: (0, 0))
    return pl.pallas_call(
        body, name=name, grid=(lp // tm,), in_specs=[row, vec, row, row], out_specs=(row, row, vec),
        out_shape=(jax.ShapeDtypeStruct((lp, d), F32), jax.ShapeDtypeStruct((lp, d), BF16),
                   jax.ShapeDtypeStruct((1, d), F32)),
        compiler_params=_cp("arbitrary"))(h, w, dhn, dres)


def _conv_pre(xx, w, rows, off):
    acc = None
    for j in range(CONV_K):
        sh = CONV_K - 1 - j
        term = (pltpu.roll(xx, sh, 0) if sh else xx)[off:off + rows] * w[j]
        acc = term if acc is None else acc + term
    return acc


def _conv_fwd(proj, conv_w, ncols, name):
    lp = proj.shape[0]
    tm, tc = _tile(lp, 384), _tile(ncols, 1024, 128)
    hb = tm // 8

    def body(x_ref, xb_ref, w_ref, o_ref):
        before = jnp.where(pl.program_id(0) > 0, xb_ref[...], 0.0)
        xx = jnp.concatenate([before, x_ref[...]], axis=0)
        o_ref[...] = _silu(_conv_pre(xx, [w_ref[j:j + 1, :] for j in range(CONV_K)], tm, 8))

    return pl.pallas_call(
        body, name=name, grid=(lp // tm, ncols // tc),
        in_specs=[pl.BlockSpec((tm, tc), lambda i, j: (i, j)),
                  pl.BlockSpec((8, tc), lambda i, j: (jnp.maximum(i * hb - 1, 0), j)),
                  pl.BlockSpec((CONV_K, tc), lambda i, j: (0, j))],
        out_specs=pl.BlockSpec((tm, tc), lambda i, j: (i, j)),
        out_shape=jax.ShapeDtypeStruct((lp, ncols), F32), compiler_params=_cp("parallel", "parallel"))(proj, proj, conv_w)


def _conv_bwd(proj, col0, conv_w, dact, name):
    lp, ncols = dact.shape
    tm, tc = _tile(lp, 384), _tile(ncols, 512, 128)
    hb, nt, cb0 = tm // 8, lp // tm, col0 // tc
    assert col0 % tc == 0

    def body(x_ref, xb_ref, xa_ref, d_ref, da_ref, w_ref, dx_ref, dw_ref):
        i = pl.program_id(1)
        before = jnp.where(i > 0, xb_ref[...], 0.0)
        last = i == nt - 1
        xx = jnp.concatenate([before, x_ref[...], jnp.where(last, 0.0, xa_ref[...])], axis=0)
        w = [w_ref[j:j + 1, :] for j in range(CONV_K)]
        pre = _conv_pre(xx, w, tm + 8, 8)
        dd = jnp.concatenate([d_ref[...], jnp.where(last, 0.0, da_ref[...])], axis=0)
        dpre = dd * _dsilu(pre)
        dx = None
        for j in range(CONV_K):
            sh = CONV_K - 1 - j
            term = (pltpu.roll(dpre, tm + 8 - sh, 0) if sh else dpre)[:tm] * w[j]
            dx = term if dx is None else dx + term
        dx_ref[...] = dx.astype(BF16)

        @pl.when(i == 0)
        def _():
            dw_ref[...] = jnp.zeros_like(dw_ref)

        for j in range(CONV_K):
            sh = CONV_K - 1 - j
            xs = (pltpu.roll(xx, sh, 0) if sh else xx)[8:8 + tm]
            dw_ref[j:j + 1, :] += jnp.sum(dpre[:tm] * xs, axis=0, keepdims=True)

    return pl.pallas_call(
        body, name=name, grid=(ncols // tc, nt),
        in_specs=[pl.BlockSpec((tm, tc), lambda j, i: (i, cb0 + j)),
                  pl.BlockSpec((8, tc), lambda j, i: (jnp.maximum(i * hb - 1, 0), cb0 + j)),
                  pl.BlockSpec((8, tc), lambda j, i: (jnp.minimum((i + 1) * hb, nt * hb - 1), cb0 + j)),
                  pl.BlockSpec((tm, tc), lambda j, i: (i, j)),
                  pl.BlockSpec((8, tc), lambda j, i: (jnp.minimum((i + 1) * hb, nt * hb - 1), j)),
                  pl.BlockSpec((CONV_K, tc), lambda j, i: (0, j))],
        out_specs=(pl.BlockSpec((tm, tc), lambda j, i: (i, j)), pl.BlockSpec((CONV_K, tc), lambda j, i: (0, j))),
        out_shape=(jax.ShapeDtypeStruct((lp, ncols), BF16), jax.ShapeDtypeStruct((CONV_K, ncols), F32)),
        compiler_params=_cp("parallel", "arbitrary"))(proj, proj, proj, dact, dact, conv_w)


def _softplus(x):
    return jnp.maximum(x, 0.0) + jnp.log(1.0 + jnp.exp(-jnp.abs(x)))


def _gates_fwd(gl, a_log2, dt_bias2, name):
    lp, w2 = gl.shape
    hv = w2 // 2

    def body(gl_ref, al_ref, dt_ref, o_ref):
        x = gl_ref[...]
        live = _iota((lp, 1), 0) >= INERT
        is_beta = _iota((1, w2), 1) < hv
        g = -jnp.exp(al_ref[...]) * _softplus(x + dt_ref[...])
        o_ref[...] = jnp.where(live, jnp.where(is_beta, jax.nn.sigmoid(x), g), 0.0)

    return pl.pallas_call(body, name=name, out_shape=jax.ShapeDtypeStruct((lp, w2), F32))(gl, a_log2, dt_bias2)


def _gates_bwd(gl, a_log2, dt_bias2, dbg, name):
    lp, w2 = gl.shape
    hv = w2 // 2

    def body(gl_ref, al_ref, dt_ref, d_ref, dl_ref, dal_ref, ddt_ref):
        x = gl_ref[...]
        live = _iota((lp, 1), 0) >= INERT
        is_beta = _iota((1, w2), 1) < hv
        d = jnp.where(live, d_ref[...], 0.0)
        beta = jax.nn.sigmoid(x)
        ea = jnp.exp(al_ref[...])
        u = x + dt_ref[...]
        dg = jnp.where(is_beta, 0.0, d)
        dal_ref[...] = jnp.sum(dg * (-ea) * _softplus(u), axis=0, keepdims=True)
        du = dg * (-ea) * jax.nn.sigmoid(u)
        ddt_ref[...] = jnp.sum(du, axis=0, keepdims=True)
        dl_ref[...] = jnp.where(is_beta, d * beta * (1.0 - beta), du).astype(BF16)

    vec = jax.ShapeDtypeStruct((1, w2), F32)
    return pl.pallas_call(
        body, name=name, out_shape=(jax.ShapeDtypeStruct((lp, w2), BF16), vec, vec))(gl, a_log2, dt_bias2, dbg)


def _l2n(x):
    r = lax.rsqrt(jnp.sum(x * x, axis=-1, keepdims=True) + EPS)
    return x * r, r


def _tri_inverse(mats):
    eye = (_iota((CH, CH), 0) == _iota((CH, CH), 1)).astype(F32)
    ts = [eye - a for a in mats]
    ps = [_dot_f32(a, a) for a in mats]
    n = 2
    while n < CH:
        ts = [t + _dot_f32(t, p) for t, p in zip(ts, ps)]
        n *= 2
        if n < CH:
            ps = [_dot_f32(p, p) for p in ps]
    return ts


def _chunk_local(qn, kn, v, b_row, g_row):
    ri, ci = _iota((CH, CH), 0), _iota((CH, CH), 1)
    incl, strict = ri >= ci, ri > ci
    gam_col = jnp.sum(jnp.where(incl, g_row, 0.0), axis=1, keepdims=True)
    gam_row = _col_to_row(gam_col)
    b_col = _row_to_col(b_row)
    dec = jnp.exp(jnp.where(incl, gam_col - gam_row, -jnp.inf))
    eg = jnp.exp(gam_col)
    gl = jnp.sum(g_row, axis=1, keepdims=True)
    ekd = jnp.exp(gl - gam_col)
    kb = kn * b_col
    a = jnp.where(strict, _dot_nt(kb, kn) * dec, 0.0)
    p = jnp.where(incl, _dot_nt(qn, kn) * dec, 0.0)
    return dict(dec=dec, eg=eg, ekd=ekd, kb=kb, vb=v * b_col, a=a, kbg=kb * eg, p=p, qd=qn * eg, kd=kn * ekd,
                b_col=b_col, incl=incl, strict=strict)


def _chunks_per_step(nc):
    return max(g for g in (1, 2, 3, 6, 11) if nc % g == 0)


def _heads_per_step(hv):
    return min(hv, 32)


def _delta_local(act, gates, key_w, name):
    lp = act.shape[0]
    hk, nc = key_w // HD, lp // CH
    hv = 2 * hk
    g = _chunks_per_step(nc)
    tr = g * CH

    def body(q_ref, k_ref, v_ref, g_ref, u_ref, w_ref, qd_ref, kd_ref, p_ref, t_ref):
        items = []
        for j in range(g):
            rows = slice(j * CH, (j + 1) * CH)
            qn = _l2n(q_ref[rows, :])[0] * (HD ** -0.5)
            kn = _l2n(k_ref[rows, :])[0]
            for e in range(2):
                cols = slice(e * HD, (e + 1) * HD)
                r = _chunk_local(qn, kn, v_ref[rows, cols], g_ref[0, j, e:e + 1, :], g_ref[0, j, 2 + e:3 + e, :])
                qd_ref[rows, cols] = r["qd"].astype(BF16)
                kd_ref[rows, cols] = r["kd"].astype(BF16)
                p_ref[e, rows, :] = r["p"].astype(BF16)
                items.append((rows, cols, e, r["a"], r["vb"].astype(BF16), r["kbg"].astype(BF16)))
        ts = [t.astype(BF16) for t in _tri_inverse([it[3] for it in items])]
        us = [_dot(t, it[4]) for t, it in zip(ts, items)]
        ws = [_dot(t, it[5]) for t, it in zip(ts, items)]
        for (rows, cols, e, _, _, _), t, u, w in zip(items, ts, us, ws):
            u_ref[rows, cols] = u
            w_ref[rows, cols] = w.astype(BF16)
            t_ref[e, rows, :] = t

    wide = pl.BlockSpec((tr, 2 * HD), lambda h, c: (c, h))
    sq = pl.BlockSpec((2, tr, CH), lambda h, c: (h, c, 0))
    wshape = lambda dt: jax.ShapeDtypeStruct((lp, hv * HD), dt)
    sshape = jax.ShapeDtypeStruct((hv, lp, CH), BF16)
    return pl.pallas_call(
        body, name=name, grid=(hk, nc // g),
        in_specs=[pl.BlockSpec((tr, HD), lambda h, c: (c, h)),
                  pl.BlockSpec((tr, HD), lambda h, c: (c, hk + h)),
                  pl.BlockSpec((tr, 2 * HD), lambda h, c: (c, hk + h)),
                  pl.BlockSpec((1, g, 8, CH), lambda h, c: (h, c, 0, 0))],
        out_specs=(wide, wide, wide, wide, sq, sq),
        out_shape=(wshape(F32), wshape(BF16), wshape(BF16), wshape(BF16), sshape, sshape),
        compiler_params=_cp("parallel", "parallel"))(act, act, act, gates)


def _chunk_decay(g_ref, e):
    return jnp.exp(jnp.sum(g_ref[e // 2, 0, 2 + e % 2:3 + e % 2, :], axis=1, keepdims=True))


def _delta_scan(u, w, qd, kd, p, gates, name):
    lp, val = u.shape
    hv, nc = val // HD, lp // CH
    nh = _heads_per_step(hv)

    def body(u_ref, w_ref, qd_ref, kd_ref, p_ref, g_ref, o_ref, vn_ref, st_ref, s_scr):
        @pl.when(pl.program_id(1) == 0)
        def _():
            s_scr[...] = jnp.zeros_like(s_scr)

        heads = range(nh)
        col = lambda e: slice(e * HD, (e + 1) * HD)
        ss = [s_scr[e] for e in heads]
        sb = [s.astype(BF16) for s in ss]
        for e in heads:
            st_ref[0, e] = ss[e]
        ws = [_dot(w_ref[:, col(e)], sb[e]) for e in heads]
        qs = [_dot(qd_ref[:, col(e)], sb[e]) for e in heads]
        vns = [(u_ref[:, col(e)] - ws[e]).astype(BF16) for e in heads]
        pv = [_dot(p_ref[e], vns[e]) for e in heads]
        kv = [_dot_tn(kd_ref[:, col(e)], vns[e]) for e in heads]
        for e in heads:
            o_ref[:, col(e)] = qs[e] + pv[e]
            s_scr[e] = _chunk_decay(g_ref, e) * ss[e] + kv[e]
            vn_ref[:, col(e)] = vns[e]

    wide = pl.BlockSpec((CH, nh * HD), lambda h, c: (c, h))
    return pl.pallas_call(
        body, name=name, grid=(hv // nh, nc),
        in_specs=[wide, wide, wide, wide, pl.BlockSpec((nh, CH, CH), lambda h, c: (h, c, 0)),
                  pl.BlockSpec((nh // 2, 1, 8, CH), lambda h, c: (h, c, 0, 0))],
        out_specs=(wide, wide, pl.BlockSpec((1, nh, HD, HD), lambda h, c: (c, h, 0, 0))),
        out_shape=(jax.ShapeDtypeStruct((lp, val), F32), jax.ShapeDtypeStruct((lp, val), BF16),
                   jax.ShapeDtypeStruct((nc, hv, HD, HD), F32)),
        scratch_shapes=[pltpu.VMEM((nh, HD, HD), F32)],
        compiler_params=_cp("parallel", "arbitrary"))(u, w, qd, kd, p, gates)


def _delta_scan_bwd(do, w, qd, kd, p, vn, states, gates, name):
    lp, val = do.shape
    hv, nc = val // HD, lp // CH
    nh = _heads_per_step(hv)

    def body(do_ref, w_ref, qd_ref, kd_ref, p_ref, vn_ref, st_ref, g_ref,
             dvn_ref, dw_ref, dqd_ref, dkd_ref, dp_ref, sd_ref, ds_scr):
        @pl.when(pl.program_id(1) == 0)
        def _():
            ds_scr[...] = jnp.zeros_like(ds_scr)

        incl = _iota((CH, CH), 0) >= _iota((CH, CH), 1)
        heads = range(nh)
        col = lambda e: slice(e * HD, (e + 1) * HD)
        ss = [st_ref[0, e] for e in heads]
        dss = [ds_scr[e] for e in heads]
        sb = [s.astype(BF16) for s in ss]
        dsb = [d.astype(BF16) for d in dss]
        dos = [do_ref[:, col(e)].astype(BF16) for e in heads]
        egl = [_chunk_decay(g_ref, e) for e in heads]
        pdo = [_dot_tn(p_ref[e], dos[e]) for e in heads]
        kds = [_dot(kd_ref[:, col(e)], dsb[e]) for e in heads]
        qdo = [_dot_tn(qd_ref[:, col(e)], dos[e]) for e in heads]
        dqd = [_dot_nt(dos[e], sb[e]) for e in heads]
        dkd = [_dot_nt(vn_ref[:, col(e)], dsb[e]) for e in heads]
        dpp = [_dot_nt(dos[e], vn_ref[:, col(e)]) for e in heads]
        dvn = [(pdo[e] + kds[e]).astype(BF16) for e in heads]
        wdv = [_dot_tn(w_ref[:, col(e)], dvn[e]) for e in heads]
        dws = [_dot_nt(dvn[e], sb[e]) for e in heads]
        for e in heads:
            ds_scr[e] = qdo[e] + egl[e] * dss[e] - wdv[e]
            dvn_ref[:, col(e)] = dvn[e]
            dw_ref[:, col(e)] = (-dws[e]).astype(BF16)
            dqd_ref[:, col(e)] = dqd[e]
            dkd_ref[:, col(e)] = dkd[e]
            dp_ref[e] = jnp.where(incl, dpp[e], 0.0)
            sd_ref[0, 0, e:e + 1, :] = jnp.broadcast_to(egl[e] * jnp.sum(ss[e] * dss[e], keepdims=True), (1, HD))

    rev = lambda c: nc - 1 - c
    wide = pl.BlockSpec((CH, nh * HD), lambda h, c: (rev(c), h))
    sq = pl.BlockSpec((nh, CH, CH), lambda h, c: (h, rev(c), 0))
    wshape = lambda dt: jax.ShapeDtypeStruct((lp, val), dt)
    return pl.pallas_call(
        body, name=name, grid=(hv // nh, nc),
        in_specs=[wide, wide, wide, wide, sq, wide, pl.BlockSpec((1, nh, HD, HD), lambda h, c: (rev(c), h, 0, 0)),
                  pl.BlockSpec((nh // 2, 1, 8, CH), lambda h, c: (h, rev(c), 0, 0))],
        out_specs=(wide, wide, wide, wide, sq, pl.BlockSpec((1, 1, nh, HD), lambda h, c: (h, rev(c), 0, 0))),
        out_shape=(wshape(BF16), wshape(BF16), wshape(F32), wshape(F32), jax.ShapeDtypeStruct((hv, lp, CH), F32),
                   jax.ShapeDtypeStruct((hv // nh, nc, nh, HD), F32)),
        scratch_shapes=[pltpu.VMEM((nh, HD, HD), F32)],
        compiler_params=_cp("parallel", "arbitrary"))(do, w, qd, kd, p, vn, states, gates)


def _delta_local_bwd(act, gates, t, dvn, dw, dqd, dkd, dp, key_w, name):
    lp = act.shape[0]
    hk, nc = key_w // HD, lp // CH
    hv = 2 * hk
    g = _chunks_per_step(nc)
    tr = g * CH
    scale = HD ** -0.5

    def body(q_ref, k_ref, v_ref, g_ref, t_ref, dvn_ref, dw_ref, dqd_ref, dkd_ref, dp_ref, dq_ref, dk_ref, dv_ref, dg_ref):
        ri, ci = _iota((CH, CH), 0), _iota((CH, CH), 1)
        inner = lambda x, z: jnp.sum(x * z, axis=1, keepdims=True)
        norms, items = [], []
        for j in range(g):
            rows = slice(j * CH, (j + 1) * CH)
            qh, qr = _l2n(q_ref[rows, :])
            kn, kr = _l2n(k_ref[rows, :])
            qn = qh * scale
            norms.append((rows, qh, qr, kn, kr, qn))
            dg_ref[0, j, 4:8, :] = jnp.zeros((4, CH), F32)
            for e in range(2):
                cols = slice(e * HD, (e + 1) * HD)
                v = v_ref[rows, cols]
                r = _chunk_local(qn, kn, v, g_ref[0, j, e:e + 1, :], g_ref[0, j, 2 + e:3 + e, :])
                items.append(dict(r, j=j, e=e, rows=rows, cols=cols, v=v, kn=kn, qn=qn, t=t_ref[e, rows, :],
                                  dvn=dvn_ref[rows, cols], dw=dw_ref[rows, cols]))
        for it in items:
            it["dt"] = _dot_nt(it["dvn"], it["vb"]) + _dot_nt(it["dw"], it["kbg"])
            it["dvb"] = _dot_tn(it["t"], it["dvn"])
            it["dkbg"] = _dot_tn(it["t"], it["dw"])
        for it in items:
            it["x"] = _dot_tn(it["t"], it["dt"])
        for it in items:
            it["da"] = -jnp.where(it["strict"], _dot_nt(it["x"], it["t"]), 0.0)
        for it in items:
            dp = dp_ref[it["e"], it["rows"], :]
            it["gmat"] = it["da"] * it["a"] + dp * it["p"]
            mm, nn = (it["da"] * it["dec"]).astype(BF16), (dp * it["dec"]).astype(BF16)
            it["dkb"] = _dot(mm, it["kn"]) + it["dkbg"] * it["eg"]
            it["dkn"] = _dot_tn(mm, it["kb"]) + _dot_tn(nn, it["qn"])
            it["dqn"] = _dot(nn, it["kn"])
        for it in items:
            j, e, rows, cols = it["j"], it["e"], it["rows"], it["cols"]
            dqd, dkd, gmat, dkb = dqd_ref[rows, cols], dkd_ref[rows, cols], it["gmat"], it["dkb"]
            it["dkn"] = it["dkn"] + dkd * it["ekd"] + it["b_col"] * dkb
            it["dqn"] = it["dqn"] + dqd * it["eg"]
            dkd_kd = inner(dkd, it["kd"])
            dgam = (jnp.sum(gmat, axis=1, keepdims=True) - _row_to_col(jnp.sum(gmat, axis=0, keepdims=True))
                    + inner(dqd, it["qd"]) + inner(it["dkbg"], it["kbg"]) - dkd_kd)
            dgl = jnp.max(g_ref[0, j, 4 + e:5 + e, :], axis=1, keepdims=True) + jnp.sum(dkd_kd, keepdims=True)
            dgam = dgam + jnp.where(_iota((CH, 1), 0) == CH - 1, dgl, 0.0)
            dg_ref[0, j, 2 + e:3 + e, :] = jnp.sum(jnp.where(ri >= ci, dgam, 0.0), axis=0, keepdims=True)
            dg_ref[0, j, e:e + 1, :] = _col_to_row(inner(dkb, it["kn"]) + inner(it["dvb"], it["v"]))
            dv_ref[rows, cols] = it["b_col"] * it["dvb"]
        for j, (rows, qh, qr, kn, kr, _) in enumerate(norms):
            dqh = (items[2 * j]["dqn"] + items[2 * j + 1]["dqn"]) * scale
            dkn = items[2 * j]["dkn"] + items[2 * j + 1]["dkn"]
            dq_ref[rows, :] = qr * (dqh - qh * jnp.sum(dqh * qh, axis=1, keepdims=True))
            dk_ref[rows, :] = kr * (dkn - kn * jnp.sum(dkn * kn, axis=1, keepdims=True))

    narrow = pl.BlockSpec((tr, HD), lambda h, c: (c, h))
    wide = pl.BlockSpec((tr, 2 * HD), lambda h, c: (c, h))
    sq = pl.BlockSpec((2, tr, CH), lambda h, c: (h, c, 0))
    gate = pl.BlockSpec((1, g, 8, CH), lambda h, c: (h, c, 0, 0))
    return pl.pallas_call(
        body, name=name, grid=(hk, nc // g),
        in_specs=[narrow, pl.BlockSpec((tr, HD), lambda h, c: (c, hk + h)),
                  pl.BlockSpec((tr, 2 * HD), lambda h, c: (c, hk + h)), gate, sq, wide, wide, wide, wide, sq],
        out_specs=(narrow, narrow, wide, gate),
        out_shape=(jax.ShapeDtypeStruct((lp, key_w), F32), jax.ShapeDtypeStruct((lp, key_w), F32),
                   jax.ShapeDtypeStruct((lp, hv * HD), F32), jax.ShapeDtypeStruct((hk, nc, 8, CH), F32)),
        compiler_params=_cp("parallel", "parallel"))(act, act, act, gates, t, dvn, dw, dqd, dkd, dp)


def _head_group(nheads):
    return 4 if nheads % 4 == 0 else 1


def _outnorm_fwd(o, proj, z_col0, w, name):
    lp, val = o.shape
    hg = _head_group(val // HD)
    bw = hg * HD
    tm, zb = _tile(lp, 1056), z_col0 // bw

    def body(o_ref, z_ref, w_ref, y_ref):
        for j in range(hg):
            cols = slice(j * HD, (j + 1) * HD)
            xf = o_ref[:, cols]
            r = lax.rsqrt(jnp.mean(xf * xf, axis=-1, keepdims=True) + EPS)
            y_ref[:, cols] = (xf * r * w_ref[...] * _silu(z_ref[:, cols])).astype(BF16)

    return pl.pallas_call(
        body, name=name, grid=(lp // tm, val // bw),
        in_specs=[pl.BlockSpec((tm, bw), lambda i, h: (i, h)), pl.BlockSpec((tm, bw), lambda i, h: (i, zb + h)),
                  pl.BlockSpec((1, HD), lambda i, h: (0, 0))],
        out_specs=pl.BlockSpec((tm, bw), lambda i, h: (i, h)), out_shape=jax.ShapeDtypeStruct((lp, val), BF16),
        compiler_params=_cp("parallel", "parallel"))(o, proj, w)


def _outnorm_bwd(o, proj, z_col0, w, dy, name):
    lp, val = o.shape
    hg = _head_group(val // HD)
    bw = hg * HD
    tm, zb = _tile(lp, 1056), z_col0 // bw

    def body(o_ref, z_ref, w_ref, dy_ref, do_ref, dz_ref, dw_ref):
        @pl.when((pl.program_id(0) == 0) & (pl.program_id(1) == 0))
        def _():
            dw_ref[...] = jnp.zeros_like(dw_ref)

        for j in range(hg):
            cols = slice(j * HD, (j + 1) * HD)
            xf, z, d = o_ref[:, cols], z_ref[:, cols], dy_ref[:, cols]
            r = lax.rsqrt(jnp.mean(xf * xf, axis=-1, keepdims=True) + EPS)
            xhat = xf * r
            dn = d * _silu(z)
            dz_ref[:, cols] = (d * xhat * w_ref[...] * _dsilu(z)).astype(BF16)
            dxhat = dn * w_ref[...]
            do_ref[:, cols] = r * (dxhat - xhat * jnp.mean(dxhat * xhat, axis=-1, keepdims=True))
            dw_ref[...] += jnp.sum(dn * xhat, axis=0, keepdims=True)

    blk = pl.BlockSpec((tm, bw), lambda i, h: (i, h))
    vec = pl.BlockSpec((1, HD), lambda i, h: (0, 0))
    return pl.pallas_call(
        body, name=name, grid=(lp // tm, val // bw),
        in_specs=[blk, pl.BlockSpec((tm, bw), lambda i, h: (i, zb + h)), vec, blk], out_specs=(blk, blk, vec),
        out_shape=(jax.ShapeDtypeStruct((lp, val), F32), jax.ShapeDtypeStruct((lp, val), BF16),
                   jax.ShapeDtypeStruct((1, HD), F32)),
        compiler_params=_cp("arbitrary", "arbitrary"))(o, proj, w, dy)


def _qknorm_fwd(proj, qw, kw, width, name):
    lp = proj.shape[0]
    hg = _head_group(width // HD)
    bw = hg * HD
    tm, nh = _tile(lp, 1056), width // bw

    def body(q_ref, k_ref, v_ref, qw_ref, kw_ref, qo_ref, ko_ref, vo_ref):
        for x_ref, w_ref, o_ref in ((q_ref, qw_ref, qo_ref), (k_ref, kw_ref, ko_ref)):
            for j in range(hg):
                cols = slice(j * HD, (j + 1) * HD)
                xf = x_ref[:, cols]
                r = lax.rsqrt(jnp.mean(xf * xf, axis=-1, keepdims=True) + EPS)
                o_ref[:, cols] = (xf * r * w_ref[...]).astype(BF16)
        vo_ref[...] = v_ref[...].astype(BF16)

    blk = lambda off: pl.BlockSpec((tm, bw), lambda i, h: (i, off + h))
    vec = pl.BlockSpec((1, HD), lambda i, h: (0, 0))
    shp = jax.ShapeDtypeStruct((lp, width), BF16)
    return pl.pallas_call(
        body, name=name, grid=(lp // tm, nh), in_specs=[blk(0), blk(nh), blk(2 * nh), vec, vec],
        out_specs=(blk(0), blk(0), blk(0)), out_shape=(shp, shp, shp),
        compiler_params=_cp("parallel", "parallel"))(proj, proj, proj, qw, kw)


def _qknorm_bwd(proj, qw, kw, dqn, dkn, width, name):
    lp = proj.shape[0]
    hg = _head_group(width // HD)
    bw = hg * HD
    tm, nh = _tile(lp, 1056), width // bw

    def body(q_ref, k_ref, qw_ref, kw_ref, dqn_ref, dkn_ref, dq_ref, dk_ref, dqw_ref, dkw_ref):
        first = (pl.program_id(0) == 0) & (pl.program_id(1) == 0)
        for x_ref, w_ref, dy_ref, dx_ref, dw_ref in ((q_ref, qw_ref, dqn_ref, dq_ref, dqw_ref),
                                                       (k_ref, kw_ref, dkn_ref, dk_ref, dkw_ref)):
            @pl.when(first)
            def _():
                dw_ref[...] = jnp.zeros_like(dw_ref)

            for j in range(hg):
                cols = slice(j * HD, (j + 1) * HD)
                xf, dy = x_ref[:, cols], dy_ref[:, cols]
                r = lax.rsqrt(jnp.mean(xf * xf, axis=-1, keepdims=True) + EPS)
                xhat = xf * r
                dxhat = dy * w_ref[...]
                dx_ref[:, cols] = (r * (dxhat - xhat * jnp.mean(dxhat * xhat, axis=-1, keepdims=True))).astype(BF16)
                dw_ref[...] += jnp.sum(dy * xhat, axis=0, keepdims=True)

    blk = lambda off: pl.BlockSpec((tm, bw), lambda i, h: (i, off + h))
    vec = pl.BlockSpec((1, HD), lambda i, h: (0, 0))
    shp = jax.ShapeDtypeStruct((lp, width), BF16)
    vshp = jax.ShapeDtypeStruct((1, HD), F32)
    return pl.pallas_call(
        body, name=name, grid=(lp // tm, nh), in_specs=[blk(0), blk(nh), vec, vec, blk(0), blk(0)],
        out_specs=(blk(0), blk(0), vec, vec), out_shape=(shp, shp, vshp, vshp),
        compiler_params=_cp("arbitrary", "arbitrary"))(proj, proj, qw, kw, dqn, dkn)


def _sb_tq(lp):
    return 3 * QB if lp % (3 * QB) == 0 else QB


def _add_rows(x, r0, delta):
    return x + delta if r0 == 0 else jnp.concatenate([x[:r0], x[r0:] + delta], axis=0)


def _sb_rows(kb):
    return pl.ds(kb * QB if isinstance(kb, int) else pl.multiple_of(kb * QB, QB), QB)


def _sb_scores(qk, t_idx, kb, masked):
    z = qk * (HD ** -0.5)
    sp = jnp.log(1.0 + jnp.exp(-jnp.abs(z)))
    lsz = jnp.minimum(z, 0.0) - sp
    lk = -jnp.maximum(z, 0.0) - sp
    if not masked:
        return None, lsz, lk
    s_idx = kb * QB + _iota((1, QB), 1)
    valid = (s_idx < t_idx) & (s_idx >= INERT)
    return valid, lsz, jnp.where(valid, lk, 0.0)


def _sb_fwd(qn, kn, vv, proj, gate_col0, name):
    lp, width = qn.shape
    tq = _sb_tq(lp)
    nh, nq, gb, nsub = width // HD, lp // tq, gate_col0 // HD, tq // QB

    def body(q_ref, k_ref, v_ref, g_ref, o_ref, og_ref, tot_ref):
        qb = pl.program_id(1)
        q = q_ref[...]
        t_idx = qb * tq + _iota((tq, 1), 0)
        upper = (_iota((QB, QB), 0) > _iota((QB, QB), 1)).astype(BF16)

        def step(kg, carry, masked):
            run, acc = carry
            kbs = [kg * nsub + sub for sub in reversed(range(nsub))]
            rows = [_sb_rows(kb) for kb in kbs]
            qks = [_dot_nt(q, k_ref[r, :]) for r in rows]
            scores = [_sb_scores(qk, t_idx, kb, masked) for qk, kb in zip(qks, kbs)]
            sums = [_dot_split(lk, upper) for _, _, lk in scores]
            probs = []
            for (valid, lsz, lk), part in zip(scores, sums):
                a = jnp.exp(lsz + part + run)
                probs.append((jnp.where(valid, a, 0.0) if masked else a).astype(BF16))
                run = run + jnp.sum(lk, axis=1, keepdims=True)
            for a, r in zip(probs, rows):
                acc = acc + _dot(a, v_ref[r, :])
            return run, acc

        carry = step(qb, (jnp.zeros((tq, 1), F32), jnp.zeros((tq, HD), F32)), True)
        carry = lax.fori_loop(1, qb, lambda i, cr: step(qb - i, cr, False), carry)
        run, acc = lax.fori_loop(0, jnp.minimum(qb, 1), lambda _, cr: step(0, cr, True), carry)
        o_ref[...] = acc
        og_ref[...] = (acc * _silu(g_ref[...])).astype(BF16)
        tot_ref[0, 0] = _col_to_row(run)

    full = pl.BlockSpec((lp, HD), lambda h, i: (0, h))
    blk = pl.BlockSpec((tq, HD), lambda h, i: (i, h))
    return pl.pallas_call(
        body, name=name, grid=(nh, nq),
        in_specs=[blk, full, full, pl.BlockSpec((tq, HD), lambda h, i: (i, gb + h))],
        out_specs=(blk, blk, pl.BlockSpec((1, 1, 1, tq), lambda h, i: (h, i, 0, 0))),
        out_shape=(jax.ShapeDtypeStruct((lp, width), F32), jax.ShapeDtypeStruct((lp, width), BF16),
                   jax.ShapeDtypeStruct((nh, nq, 1, tq), F32)),
        compiler_params=_cp("parallel", "parallel"))(qn, kn, vv, proj)


def _sb_bwd(qn, kn, vv, proj, gate_col0, att, tot, dog, name):
    lp, width = qn.shape
    tq = _sb_tq(lp)
    nh, nq, gb, nsub = width // HD, lp // tq, gate_col0 // HD, tq // QB
    scale = HD ** -0.5

    def body(q_ref, k_ref, v_ref, g_ref, att_ref, tot_ref, dog_ref, dq_ref, dk_ref, dv_ref, dg_ref, dk_acc, dv_acc):
        qb = pl.program_id(1)

        @pl.when(qb == 0)
        def _():
            dk_acc[...] = jnp.zeros_like(dk_acc)
            dv_acc[...] = jnp.zeros_like(dv_acc)

        q, gate, dg_out = q_ref[...], g_ref[...], dog_ref[...]
        d_o = (dg_out * _silu(gate)).astype(BF16)
        dg_ref[...] = (dg_out * att_ref[...] * _dsilu(gate)).astype(BF16)
        total = _row_to_col(tot_ref[0, 0])
        t_idx = qb * tq + _iota((tq, 1), 0)
        ri, ci = _iota((QB, QB), 0), _iota((QB, QB), 1)
        lower_incl = (ri <= ci).astype(BF16)
        lower_excl = (ri < ci).astype(BF16)

        def step(kg, carry, masked, diag=False):
            run, erun, dq = carry
            kbs = [kg * nsub + sub for sub in range(nsub)]
            rows = [_sb_rows(kb) for kb in kbs]
            r0s = [sub * QB if diag else 0 for sub in range(nsub)]
            qks = [_dot_nt(q[r0:], k_ref[r, :]) for r, r0 in zip(rows, r0s)]
            dprobs = [_dot_nt(d_o[r0:], v_ref[r, :]) for r, r0 in zip(rows, r0s)]
            scores = [_sb_scores(qk, t_idx[r0:], kb, masked) for qk, kb, r0 in zip(qks, kbs, r0s)]
            sums = [_dot_split(lk, lower_incl) for _, _, lk in scores]
            probs, es = [], []
            for (valid, lsz, lk), part, dprob, r0 in zip(scores, sums, dprobs, r0s):
                a = jnp.exp(lsz + (total[r0:] - run[r0:] - part))
                a = jnp.where(valid, a, 0.0) if masked else a
                probs.append(a.astype(BF16))
                es.append(a * dprob)
                run = _add_rows(run, r0, jnp.sum(lk, axis=1, keepdims=True))
            esums = [_dot_split(e, lower_excl) for e in es]
            for a, r, r0 in zip(probs, rows, r0s):
                dv_acc[r, :] += _dot_tn(a, d_o[r0:])
            dzs = []
            for (valid, lsz, _), e, part, r0 in zip(scores, es, esums, r0s):
                sig = jnp.exp(lsz)
                dz = e * (1.0 - sig) - sig * (erun[r0:] + part)
                dz = jnp.where(valid, dz, 0.0) if masked else dz
                dzs.append((dz * scale).astype(BF16))
                erun = _add_rows(erun, r0, jnp.sum(e, axis=1, keepdims=True))
            for dz, r, r0 in zip(dzs, rows, r0s):
                dk_acc[r, :] += _dot_tn(dz, q[r0:])
                dq = _add_rows(dq, r0, _dot(dz, k_ref[r, :]))
            return run, erun, dq

        zero = jnp.zeros((tq, 1), F32)
        carry = step(0, (zero, zero, jnp.zeros((tq, HD), F32)), True)
        carry = lax.fori_loop(1, qb, lambda kg, cr: step(kg, cr, False), carry)
        _, _, dq = lax.fori_loop(0, jnp.minimum(qb, 1), lambda _, cr: step(qb, cr, True, diag=True), carry)
        dq_ref[...] = dq

        @pl.when(qb == nq - 1)
        def _():
            dk_ref[...] = dk_acc[...]
            dv_ref[...] = dv_acc[...].astype(BF16)

    full = pl.BlockSpec((lp, HD), lambda h, i: (0, h))
    blk = pl.BlockSpec((tq, HD), lambda h, i: (i, h))
    return pl.pallas_call(
        body, name=name, grid=(nh, nq),
        in_specs=[blk, full, full, pl.BlockSpec((tq, HD), lambda h, i: (i, gb + h)), blk,
                  pl.BlockSpec((1, 1, 1, tq), lambda h, i: (h, i, 0, 0)), blk],
        out_specs=(blk, full, full, blk),
        out_shape=(jax.ShapeDtypeStruct((lp, width), F32), jax.ShapeDtypeStruct((lp, width), F32),
                   jax.ShapeDtypeStruct((lp, width), BF16), jax.ShapeDtypeStruct((lp, width), BF16)),
        scratch_shapes=[pltpu.VMEM((lp, HD), F32), pltpu.VMEM((lp, HD), F32)],
        compiler_params=_cp("parallel", "arbitrary"))(qn, kn, vv, proj, att, tot, dog)


def _loss_head(h, target, name):
    lp, d = h.shape
    tm = _tile(PAD, 128)
    nt = lp // tm
    npad = PAD // tm

    def body(h_ref, t_ref, l_ref, dh_ref, dhb_ref):
        i = pl.program_id(0)
        err = jnp.where(i >= npad, h_ref[...] - t_ref[...], 0.0)
        dh = err * (1.0 / d)
        dh_ref[...] = dh
        dhb_ref[...] = dh.astype(BF16)

        @pl.when(i == 0)
        def _():
            l_ref[...] = jnp.zeros_like(l_ref)

        l_ref[...] += (0.5 / d) * jnp.sum(err * err, keepdims=True)

    row = pl.BlockSpec((tm, d), lambda i: (i, 0))
    return pl.pallas_call(
        body, name=name, grid=(nt,),
        in_specs=[row, pl.BlockSpec((tm, d), lambda i: (jnp.maximum(i - npad, 0), 0))],
        out_specs=(pl.BlockSpec((1, 1), lambda i: (0, 0)), row, row),
        out_shape=(jax.ShapeDtypeStruct((1, 1), F32), jax.ShapeDtypeStruct((lp, d), F32),
                   jax.ShapeDtypeStruct((lp, d), BF16)),
        compiler_params=_cp("arbitrary"))(h, target)


def _adamw(parts, w, m, v, name, mine=None):
    r, c = w.shape
    tr = _tile(r, max(8, (1 << 18) // c // 8 * 8))
    nparts = parts.shape[0]

    def body(*refs):
        p_ref, w_ref, m_ref, v_ref = refs[:4]
        g_ref, d_ref, mo_ref, vo_ref = refs[-4:]
        g = p_ref[0].astype(F32)
        if mine is not None:
            g = refs[4][...].astype(F32) + g
        for k in range(1, nparts):
            g = g + p_ref[k].astype(F32)
        mn = ADAM_B1 * m_ref[...] + (1.0 - ADAM_B1) * g
        vn = ADAM_B2 * v_ref[...] + (1.0 - ADAM_B2) * jnp.square(g)
        m_hat = mn / (1.0 - ADAM_B1 ** ADAM_STEP)
        v_hat = vn / (1.0 - ADAM_B2 ** ADAM_STEP)
        g_ref[...] = g
        d_ref[...] = -ADAM_LR * (m_hat / (jnp.sqrt(v_hat) + ADAM_EPS) + ADAM_WD * w_ref[...])
        mo_ref[...] = mn
        vo_ref[...] = vn

    blk = pl.BlockSpec((tr, c), lambda i: (i, 0))
    shp = jax.ShapeDtypeStruct((r, c), F32)
    extra = [] if mine is None else [mine]
    return pl.pallas_call(
        body, name=name, grid=(r // tr,),
        in_specs=[pl.BlockSpec((nparts, tr, c), lambda i: (0, i, 0)), blk, blk, blk] + [blk] * len(extra),
        out_specs=(blk, blk, blk, blk), out_shape=(shp, shp, shp, shp), compiler_params=_cp("parallel"))(parts, w, m, v, *extra)


def _unshard_cols(g):
    return jnp.transpose(g, (1, 0, 2)).reshape(g.shape[1], NDEV * g.shape[2])


def _shard_cols(a):
    r, c = a.shape
    return jnp.transpose(a.reshape(r, NDEV, c // NDEV), (1, 0, 2))


def kernel(x, meta_tokens, dn_norm_w, dn_w_in, dn_conv_w, dn_a_log, dn_dt_bias, dn_out_norm_w, dn_w_out, sb_norm_w, sb_w_in, sb_q_norm_w, sb_k_norm_w, sb_w_out, loss_target, m_meta_tokens, m_dn_norm_w, m_dn_w_in, m_dn_conv_w, m_dn_a_log, m_dn_dt_bias, m_dn_out_norm_w, m_dn_w_out, m_sb_norm_w, m_sb_w_in, m_sb_q_norm_w, m_sb_k_norm_w, m_sb_w_out, v_meta_tokens, v_dn_norm_w, v_dn_w_in, v_dn_conv_w, v_dn_a_log, v_dn_dt_bias, v_dn_out_norm_w, v_dn_w_out, v_sb_norm_w, v_sb_w_in, v_sb_q_norm_w, v_sb_k_norm_w, v_sb_w_out):
    seq, d = x.shape[1], x.shape[2]
    lp = PAD + seq
    key_w = d
    val_w = 2 * d
    hv = val_w // HD
    conv_w_cols = 2 * key_w + val_w
    main_w = conv_w_cols + val_w
    sb_w = d
    nc = lp // CH
    hk = key_w // HD

    (g_meta, g_sbn, g_conv) = _exchange([meta_tokens, sb_norm_w, dn_conv_w[0]], True, "gather_vectors")
    w_in0_mine = dn_w_in[0].astype(BF16)
    st_a, tok_a = _split_start([w_in0_mine], [lax.empty((NDEV,) + w_in0_mine.shape, BF16)], _plan_gather_chips, 4,
                               "gather_w_in0_chips_start", after=g_meta)
    meta = _unshard_cols(g_meta) + tok_a[:1, :1]
    sbn_w = _unshard_cols(g_sbn)
    conv_w = _unshard_cols(g_conv)
    h0 = jnp.concatenate([jnp.zeros((INERT, d), F32), meta, x[0]], axis=0)
    hn0 = _rms_fwd(h0, dn_norm_w, "dn_norm")
    lands_a, srcs_a = _split_wait(st_a, hn0, "gather_w_in0_chips_wait")
    st_a2, tok_a2 = _split_start([], lands_a, _plan_gather_forward, 3, "gather_w_in0_forward_start")
    st_b, tok_b = _exchange_start([dn_w_out[0].astype(BF16), sb_w_in[0].astype(BF16), sb_w_out[0].astype(BF16)],
                                  True, "gather_w_rest_start", after=tok_a2)
    lands_a, _ = _split_wait(st_a2, tok_b, "gather_w_in0_forward_wait")
    w_in0 = _unshard_cols(_with_own(lands_a[0], srcs_a[0]))
    main_cols, gate_cols = (0, main_w), (main_w, 2 * hv)

    proj0 = _matmul(hn0, w_in0, mode="nn", out_dtype=F32, tm=1056, tn=512, tk=4096, name="dn_in_proj", b_cols=main_cols)
    gl0 = _matmul(hn0, w_in0, mode="nn", out_dtype=F32, tm=1056, tn=512, tk=4096, name="dn_gate_proj", b_cols=gate_cols)
    act0 = _conv_fwd(proj0, conv_w, conv_w_cols, "dn_conv")
    a_log2 = jnp.concatenate([jnp.zeros_like(dn_a_log), dn_a_log], axis=1)
    dt_bias2 = jnp.concatenate([jnp.zeros_like(dn_dt_bias), dn_dt_bias], axis=1)
    bg = _gates_fwd(gl0, a_log2, dt_bias2, "dn_gates")
    pack = lambda t: jnp.transpose(t.reshape(nc, CH, hk, 2), (2, 0, 3, 1))
    gates = jnp.concatenate([pack(bg[:, :hv]), pack(bg[:, hv:]), jnp.zeros((hk, nc, 4, CH), F32)], axis=2)
    u0, w0, qd0, kd0, p0, t0 = _delta_local(act0, gates, key_w, "dn_delta_local")
    o0, vn0, states = _delta_scan(u0, w0, qd0, kd0, p0, gates, "dn_delta_scan")
    o0g = _outnorm_fwd(o0, proj0, conv_w_cols, dn_out_norm_w, "dn_out_norm")
    lands_b, srcs_b = _exchange_wait(st_b, o0g, "gather_w_rest_wait")
    g_out0, g_in1, g_out1 = [_with_own(l, s) for l, s in zip(lands_b, srcs_b)]
    w_out0 = g_out0.reshape(val_w, d)
    w_out1 = g_out1.reshape(sb_w, d)
    h1 = _matmul(o0g, w_out0, mode="nn", out_dtype=F32, tm=1056, tn=512, tk=4096, name="dn_out_proj", add=h0)
    hn1 = _rms_fwd(h1, sbn_w, "sb_norm")
    proj1 = _matmul(hn1, g_in1, mode="nn", out_dtype=F32, tm=1056, tn=512, tk=4096, name="sb_in_proj", sharded=True)
    qn1, kn1, vv1 = _qknorm_fwd(proj1, sb_q_norm_w, sb_k_norm_w, sb_w, "sb_qk_norm")
    att1, o1g, tot1 = _sb_fwd(qn1, kn1, vv1, proj1, 3 * sb_w, "sb_attn")
    h2 = _matmul(o1g, w_out1, mode="nn", out_dtype=F32, tm=1056, tn=512, tk=4096, name="sb_out_proj", add=h1)
    loss_part, dh2, dh2b = _loss_head(h2, loss_target[0], "loss_head")
    loss = lax.psum(loss_part[0, 0], ("x", "y", "c"))

    p_out1 = _matmul(o1g, dh2b, mode="tn", out_dtype=BF16, tm=1024, tn=512, tk=lp, name="sb_out_wgrad")
    do1g = _matmul(dh2b, w_out1, mode="nt", out_dtype=F32, tm=1056, tn=512, tk=4096, name="sb_out_dgrad")
    dqn1, dkn1, dv1, dgate1 = _sb_bwd(qn1, kn1, vv1, proj1, 3 * sb_w, att1, tot1, do1g, "sb_attn_bwd")
    dq1, dk1, d_qw, d_kw = _qknorm_bwd(proj1, sb_q_norm_w, sb_k_norm_w, dqn1, dkn1, sb_w, "sb_qk_norm_bwd")
    dproj1 = jnp.concatenate([dq1, dk1, dv1, dgate1], axis=1)
    p_in1 = _matmul(hn1, dproj1, mode="tn", out_dtype=BF16, tm=1024, tn=512, tk=lp, name="sb_in_wgrad", sharded=True)
    st_s1, tok_s1 = _exchange_start([p_out1.reshape(NDEV, sb_w // NDEV, d), p_in1], False, "scatter_sb_start")
    dhn1 = _matmul(dproj1, g_in1, mode="nt", out_dtype=F32, tm=1056, tn=512, tk=4096, name="sb_in_dgrad", after=tok_s1,
                   sharded=True)
    dh1, dh1b, d_sbn = _rms_bwd(h1, sbn_w, dhn1, dh2, "sb_norm_bwd")

    p_out0 = _matmul(o0g, dh1b, mode="tn", out_dtype=BF16, tm=1024, tn=512, tk=lp, name="dn_out_wgrad")
    st_s2, tok_s2 = _exchange_start([p_out0.reshape(NDEV, val_w // NDEV, d)], False, "scatter_dn_out_start")
    do0g = _matmul(dh1b, w_out0, mode="nt", out_dtype=F32, tm=1056, tn=512, tk=4096, name="dn_out_dgrad", after=tok_s2)
    do0, dz0, d_onw = _outnorm_bwd(o0, proj0, conv_w_cols, dn_out_norm_w, do0g, "dn_out_norm_bwd")
    dvn0, dw0, dqd0, dkd0, dp0, sd0 = _delta_scan_bwd(do0, w0, qd0, kd0, p0, vn0, states, gates, "dn_delta_scan_bwd")
    sd_rows = jnp.transpose(jnp.transpose(sd0[..., 0], (0, 2, 1)).reshape(hk, 2, nc), (0, 2, 1))
    gates_b = jnp.concatenate([gates[:, :, :4], jnp.broadcast_to(sd_rows[..., None], (hk, nc, 2, CH)),
                               jnp.zeros((hk, nc, 2, CH), F32)], axis=2)
    dq_act, dk_act, dv_act, dgates = _delta_local_bwd(act0, gates_b, t0, dvn0, dw0, dqd0, dkd0, dp0, key_w,
                                                      "dn_delta_local_bwd")
    unpack = lambda t: jnp.transpose(t, (1, 3, 0, 2)).reshape(lp, hv)
    dbg = jnp.concatenate([unpack(dgates[:, :, 0:2]), unpack(dgates[:, :, 2:4])], axis=1)
    dgl0, d_alog2, d_dtb2 = _gates_bwd(gl0, a_log2, dt_bias2, dbg, "dn_gates_bwd")
    d_alog, d_dtb = d_alog2[:, hv:], d_dtb2[:, hv:]
    dxq, dcw_q = _conv_bwd(proj0, 0, conv_w[:, :key_w], dq_act, "dn_conv_bwd_q")
    dxk, dcw_k = _conv_bwd(proj0, key_w, conv_w[:, key_w:2 * key_w], dk_act, "dn_conv_bwd_k")
    dxv, dcw_v = _conv_bwd(proj0, 2 * key_w, conv_w[:, 2 * key_w:], dv_act, "dn_conv_bwd_v")
    dproj0 = jnp.concatenate([dxq, dxk, dxv, dz0], axis=1)
    p_in0_main = _matmul(hn0, dproj0, mode="tn", out_dtype=BF16, tm=1024, tn=512, tk=lp, name="dn_in_wgrad")
    p_in0_gate = _matmul(hn0, dgl0, mode="tn", out_dtype=BF16, tm=1024, tn=512, tk=lp, name="dn_gate_wgrad")
    p_in0 = _shard_cols(jnp.concatenate([p_in0_main, p_in0_gate], axis=1))
    st_c, tok_c = _split_start([p_in0], [lax.empty((4,) + p_in0.shape[1:], BF16)], _plan_scatter_core, 4,
                               "scatter_dn_in_core_start")
    dhn0 = _matmul(dgl0, w_in0, mode="nt", out_dtype=F32, tm=1056, tn=512, tk=4096, name="dn_gate_dgrad", after=tok_c,
                   b_cols=gate_cols)
    (zone_c,), (p_in0,) = _split_wait(st_c, dhn0, "scatter_dn_in_core_wait")
    sums_in0 = _chip_sums(p_in0, zone_c, "dn_in_chip_sums")
    st_s3, tok_s3 = _split_start([sums_in0], [lax.empty((3,) + p_in0.shape[1:], BF16)], _plan_scatter_chips, 3,
                                 "scatter_dn_in_chips_start")
    dhn0 = _matmul(dproj0, w_in0, mode="nt", out_dtype=F32, tm=1056, tn=512, tk=4096, name="dn_in_dgrad", add=dhn0,
                   after=tok_s3, b_cols=main_cols)
    dh0, _, d_dnn = _rms_bwd(h0, dn_norm_w, dhn0, dh1, "dn_norm_bwd")
    grad_x = dh0[PAD:][None]

    p_conv = _shard_cols(jnp.concatenate([dcw_q, dcw_k, dcw_v], axis=1))
    (r_meta, r_sbn, r_conv) = _exchange([_shard_cols(dh0[INERT:PAD]), _shard_cols(d_sbn), p_conv], False, "scatter_vector_grads")
    small = jnp.concatenate([d_dnn, d_alog, d_dtb, d_onw, d_qw, d_kw], axis=1)
    (r_small,) = _exchange([small], True, "gather_replicated_grads")
    outs = {}
    outs["meta_tokens"] = _adamw(r_meta, meta_tokens, m_meta_tokens, v_meta_tokens, "adamw_meta")
    outs["dn_conv_w"] = _adamw(r_conv, dn_conv_w[0], m_dn_conv_w[0], v_dn_conv_w[0], "adamw_dn_conv")
    outs["sb_norm_w"] = _adamw(r_sbn, sb_norm_w, m_sb_norm_w, v_sb_norm_w, "adamw_sb_norm")
    me = _own_index()
    own = lambda src: lax.dynamic_index_in_dim(src, me, 0, keepdims=False)
    (r_out1, r_in1), (s_out1, s_in1) = _exchange_wait(st_s1, r_small, "scatter_sb_wait")
    outs["sb_w_in"] = _adamw(r_in1, sb_w_in[0], m_sb_w_in[0], v_sb_w_in[0], "adamw_sb_w_in", mine=own(s_in1))
    outs["sb_w_out"] = _adamw(r_out1, sb_w_out[0], m_sb_w_out[0], v_sb_w_out[0], "adamw_sb_w_out", mine=own(s_out1))
    (r_out0,), (s_out0,) = _exchange_wait(st_s2, outs["sb_w_in"][1], "scatter_dn_out_wait")
    outs["dn_w_out"] = _adamw(r_out0, dn_w_out[0], m_dn_w_out[0], v_dn_w_out[0], "adamw_dn_w_out", mine=own(s_out0))
    cat = lambda *a: jnp.concatenate(a, axis=1)
    rep = _adamw(r_small, cat(dn_norm_w, dn_a_log, dn_dt_bias, dn_out_norm_w, sb_q_norm_w, sb_k_norm_w),
                 cat(m_dn_norm_w, m_dn_a_log, m_dn_dt_bias, m_dn_out_norm_w, m_sb_q_norm_w, m_sb_k_norm_w),
                 cat(v_dn_norm_w, v_dn_a_log, v_dn_dt_bias, v_dn_out_norm_w, v_sb_q_norm_w, v_sb_k_norm_w),
                 "adamw_replicated")
    off = 0
    for nm, wd in (("dn_norm_w", d), ("dn_a_log", hv), ("dn_dt_bias", hv), ("dn_out_norm_w", HD),
                   ("sb_q_norm_w", HD), ("sb_k_norm_w", HD)):
        outs[nm] = tuple(t[:, off:off + wd] for t in rep)
        off += wd
    behind = jnp.broadcast_to(outs["dn_w_out"][1][0, 0] + rep[1][0, 0] + outs["sb_w_out"][1][0, 0] + outs["meta_tokens"][1][0, 0]
                              + outs["dn_conv_w"][1][0, 0] + outs["sb_norm_w"][1][0, 0], (8, 128))
    (r_in0,), (sums_in0,) = _split_wait(st_s3, behind, "scatter_dn_in_chips_wait")
    outs["dn_w_in"] = _adamw(r_in0, dn_w_in[0], m_dn_w_in[0], v_dn_w_in[0], "adamw_dn_w_in", mine=sums_in0[0])
    lead =("dn_w_in", "dn_conv_w", "dn_w_out", "sb_w_in", "sb_w_out")
    order = ("meta_tokens", "dn_norm_w", "dn_w_in", "dn_conv_w", "dn_a_log", "dn_dt_bias", "dn_out_norm_w", "dn_w_out",
             "sb_norm_w", "sb_w_in", "sb_q_norm_w", "sb_k_norm_w", "sb_w_out")
    fix = lambda nm, t: t[None] if nm in lead else t
    result = [loss, grad_x]
    for kind in range(4):
        result += [fix(nm, outs[nm][kind]) for nm in order]
    return tuple(result)
```

```python
import functools

import jax
import jax.numpy as jnp
from jax import lax
from jax.experimental import pallas as pl
from jax.experimental.pallas import tpu as pltpu

F32 = jnp.float32
BF16 = jnp.bfloat16
HD = 128
CH = 64
QB = 128
N_META = 16
PAD = 128
INERT = PAD - N_META
NDEV = 8
CONV_K = 4
EPS = 1e-6
VMEM_LIMIT = 56 * 1024 * 1024

ADAM_LR, ADAM_B1, ADAM_B2, ADAM_EPS, ADAM_WD, ADAM_STEP = 0.001, 0.9, 0.999, 1e-08, 0.01, 10
MESH = pl.DeviceIdType.MESH


def _cp(*sem):
    return pltpu.CompilerParams(dimension_semantics=sem, vmem_limit_bytes=VMEM_LIMIT)


def _tile(n, pref, mult=8):
    if n <= pref:
        return n
    for t in range(pref - pref % mult, 0, -mult):
        if n % t == 0:
            return t
    return n


def _silu(x):
    return x * jax.nn.sigmoid(x)


def _dsilu(x):
    s = jax.nn.sigmoid(x)
    return s * (1.0 + x * (1.0 - s))


def _dot(a, b, dims=((1,), (0,))):
    return lax.dot_general(a.astype(BF16), b.astype(BF16), (dims, ((), ())), preferred_element_type=F32)


def _dot_nt(a, b):
    return _dot(a, b, ((1,), (1,)))


def _dot_tn(a, b):
    return _dot(a, b, ((0,), (0,)))


def _dot_f32(a, b):
    dn = (((1,), (0,)), ((), ()))
    ah, bh = a.astype(BF16), b.astype(BF16)
    al, bl = (a - ah.astype(F32)).astype(BF16), (b - bh.astype(F32)).astype(BF16)
    mm = lambda x, y: lax.dot_general(x, y, dn, preferred_element_type=F32)
    return mm(ah, bh) + (mm(ah, bl) + mm(al, bh))


def _dot_split(a, m):
    hi = a.astype(BF16)
    lo = (a - hi.astype(F32)).astype(BF16)
    dn = (((1,), (0,)), ((), ()))
    return (lax.dot_general(hi, m, dn, preferred_element_type=F32)
            + lax.dot_general(lo, m, dn, preferred_element_type=F32))


def _iota(shape, dim):
    return lax.broadcasted_iota(jnp.int32, shape, dim)


def _col_to_row(col):
    n = col.shape[0]
    eye = _iota((n, n), 0) == _iota((n, n), 1)
    return jnp.sum(jnp.where(eye, col, 0.0), axis=0, keepdims=True)


def _row_to_col(row):
    n = row.shape[1]
    eye = _iota((n, n), 0) == _iota((n, n), 1)
    return jnp.sum(jnp.where(eye, row, 0.0), axis=1, keepdims=True)


def _exchange(arrs, gather, name):
    n = len(arrs)

    def body(*refs):
        ins, outs = refs[:n], refs[n:2 * n]
        send_sems, recv_sems, local_sems = refs[2 * n:]
        x, y, c = lax.axis_index("x"), lax.axis_index("y"), lax.axis_index("c")
        me = 4 * x + 2 * y + c
        sends = []
        for i in range(n):
            mine = pltpu.make_async_copy(ins[i] if gather else ins[i].at[me], outs[i].at[me], local_sems.at[i])
            mine.start()
            sends.append(mine)
        for k in range(1, NDEV):
            px, py, pc = x ^ (k >> 2), y ^ ((k >> 1) & 1), c ^ (k & 1)
            peer = 4 * px + 2 * py + pc
            for i in range(n):
                cp = pltpu.make_async_remote_copy(
                    src_ref=ins[i] if gather else ins[i].at[peer], dst_ref=outs[i].at[me],
                    send_sem=send_sems.at[i * NDEV + k], recv_sem=recv_sems.at[i * NDEV + k],
                    device_id=(px, py, pc), device_id_type=MESH)
                cp.start()
                sends.append(cp)
        for k in range(1, NDEV):
            px, py, pc = x ^ (k >> 2), y ^ ((k >> 1) & 1), c ^ (k & 1)
            peer = 4 * px + 2 * py + pc
            for i in range(n):
                pltpu.make_async_remote_copy(
                    src_ref=outs[i].at[peer], dst_ref=outs[i].at[peer],
                    send_sem=send_sems.at[i * NDEV + k], recv_sem=recv_sems.at[i * NDEV + k],
                    device_id=(px, py, pc), device_id_type=MESH).wait_recv()
        for i in range(n):
            sends[i].wait()
        for cp in sends[n:]:
            cp.wait_send()

    hbm = pl.BlockSpec(memory_space=pltpu.HBM)
    out_shape = tuple(jax.ShapeDtypeStruct(((NDEV,) + a.shape) if gather else a.shape, a.dtype) for a in arrs)
    return pl.pallas_call(
        body, name=name, out_shape=out_shape, in_specs=[hbm] * n, out_specs=tuple([hbm] * n),
        scratch_shapes=[pltpu.SemaphoreType.DMA((n * NDEV,)), pltpu.SemaphoreType.DMA((n * NDEV,)),
                        pltpu.SemaphoreType.DMA((n,))],
        compiler_params=pltpu.CompilerParams(has_side_effects=True),
    )(*arrs)


_HBM_SPEC = pl.BlockSpec(memory_space=pltpu.HBM)
_SEM_SPEC = pl.BlockSpec(memory_space=pltpu.SEMAPHORE)
_EFFECT = pltpu.SideEffectType.DATAFLOW_SIDE_EFFECTING


def _place():
    return lax.axis_index("x"), lax.axis_index("y"), lax.axis_index("c")


def _dev(px, py, pc):
    return 4 * px + 2 * py + pc


def _plan_direct(n, gather):
    def plan(ins, lnd, for_wait):
        x, y, c = _place()
        copies = []
        for k in range(1, NDEV):
            px, py, pc = x ^ (k >> 2), y ^ ((k >> 1) & 1), c ^ (k & 1)
            for i in range(n):
                slot = (_dev(px, py, pc) if for_wait else _dev(x, y, c)) if gather else k - 1
                copies.append((ins[i] if gather else ins[i].at[_dev(px, py, pc)], lnd[i].at[slot], (px, py, pc)))
        return copies
    return plan


def _plan_gather_chips(ins, lnd, for_wait):
    x, y, c = _place()
    copies = []
    for k in range(4):
        px, py, pc = (x, y, 1 - c) if k == 0 else (x ^ (k >> 1), y ^ (k & 1), c)
        copies.append((ins[0], lnd[0].at[_dev(px, py, pc) if for_wait else _dev(x, y, c)], (px, py, pc)))
    return copies


def _plan_gather_forward(ins, lnd, for_wait):
    x, y, c = _place()
    copies = []
    for k in range(1, 4):
        px, py = x ^ (k >> 1), y ^ (k & 1)
        copies.append((lnd[0].at[_dev(px, py, c)], lnd[0].at[_dev(px, py, 1 - c if for_wait else c)], (x, y, 1 - c)))
    return copies


def _plan_scatter_core(ins, lnd, for_wait):
    x, y, c = _place()
    return [(ins[0].at[_dev(x ^ (k >> 1), y ^ (k & 1), 1 - c)], lnd[0].at[k], (x, y, 1 - c)) for k in range(4)]


def _plan_scatter_chips(ins, lnd, for_wait):
    x, y, c = _place()
    return [(ins[0].at[k], lnd[0].at[k - 1], (x ^ (k >> 1), y ^ (k & 1), c)) for k in range(1, 4)]


def _plan_descriptors(plan, ins, lnd, send_sems, recv_sems, for_wait):
    return [pltpu.make_async_remote_copy(src_ref=src, dst_ref=dst, send_sem=send_sems.at[j], recv_sem=recv_sems.at[j],
                                         device_id=dev, device_id_type=MESH)
            for j, (src, dst, dev) in enumerate(plan(ins, lnd, for_wait))]


def _split_start(srcs, lands, plan, ncopies, name, after=None):
    ns, nl = len(srcs), len(lands)
    extra = [] if after is None else [after]

    def body(*refs):
        ins, lnd = refs[:ns], refs[ns:ns + nl]
        send_sems, recv_sems = refs[ns + nl + len(extra)], refs[ns + nl + len(extra) + 1]
        token = refs[-1]
        for cp in _plan_descriptors(plan, ins, lnd, send_sems, recv_sems, False):
            cp.start()
        token[...] = jnp.zeros_like(token)

    sems = pltpu.SemaphoreType.DMA((ncopies,))
    both = list(srcs) + list(lands)
    outs = pl.pallas_call(
        body, name=name,
        out_shape=(sems, sems, *[pltpu.HBM(a.shape, a.dtype) for a in both], jax.ShapeDtypeStruct((8, 128), F32)),
        in_specs=[_HBM_SPEC] * (ns + nl) + [pl.BlockSpec(memory_space=pl.ANY)] * len(extra),
        out_specs=(_SEM_SPEC, _SEM_SPEC, *[_HBM_SPEC] * (ns + nl), pl.BlockSpec(memory_space=pltpu.VMEM)),
        input_output_aliases={i: 2 + i for i in range(ns + nl)},
        compiler_params=pltpu.CompilerParams(has_side_effects=_EFFECT),
    )(*[pltpu.with_memory_space_constraint(a, pltpu.HBM) for a in both], *extra)
    return (outs[0], outs[1], list(outs[2:2 + ns]), list(outs[2 + ns:2 + ns + nl]), plan), outs[-1]


def _split_wait(state, after, name):
    send_sems, recv_sems, srcs, lands, plan = state
    ns, nl = len(srcs), len(lands)

    def body(*refs):
        ins, lnd = refs[:ns], refs[ns:ns + nl]
        for cp in _plan_descriptors(plan, ins, lnd, refs[ns + nl], refs[ns + nl + 1], True):
            cp.wait_send()
            cp.wait_recv()

    outs = pl.pallas_call(
        body, name=name,
        out_shape=tuple(pltpu.HBM(a.shape, a.dtype) for a in srcs + lands),
        in_specs=[_HBM_SPEC] * (ns + nl) + [_SEM_SPEC, _SEM_SPEC, pl.BlockSpec(memory_space=pl.ANY)],
        out_specs=tuple([_HBM_SPEC] * (ns + nl)), input_output_aliases={i: i for i in range(ns + nl)},
        compiler_params=pltpu.CompilerParams(has_side_effects=_EFFECT),
    )(*srcs, *lands, send_sems, recv_sems, after)
    return list(outs[ns:]), list(outs[:ns])


def _exchange_start(arrs, gather, name, after=None):
    lands = [lax.empty(((NDEV,) + a.shape) if gather else ((NDEV - 1,) + a.shape[1:]), a.dtype) for a in arrs]
    return _split_start(arrs, lands, _plan_direct(len(arrs), gather), len(arrs) * (NDEV - 1), name, after)


_exchange_wait = _split_wait


def _chip_sums(parts, zone, name):
    _, r, c = parts.shape
    tr = _tile(r, max(16, (1 << 19) // c // 16 * 16), 16)
    x, y, core = _place()
    mine = jnp.stack([lax.dynamic_index_in_dim(parts, _dev(x ^ (k >> 1), y ^ (k & 1), core), 0, keepdims=False)
                      for k in range(4)])

    def body(p_ref, z_ref, o_ref):
        o_ref[...] = (p_ref[...].astype(F32) + z_ref[...].astype(F32)).astype(o_ref.dtype)

    blk = pl.BlockSpec((1, tr, c), lambda k, i: (k, i, 0))
    return pl.pallas_call(
        body, name=name, grid=(4, r // tr), in_specs=[blk, blk], out_specs=blk,
        out_shape=jax.ShapeDtypeStruct((4, r, c), parts.dtype),
        compiler_params=_cp("parallel", "parallel"))(mine, zone)


def _own_index():
    return 4 * lax.axis_index("x") + 2 * lax.axis_index("y") + lax.axis_index("c")


def _with_own(land, mine):
    return lax.dynamic_update_index_in_dim(land, mine, _own_index(), 0)


def _matmul(a, b, *, mode, out_dtype, tm, tn, tk, name, add=None, after=None, b_cols=None, sharded=False):
    shard_c = None
    if mode == "nn":
        (m, kd), (_, n) = a.shape, b.shape[-2:]
    elif mode == "nt":
        (m, kd), (n, _) = a.shape, b.shape[-2:]
    else:
        (kd, m), (_, n) = a.shape, b.shape
    if sharded:
        shard_c = b.shape[2] if mode in ("nn", "nt") else n // NDEV
        n = NDEV * shard_c if mode == "nn" else n
    if b_cols is not None and mode == "nn":
        n = b_cols[1]
    tm, tn, tk = _tile(m, tm, 16), _tile(shard_c if sharded and mode != "nt" else n, tn, 128), _tile(
        shard_c if sharded and mode == "nt" else kd, tk, 128)
    nk = kd // tk
    jb = kb = 0
    if b_cols is not None:
        assert mode in ("nn", "nt") and b_cols[0] % (tn if mode == "nn" else tk) == 0 and (mode == "nn" or b_cols[1] == kd)
        jb, kb = (b_cols[0] // tn, 0) if mode == "nn" else (0, b_cols[0] // tk)
    a_spec = pl.BlockSpec((tk, tm), lambda i, j, k: (k, i)) if mode == "tn" else pl.BlockSpec((tm, tk), lambda i, j, k: (i, k))
    b_spec = (pl.BlockSpec((tn, tk), lambda i, j, k: (j, kb + k)) if mode == "nt"
              else pl.BlockSpec((tk, tn), lambda i, j, k: (k, jb + j)))
    o_spec = pl.BlockSpec((tm, tn), lambda i, j, k: (i, j))
    out_shape = jax.ShapeDtypeStruct((m, n), out_dtype)
    if sharded:
        assert b_cols is None
        per = shard_c // (tk if mode == "nt" else tn)
        if mode == "nn":
            b_spec = pl.BlockSpec((None, tk, tn), lambda i, j, k: (j // per, k, j % per))
        elif mode == "nt":
            b_spec = pl.BlockSpec((None, tn, tk), lambda i, j, k: (k // per, j, k % per))
        else:
            o_spec = pl.BlockSpec((None, tm, tn), lambda i, j, k: (j // per, i, j % per))
            out_shape = jax.ShapeDtypeStruct((NDEV, m, shard_c), out_dtype)
    dims = {"nn": ((1,), (0,)), "nt": ((1,), (1,)), "tn": ((0,), (0,))}[mode]

    def body(*refs, nk):
        a_ref, b_ref = refs[0], refs[1]
        o_ref, acc_ref = refs[-2], refs[-1]
        k = pl.program_id(2)

        @pl.when(k == 0)
        def _():
            acc_ref[...] = jnp.zeros_like(acc_ref)

        acc_ref[...] += lax.dot_general(a_ref[...], b_ref[...], (dims, ((), ())), preferred_element_type=F32)

        @pl.when(k == nk - 1)
        def _():
            r = acc_ref[...]
            if add is not None:
                r = r + refs[2][...]
            o_ref[...] = r.astype(o_ref.dtype)

    ins, specs = [a, b], [a_spec, b_spec]
    if add is not None:
        ins.append(add)
        specs.append(o_spec)
    if after is not None:
        ins.append(after)
        specs.append(pl.BlockSpec(after.shape, lambda i, j, k: (0, 0)))
    return pl.pallas_call(
        functools.partial(body, nk=nk), name=name, grid=(m // tm, n // tn, nk),
        in_specs=specs, out_specs=o_spec, out_shape=out_shape,
        scratch_shapes=[pltpu.VMEM((tm, tn), F32)], compiler_params=_cp("parallel", "parallel", "arbitrary"),
    )(*ins)


def _rms_fwd(h, w, name):
    lp, d = h.shape
    tm = _tile(lp, 384)

    def body(h_ref, w_ref, o_ref):
        xf = h_ref[...]
        r = lax.rsqrt(jnp.mean(xf * xf, axis=-1, keepdims=True) + EPS)
        o_ref[...] = (xf * r * w_ref[...]).astype(o_ref.dtype)

    return pl.pallas_call(
        body, name=name, grid=(lp // tm,),
        in_specs=[pl.BlockSpec((tm, d), lambda i: (i, 0)), pl.BlockSpec((1, d), lambda i: (0, 0))],
        out_specs=pl.BlockSpec((tm, d), lambda i: (i, 0)), out_shape=jax.ShapeDtypeStruct((lp, d), BF16),
        compiler_params=_cp("parallel"))(h, w)


def _rms_bwd(h, w, dhn, dres, name):
    lp, d = h.shape
    tm = _tile(lp, 192)

    def body(h_ref, w_ref, dy_ref, dres_ref, dh_ref, dhb_ref, dw_ref):
        xf = h_ref[...]
        r = lax.rsqrt(jnp.mean(xf * xf, axis=-1, keepdims=True) + EPS)
        xhat = xf * r
        dy = dy_ref[...]
        dxhat = dy * w_ref[...]
        dx = r * (dxhat - xhat * jnp.mean(dxhat * xhat, axis=-1, keepdims=True))
        dh = dres_ref[...] + dx
        dh_ref[...] = dh
        dhb_ref[...] = dh.astype(BF16)

        @pl.when(pl.program_id(0) == 0)
        def _():
            dw_ref[...] = jnp.zeros_like(dw_ref)

        dw_ref[...] += jnp.sum(dy * xhat, axis=0, keepdims=True)

    row = pl.BlockSpec((tm, d), lambda i: (i, 0))
    vec = pl.BlockSpec((1, d), lambda i: (0, 0))
    return pl.pallas_call(
        body, name=name, grid=(lp // tm,), in_specs=[row, vec, row, row], out_specs=(row, row, vec),
        out_shape=(jax.ShapeDtypeStruct((lp, d), F32), jax.ShapeDtypeStruct((lp, d), BF16),
                   jax.ShapeDtypeStruct((1, d), F32)),
        compiler_params=_cp("arbitrary"))(h, w, dhn, dres)


def _conv_pre(xx, w, rows, off):
    acc = None
    for j in range(CONV_K):
        sh = CONV_K - 1 - j
        term = (pltpu.roll(xx, sh, 0) if sh else xx)[off:off + rows] * w[j]
        acc = term if acc is None else acc + term
    return acc


def _conv_fwd(proj, conv_w, ncols, name):
    lp = proj.shape[0]
    tm, tc = _tile(lp, 384), _tile(ncols, 1024, 128)
    hb = tm // 8

    def body(x_ref, xb_ref, w_ref, o_ref):
        before = jnp.where(pl.program_id(0) > 0, xb_ref[...], 0.0)
        xx = jnp.concatenate([before, x_ref[...]], axis=0)
        o_ref[...] = _silu(_conv_pre(xx, [w_ref[j:j + 1, :] for j in range(CONV_K)], tm, 8))

    return pl.pallas_call(
        body, name=name, grid=(lp // tm, ncols // tc),
        in_specs=[pl.BlockSpec((tm, tc), lambda i, j: (i, j)),
                  pl.BlockSpec((8, tc), lambda i, j: (jnp.maximum(i * hb - 1, 0), j)),
                  pl.BlockSpec((CONV_K, tc), lambda i, j: (0, j))],
        out_specs=pl.BlockSpec((tm, tc), lambda i, j: (i, j)),
        out_shape=jax.ShapeDtypeStruct((lp, ncols), F32), compiler_params=_cp("parallel", "parallel"))(proj, proj, conv_w)


def _conv_bwd(proj, col0, conv_w, dact, name, into=None):
    lp, ncols = dact.shape
    tm, tc = _tile(lp, 384), _tile(ncols, 512, 128)
    hb, nt, cb0 = tm // 8, lp // tm, col0 // tc
    assert col0 % tc == 0

    def body(x_ref, xb_ref, xa_ref, d_ref, da_ref, w_ref, *rest):
        dx_ref, dw_ref = rest[-2:]
        i = pl.program_id(1)
        before = jnp.where(i > 0, xb_ref[...], 0.0)
        last = i == nt - 1
        xx = jnp.concatenate([before, x_ref[...], jnp.where(last, 0.0, xa_ref[...])], axis=0)
        w = [w_ref[j:j + 1, :] for j in range(CONV_K)]
        pre = _conv_pre(xx, w, tm + 8, 8)
        dd = jnp.concatenate([d_ref[...], jnp.where(last, 0.0, da_ref[...])], axis=0)
        dpre = dd * _dsilu(pre)
        dx = None
        for j in range(CONV_K):
            sh = CONV_K - 1 - j
            term = (pltpu.roll(dpre, tm + 8 - sh, 0) if sh else dpre)[:tm] * w[j]
            dx = term if dx is None else dx + term
        dx_ref[...] = dx.astype(BF16)

        @pl.when(i == 0)
        def _():
            dw_ref[...] = jnp.zeros_like(dw_ref)

        for j in range(CONV_K):
            sh = CONV_K - 1 - j
            xs = (pltpu.roll(xx, sh, 0) if sh else xx)[8:8 + tm]
            dw_ref[j:j + 1, :] += jnp.sum(dpre[:tm] * xs, axis=0, keepdims=True)

    return pl.pallas_call(
        body, name=name, grid=(ncols // tc, nt),
        in_specs=[pl.BlockSpec((tm, tc), lambda j, i: (i, cb0 + j)),
                  pl.BlockSpec((8, tc), lambda j, i: (jnp.maximum(i * hb - 1, 0), cb0 + j)),
                  pl.BlockSpec((8, tc), lambda j, i: (jnp.minimum((i + 1) * hb, nt * hb - 1), cb0 + j)),
                  pl.BlockSpec((tm, tc), lambda j, i: (i, j)),
                  pl.BlockSpec((8, tc), lambda j, i: (jnp.minimum((i + 1) * hb, nt * hb - 1), j)),
                  pl.BlockSpec((CONV_K, tc), lambda j, i: (0, j))]
        + ([] if into is None else [pl.BlockSpec(memory_space=pl.ANY)]),
        out_specs=(pl.BlockSpec((tm, tc), lambda j, i: (i, (0 if into is None else cb0) + j)),
                   pl.BlockSpec((CONV_K, tc), lambda j, i: (0, j))),
        out_shape=(jax.ShapeDtypeStruct((lp, ncols) if into is None else into.shape, BF16),
                   jax.ShapeDtypeStruct((CONV_K, ncols), F32)),
        input_output_aliases={} if into is None else {6: 0},
        compiler_params=_cp("parallel", "arbitrary"))(proj, proj, proj, dact, dact, conv_w, *([] if into is None else [into]))


def _softplus(x):
    return jnp.maximum(x, 0.0) + jnp.log(1.0 + jnp.exp(-jnp.abs(x)))


def _gates_fwd(gl, a_log2, dt_bias2, name):
    lp, w2 = gl.shape
    hv = w2 // 2

    def body(gl_ref, al_ref, dt_ref, o_ref):
        x = gl_ref[...]
        live = _iota((lp, 1), 0) >= INERT
        is_beta = _iota((1, w2), 1) < hv
        g = -jnp.exp(al_ref[...]) * _softplus(x + dt_ref[...])
        o_ref[...] = jnp.where(live, jnp.where(is_beta, jax.nn.sigmoid(x), g), 0.0)

    return pl.pallas_call(body, name=name, out_shape=jax.ShapeDtypeStruct((lp, w2), F32))(gl, a_log2, dt_bias2)


def _gates_bwd(gl, a_log2, dt_bias2, dbg, name):
    lp, w2 = gl.shape
    hv = w2 // 2

    def body(gl_ref, al_ref, dt_ref, d_ref, dl_ref, dal_ref, ddt_ref):
        x = gl_ref[...]
        live = _iota((lp, 1), 0) >= INERT
        is_beta = _iota((1, w2), 1) < hv
        d = jnp.where(live, d_ref[...], 0.0)
        beta = jax.nn.sigmoid(x)
        ea = jnp.exp(al_ref[...])
        u = x + dt_ref[...]
        dg = jnp.where(is_beta, 0.0, d)
        dal_ref[...] = jnp.sum(dg * (-ea) * _softplus(u), axis=0, keepdims=True)
        du = dg * (-ea) * jax.nn.sigmoid(u)
        ddt_ref[...] = jnp.sum(du, axis=0, keepdims=True)
        dl_ref[...] = jnp.where(is_beta, d * beta * (1.0 - beta), du).astype(BF16)

    vec = jax.ShapeDtypeStruct((1, w2), F32)
    return pl.pallas_call(
        body, name=name, out_shape=(jax.ShapeDtypeStruct((lp, w2), BF16), vec, vec))(gl, a_log2, dt_bias2, dbg)


def _l2n(x):
    r = lax.rsqrt(jnp.sum(x * x, axis=-1, keepdims=True) + EPS)
    return x * r, r


def _tri_inverse(mats):
    eye = (_iota((CH, CH), 0) == _iota((CH, CH), 1)).astype(F32)
    ts = [eye - a for a in mats]
    ps = [_dot_f32(a, a) for a in mats]
    n = 2
    while n < CH:
        ts = [t + _dot_f32(t, p) for t, p in zip(ts, ps)]
        n *= 2
        if n < CH:
            ps = [_dot_f32(p, p) for p in ps]
    return ts


def _chunk_local(qn, kn, v, b_row, g_row):
    ri, ci = _iota((CH, CH), 0), _iota((CH, CH), 1)
    incl, strict = ri >= ci, ri > ci
    gam_col = jnp.sum(jnp.where(incl, g_row, 0.0), axis=1, keepdims=True)
    gam_row = _col_to_row(gam_col)
    b_col = _row_to_col(b_row)
    dec = jnp.exp(jnp.where(incl, gam_col - gam_row, -jnp.inf))
    eg = jnp.exp(gam_col)
    gl = jnp.sum(g_row, axis=1, keepdims=True)
    ekd = jnp.exp(gl - gam_col)
    kb = kn * b_col
    a = jnp.where(strict, _dot_nt(kb, kn) * dec, 0.0)
    p = jnp.where(incl, _dot_nt(qn, kn) * dec, 0.0)
    return dict(dec=dec, eg=eg, ekd=ekd, kb=kb, vb=v * b_col, a=a, kbg=kb * eg, p=p, qd=qn * eg, kd=kn * ekd,
                b_col=b_col, incl=incl, strict=strict)


def _chunks_per_step(nc):
    return max(g for g in (1, 2, 3, 6, 11) if nc % g == 0)


def _heads_per_step(hv):
    return min(hv, 32)


def _delta_local(act, gates, key_w, name):
    lp = act.shape[0]
    hk, nc = key_w // HD, lp // CH
    hv = 2 * hk
    g = _chunks_per_step(nc)
    tr = g * CH

    def body(q_ref, k_ref, v_ref, g_ref, u_ref, w_ref, qd_ref, kd_ref, p_ref, t_ref):
        items = []
        for j in range(g):
            rows = slice(j * CH, (j + 1) * CH)
            qn = _l2n(q_ref[rows, :])[0] * (HD ** -0.5)
            kn = _l2n(k_ref[rows, :])[0]
            for e in range(2):
                cols = slice(e * HD, (e + 1) * HD)
                r = _chunk_local(qn, kn, v_ref[rows, cols], g_ref[0, j, e:e + 1, :], g_ref[0, j, 2 + e:3 + e, :])
                qd_ref[rows, cols] = r["qd"].astype(BF16)
                kd_ref[rows, cols] = r["kd"].astype(BF16)
                p_ref[e, rows, :] = r["p"].astype(BF16)
                items.append((rows, cols, e, r["a"], r["vb"].astype(BF16), r["kbg"].astype(BF16)))
        ts = [t.astype(BF16) for t in _tri_inverse([it[3] for it in items])]
        us = [_dot(t, it[4]) for t, it in zip(ts, items)]
        ws = [_dot(t, it[5]) for t, it in zip(ts, items)]
        for (rows, cols, e, _, _, _), t, u, w in zip(items, ts, us, ws):
            u_ref[rows, cols] = u
            w_ref[rows, cols] = w.astype(BF16)
            t_ref[e, rows, :] = t

    wide = pl.BlockSpec((tr, 2 * HD), lambda h, c: (c, h))
    sq = pl.BlockSpec((2, tr, CH), lambda h, c: (h, c, 0))
    wshape = lambda dt: jax.ShapeDtypeStruct((lp, hv * HD), dt)
    sshape = jax.ShapeDtypeStruct((hv, lp, CH), BF16)
    return pl.pallas_call(
        body, name=name, grid=(hk, nc // g),
        in_specs=[pl.BlockSpec((tr, HD), lambda h, c: (c, h)),
                  pl.BlockSpec((tr, HD), lambda h, c: (c, hk + h)),
                  pl.BlockSpec((tr, 2 * HD), lambda h, c: (c, hk + h)),
                  pl.BlockSpec((1, g, 8, CH), lambda h, c: (h, c, 0, 0))],
        out_specs=(wide, wide, wide, wide, sq, sq),
        out_shape=(wshape(F32), wshape(BF16), wshape(BF16), wshape(BF16), sshape, sshape),
        compiler_params=_cp("parallel", "parallel"))(act, act, act, gates)


def _chunk_decay(g_ref, e):
    return jnp.exp(jnp.sum(g_ref[e // 2, 0, 2 + e % 2:3 + e % 2, :], axis=1, keepdims=True))


def _delta_scan(u, w, qd, kd, p, gates, name):
    lp, val = u.shape
    hv, nc = val // HD, lp // CH
    nh = _heads_per_step(hv)

    def body(u_ref, w_ref, qd_ref, kd_ref, p_ref, g_ref, o_ref, vn_ref, st_ref, s_scr):
        @pl.when(pl.program_id(1) == 0)
        def _():
            s_scr[...] = jnp.zeros_like(s_scr)

        heads = range(nh)
        col = lambda e: slice(e * HD, (e + 1) * HD)
        ss = [s_scr[e] for e in heads]
        sb = [s.astype(BF16) for s in ss]
        for e in heads:
            st_ref[0, e] = ss[e]
        ws = [_dot(w_ref[:, col(e)], sb[e]) for e in heads]
        qs = [_dot(qd_ref[:, col(e)], sb[e]) for e in heads]
        vns = [(u_ref[:, col(e)] - ws[e]).astype(BF16) for e in heads]
        pv = [_dot(p_ref[e], vns[e]) for e in heads]
        kv = [_dot_tn(kd_ref[:, col(e)], vns[e]) for e in heads]
        for e in heads:
            o_ref[:, col(e)] = qs[e] + pv[e]
            s_scr[e] = _chunk_decay(g_ref, e) * ss[e] + kv[e]
            vn_ref[:, col(e)] = vns[e]

    wide = pl.BlockSpec((CH, nh * HD), lambda h, c: (c, h))
    return pl.pallas_call(
        body, name=name, grid=(hv // nh, nc),
        in_specs=[wide, wide, wide, wide, pl.BlockSpec((nh, CH, CH), lambda h, c: (h, c, 0)),
                  pl.BlockSpec((nh // 2, 1, 8, CH), lambda h, c: (h, c, 0, 0))],
        out_specs=(wide, wide, pl.BlockSpec((1, nh, HD, HD), lambda h, c: (c, h, 0, 0))),
        out_shape=(jax.ShapeDtypeStruct((lp, val), F32), jax.ShapeDtypeStruct((lp, val), BF16),
                   jax.ShapeDtypeStruct((nc, hv, HD, HD), F32)),
        scratch_shapes=[pltpu.VMEM((nh, HD, HD), F32)],
        compiler_params=_cp("parallel", "arbitrary"))(u, w, qd, kd, p, gates)


def _delta_scan_bwd(do, w, qd, kd, p, vn, states, gates, name):
    lp, val = do.shape
    hv, nc = val // HD, lp // CH
    nh = _heads_per_step(hv)

    def body(do_ref, w_ref, qd_ref, kd_ref, p_ref, vn_ref, st_ref, g_ref,
             dvn_ref, dw_ref, dqd_ref, dkd_ref, dp_ref, sd_ref, ds_scr):
        @pl.when(pl.program_id(1) == 0)
        def _():
            ds_scr[...] = jnp.zeros_like(ds_scr)

        incl = _iota((CH, CH), 0) >= _iota((CH, CH), 1)
        heads = range(nh)
        col = lambda e: slice(e * HD, (e + 1) * HD)
        ss = [st_ref[0, e] for e in heads]
        dss = [ds_scr[e] for e in heads]
        sb = [s.astype(BF16) for s in ss]
        dsb = [d.astype(BF16) for d in dss]
        dos = [do_ref[:, col(e)].astype(BF16) for e in heads]
        egl = [_chunk_decay(g_ref, e) for e in heads]
        pdo = [_dot_tn(p_ref[e], dos[e]) for e in heads]
        kds = [_dot(kd_ref[:, col(e)], dsb[e]) for e in heads]
        qdo = [_dot_tn(qd_ref[:, col(e)], dos[e]) for e in heads]
        dqd = [_dot_nt(dos[e], sb[e]) for e in heads]
        dkd = [_dot_nt(vn_ref[:, col(e)], dsb[e]) for e in heads]
        dpp = [_dot_nt(dos[e], vn_ref[:, col(e)]) for e in heads]
        dvn = [(pdo[e] + kds[e]).astype(BF16) for e in heads]
        wdv = [_dot_tn(w_ref[:, col(e)], dvn[e]) for e in heads]
        dws = [_dot_nt(dvn[e], sb[e]) for e in heads]
        for e in heads:
            ds_scr[e] = qdo[e] + egl[e] * dss[e] - wdv[e]
            dvn_ref[:, col(e)] = dvn[e]
            dw_ref[:, col(e)] = (-dws[e]).astype(BF16)
            dqd_ref[:, col(e)] = dqd[e]
            dkd_ref[:, col(e)] = dkd[e]
            dp_ref[e] = jnp.where(incl, dpp[e], 0.0)
            sd_ref[0, 0, e:e + 1, :] = jnp.broadcast_to(egl[e] * jnp.sum(ss[e] * dss[e], keepdims=True), (1, HD))

    rev = lambda c: nc - 1 - c
    wide = pl.BlockSpec((CH, nh * HD), lambda h, c: (rev(c), h))
    sq = pl.BlockSpec((nh, CH, CH), lambda h, c: (h, rev(c), 0))
    wshape = lambda dt: jax.ShapeDtypeStruct((lp, val), dt)
    return pl.pallas_call(
        body, name=name, grid=(hv // nh, nc),
        in_specs=[wide, wide, wide, wide, sq, wide, pl.BlockSpec((1, nh, HD, HD), lambda h, c: (rev(c), h, 0, 0)),
                  pl.BlockSpec((nh // 2, 1, 8, CH), lambda h, c: (h, rev(c), 0, 0))],
        out_specs=(wide, wide, wide, wide, sq, pl.BlockSpec((1, 1, nh, HD), lambda h, c: (h, rev(c), 0, 0))),
        out_shape=(wshape(BF16), wshape(BF16), wshape(F32), wshape(F32), jax.ShapeDtypeStruct((hv, lp, CH), F32),
                   jax.ShapeDtypeStruct((hv // nh, nc, nh, HD), F32)),
        scratch_shapes=[pltpu.VMEM((nh, HD, HD), F32)],
        compiler_params=_cp("parallel", "arbitrary"))(do, w, qd, kd, p, vn, states, gates)


def _delta_local_bwd(act, gates, t, dvn, dw, dqd, dkd, dp, key_w, name):
    lp = act.shape[0]
    hk, nc = key_w // HD, lp // CH
    hv = 2 * hk
    g = _chunks_per_step(nc)
    tr = g * CH
    scale = HD ** -0.5

    def body(q_ref, k_ref, v_ref, g_ref, t_ref, dvn_ref, dw_ref, dqd_ref, dkd_ref, dp_ref, dq_ref, dk_ref, dv_ref, dg_ref):
        ri, ci = _iota((CH, CH), 0), _iota((CH, CH), 1)
        inner = lambda x, z: jnp.sum(x * z, axis=1, keepdims=True)
        norms, items = [], []
        for j in range(g):
            rows = slice(j * CH, (j + 1) * CH)
            qh, qr = _l2n(q_ref[rows, :])
            kn, kr = _l2n(k_ref[rows, :])
            qn = qh * scale
            norms.append((rows, qh, qr, kn, kr, qn))
            dg_ref[0, j, 4:8, :] = jnp.zeros((4, CH), F32)
            for e in range(2):
                cols = slice(e * HD, (e + 1) * HD)
                v = v_ref[rows, cols]
                r = _chunk_local(qn, kn, v, g_ref[0, j, e:e + 1, :], g_ref[0, j, 2 + e:3 + e, :])
                items.append(dict(r, j=j, e=e, rows=rows, cols=cols, v=v, kn=kn, qn=qn, t=t_ref[e, rows, :],
                                  dvn=dvn_ref[rows, cols], dw=dw_ref[rows, cols]))
        for it in items:
            it["dt"] = _dot_nt(it["dvn"], it["vb"]) + _dot_nt(it["dw"], it["kbg"])
            it["dvb"] = _dot_tn(it["t"], it["dvn"])
            it["dkbg"] = _dot_tn(it["t"], it["dw"])
        for it in items:
            it["x"] = _dot_tn(it["t"], it["dt"])
        for it in items:
            it["da"] = -jnp.where(it["strict"], _dot_nt(it["x"], it["t"]), 0.0)
        for it in items:
            dp = dp_ref[it["e"], it["rows"], :]
            it["gmat"] = it["da"] * it["a"] + dp * it["p"]
            mm, nn = (it["da"] * it["dec"]).astype(BF16), (dp * it["dec"]).astype(BF16)
            it["dkb"] = _dot(mm, it["kn"]) + it["dkbg"] * it["eg"]
            it["dkn"] = _dot_tn(mm, it["kb"]) + _dot_tn(nn, it["qn"])
            it["dqn"] = _dot(nn, it["kn"])
        for it in items:
            j, e, rows, cols = it["j"], it["e"], it["rows"], it["cols"]
            dqd, dkd, gmat, dkb = dqd_ref[rows, cols], dkd_ref[rows, cols], it["gmat"], it["dkb"]
            it["dkn"] = it["dkn"] + dkd * it["ekd"] + it["b_col"] * dkb
            it["dqn"] = it["dqn"] + dqd * it["eg"]
            dkd_kd = inner(dkd, it["kd"])
            dgam = (jnp.sum(gmat, axis=1, keepdims=True) - _row_to_col(jnp.sum(gmat, axis=0, keepdims=True))
                    + inner(dqd, it["qd"]) + inner(it["dkbg"], it["kbg"]) - dkd_kd)
            dgl = jnp.max(g_ref[0, j, 4 + e:5 + e, :], axis=1, keepdims=True) + jnp.sum(dkd_kd, keepdims=True)
            dgam = dgam + jnp.where(_iota((CH, 1), 0) == CH - 1, dgl, 0.0)
            dg_ref[0, j, 2 + e:3 + e, :] = jnp.sum(jnp.where(ri >= ci, dgam, 0.0), axis=0, keepdims=True)
            dg_ref[0, j, e:e + 1, :] = _col_to_row(inner(dkb, it["kn"]) + inner(it["dvb"], it["v"]))
            dv_ref[rows, cols] = it["b_col"] * it["dvb"]
        for j, (rows, qh, qr, kn, kr, _) in enumerate(norms):
            dqh = (items[2 * j]["dqn"] + items[2 * j + 1]["dqn"]) * scale
            dkn = items[2 * j]["dkn"] + items[2 * j + 1]["dkn"]
            dq_ref[rows, :] = qr * (dqh - qh * jnp.sum(dqh * qh, axis=1, keepdims=True))
            dk_ref[rows, :] = kr * (dkn - kn * jnp.sum(dkn * kn, axis=1, keepdims=True))

    narrow = pl.BlockSpec((tr, HD), lambda h, c: (c, h))
    wide = pl.BlockSpec((tr, 2 * HD), lambda h, c: (c, h))
    sq = pl.BlockSpec((2, tr, CH), lambda h, c: (h, c, 0))
    gate = pl.BlockSpec((1, g, 8, CH), lambda h, c: (h, c, 0, 0))
    return pl.pallas_call(
        body, name=name, grid=(hk, nc // g),
        in_specs=[narrow, pl.BlockSpec((tr, HD), lambda h, c: (c, hk + h)),
                  pl.BlockSpec((tr, 2 * HD), lambda h, c: (c, hk + h)), gate, sq, wide, wide, wide, wide, sq],
        out_specs=(narrow, narrow, wide, gate),
        out_shape=(jax.ShapeDtypeStruct((lp, key_w), F32), jax.ShapeDtypeStruct((lp, key_w), F32),
                   jax.ShapeDtypeStruct((lp, hv * HD), F32), jax.ShapeDtypeStruct((hk, nc, 8, CH), F32)),
        compiler_params=_cp("parallel", "parallel"))(act, act, act, gates, t, dvn, dw, dqd, dkd, dp)


def _head_group(nheads):
    return 4 if nheads % 4 == 0 else 1


def _outnorm_fwd(o, proj, z_col0, w, name):
    lp, val = o.shape
    hg = _head_group(val // HD)
    bw = hg * HD
    tm, zb = _tile(lp, 1056), z_col0 // bw

    def body(o_ref, z_ref, w_ref, y_ref):
        for j in range(hg):
            cols = slice(j * HD, (j + 1) * HD)
            xf = o_ref[:, cols]
            r = lax.rsqrt(jnp.mean(xf * xf, axis=-1, keepdims=True) + EPS)
            y_ref[:, cols] = (xf * r * w_ref[...] * _silu(z_ref[:, cols])).astype(BF16)

    return pl.pallas_call(
        body, name=name, grid=(lp // tm, val // bw),
        in_specs=[pl.BlockSpec((tm, bw), lambda i, h: (i, h)), pl.BlockSpec((tm, bw), lambda i, h: (i, zb + h)),
                  pl.BlockSpec((1, HD), lambda i, h: (0, 0))],
        out_specs=pl.BlockSpec((tm, bw), lambda i, h: (i, h)), out_shape=jax.ShapeDtypeStruct((lp, val), BF16),
        compiler_params=_cp("parallel", "parallel"))(o, proj, w)


def _outnorm_bwd(o, proj, z_col0, w, dy, name):
    lp, val = o.shape
    hg = _head_group(val // HD)
    bw = hg * HD
    tm, zb = _tile(lp, 1056), z_col0 // bw

    def body(o_ref, z_ref, w_ref, dy_ref, do_ref, dz_ref, dw_ref):
        @pl.when((pl.program_id(0) == 0) & (pl.program_id(1) == 0))
        def _():
            dw_ref[...] = jnp.zeros_like(dw_ref)

        for j in range(hg):
            cols = slice(j * HD, (j + 1) * HD)
            xf, z, d = o_ref[:, cols], z_ref[:, cols], dy_ref[:, cols]
            r = lax.rsqrt(jnp.mean(xf * xf, axis=-1, keepdims=True) + EPS)
            xhat = xf * r
            dn = d * _silu(z)
            dz_ref[:, cols] = (d * xhat * w_ref[...] * _dsilu(z)).astype(BF16)
            dxhat = dn * w_ref[...]
            do_ref[:, cols] = r * (dxhat - xhat * jnp.mean(dxhat * xhat, axis=-1, keepdims=True))
            dw_ref[...] += jnp.sum(dn * xhat, axis=0, keepdims=True)

    blk = pl.BlockSpec((tm, bw), lambda i, h: (i, h))
    vec = pl.BlockSpec((1, HD), lambda i, h: (0, 0))
    return pl.pallas_call(
        body, name=name, grid=(lp // tm, val // bw),
        in_specs=[blk, pl.BlockSpec((tm, bw), lambda i, h: (i, zb + h)), vec, blk],
        out_specs=(blk, pl.BlockSpec((tm, bw), lambda i, h: (i, zb + h)), vec),
        out_shape=(jax.ShapeDtypeStruct((lp, val), F32), jax.ShapeDtypeStruct((lp, z_col0 + val), BF16),
                   jax.ShapeDtypeStruct((1, HD), F32)),
        compiler_params=_cp("arbitrary", "arbitrary"))(o, proj, w, dy)


def _qknorm_fwd(proj, qw, kw, width, name):
    lp = proj.shape[0]
    hg = _head_group(width // HD)
    bw = hg * HD
    tm, nh = _tile(lp, 1056), width // bw

    def body(q_ref, k_ref, v_ref, qw_ref, kw_ref, qo_ref, ko_ref, vo_ref):
        for x_ref, w_ref, o_ref in ((q_ref, qw_ref, qo_ref), (k_ref, kw_ref, ko_ref)):
            for j in range(hg):
                cols = slice(j * HD, (j + 1) * HD)
                xf = x_ref[:, cols]
                r = lax.rsqrt(jnp.mean(xf * xf, axis=-1, keepdims=True) + EPS)
                o_ref[:, cols] = (xf * r * w_ref[...]).astype(BF16)
        vo_ref[...] = v_ref[...].astype(BF16)

    blk = lambda off: pl.BlockSpec((tm, bw), lambda i, h: (i, off + h))
    vec = pl.BlockSpec((1, HD), lambda i, h: (0, 0))
    shp = jax.ShapeDtypeStruct((lp, width), BF16)
    return pl.pallas_call(
        body, name=name, grid=(lp // tm, nh), in_specs=[blk(0), blk(nh), blk(2 * nh), vec, vec],
        out_specs=(blk(0), blk(0), blk(0)), out_shape=(shp, shp, shp),
        compiler_params=_cp("parallel", "parallel"))(proj, proj, proj, qw, kw)


def _qknorm_bwd(proj, qw, kw, dqn, dkn, width, name):
    lp = proj.shape[0]
    hg = _head_group(width // HD)
    bw = hg * HD
    tm, nh = _tile(lp, 1056), width // bw

    def body(q_ref, k_ref, qw_ref, kw_ref, dqn_ref, dkn_ref, dq_ref, dk_ref, dqw_ref, dkw_ref):
        first = (pl.program_id(0) == 0) & (pl.program_id(1) == 0)
        for x_ref, w_ref, dy_ref, dx_ref, dw_ref in ((q_ref, qw_ref, dqn_ref, dq_ref, dqw_ref),
                                                       (k_ref, kw_ref, dkn_ref, dk_ref, dkw_ref)):
            @pl.when(first)
            def _():
                dw_ref[...] = jnp.zeros_like(dw_ref)

            for j in range(hg):
                cols = slice(j * HD, (j + 1) * HD)
                xf, dy = x_ref[:, cols], dy_ref[:, cols]
                r = lax.rsqrt(jnp.mean(xf * xf, axis=-1, keepdims=True) + EPS)
                xhat = xf * r
                dxhat = dy * w_ref[...]
                dx_ref[:, cols] = (r * (dxhat - xhat * jnp.mean(dxhat * xhat, axis=-1, keepdims=True))).astype(BF16)
                dw_ref[...] += jnp.sum(dy * xhat, axis=0, keepdims=True)

    blk = lambda off: pl.BlockSpec((tm, bw), lambda i, h: (i, off + h))
    vec = pl.BlockSpec((1, HD), lambda i, h: (0, 0))
    shp = jax.ShapeDtypeStruct((lp, width), BF16)
    vshp = jax.ShapeDtypeStruct((1, HD), F32)
    return pl.pallas_call(
        body, name=name, grid=(lp // tm, nh), in_specs=[blk(0), blk(nh), vec, vec, blk(0), blk(0)],
        out_specs=(blk(0), blk(0), vec, vec), out_shape=(shp, shp, vshp, vshp),
        compiler_params=_cp("arbitrary", "arbitrary"))(proj, proj, qw, kw, dqn, dkn)


def _sb_tq(lp):
    return 3 * QB if lp % (3 * QB) == 0 else QB


def _add_rows(x, r0, delta):
    return x + delta if r0 == 0 else jnp.concatenate([x[:r0], x[r0:] + delta], axis=0)


def _sb_rows(kb):
    return pl.ds(kb * QB if isinstance(kb, int) else pl.multiple_of(kb * QB, QB), QB)


def _sb_scores(qk, t_idx, kb, masked):
    z = qk * (HD ** -0.5)
    sp = jnp.log(1.0 + jnp.exp(-jnp.abs(z)))
    lsz = jnp.minimum(z, 0.0) - sp
    lk = -jnp.maximum(z, 0.0) - sp
    if not masked:
        return None, lsz, lk
    s_idx = kb * QB + _iota((1, QB), 1)
    valid = (s_idx < t_idx) & (s_idx >= INERT)
    return valid, lsz, jnp.where(valid, lk, 0.0)


def _sb_fwd(qn, kn, vv, proj, gate_col0, name):
    lp, width = qn.shape
    tq = _sb_tq(lp)
    nh, nq, gb, nsub = width // HD, lp // tq, gate_col0 // HD, tq // QB

    def body(q_ref, k_ref, v_ref, g_ref, o_ref, og_ref, tot_ref):
        qb = pl.program_id(1)
        q = q_ref[...]
        t_idx = qb * tq + _iota((tq, 1), 0)
        upper = (_iota((QB, QB), 0) > _iota((QB, QB), 1)).astype(BF16)

        def step(kg, carry, masked):
            run, acc = carry
            kbs = [kg * nsub + sub for sub in reversed(range(nsub))]
            rows = [_sb_rows(kb) for kb in kbs]
            qks = [_dot_nt(q, k_ref[r, :]) for r in rows]
            scores = [_sb_scores(qk, t_idx, kb, masked) for qk, kb in zip(qks, kbs)]
            sums = [_dot_split(lk, upper) for _, _, lk in scores]
            probs = []
            for (valid, lsz, lk), part in zip(scores, sums):
                a = jnp.exp(lsz + part + run)
                probs.append((jnp.where(valid, a, 0.0) if masked else a).astype(BF16))
                run = run + jnp.sum(lk, axis=1, keepdims=True)
            for a, r in zip(probs, rows):
                acc = acc + _dot(a, v_ref[r, :])
            return run, acc

        carry = step(qb, (jnp.zeros((tq, 1), F32), jnp.zeros((tq, HD), F32)), True)
        carry = lax.fori_loop(1, qb, lambda i, cr: step(qb - i, cr, False), carry)
        run, acc = lax.fori_loop(0, jnp.minimum(qb, 1), lambda _, cr: step(0, cr, True), carry)
        o_ref[...] = acc
        og_ref[...] = (acc * _silu(g_ref[...])).astype(BF16)
        tot_ref[0, 0] = _col_to_row(run)

    full = pl.BlockSpec((lp, HD), lambda h, i: (0, h))
    blk = pl.BlockSpec((tq, HD), lambda h, i: (i, h))
    return pl.pallas_call(
        body, name=name, grid=(nh, nq),
        in_specs=[blk, full, full, pl.BlockSpec((tq, HD), lambda h, i: (i, gb + h))],
        out_specs=(blk, blk, pl.BlockSpec((1, 1, 1, tq), lambda h, i: (h, i, 0, 0))),
        out_shape=(jax.ShapeDtypeStruct((lp, width), F32), jax.ShapeDtypeStruct((lp, width), BF16),
                   jax.ShapeDtypeStruct((nh, nq, 1, tq), F32)),
        compiler_params=_cp("parallel", "parallel"))(qn, kn, vv, proj)


def _sb_bwd(qn, kn, vv, proj, gate_col0, att, tot, dog, name):
    lp, width = qn.shape
    tq = _sb_tq(lp)
    nh, nq, gb, nsub = width // HD, lp // tq, gate_col0 // HD, tq // QB
    scale = HD ** -0.5

    def body(q_ref, k_ref, v_ref, g_ref, att_ref, tot_ref, dog_ref, dq_ref, dk_ref, dv_ref, dg_ref, dk_acc, dv_acc):
        qb = pl.program_id(1)

        @pl.when(qb == 0)
        def _():
            dk_acc[...] = jnp.zeros_like(dk_acc)
            dv_acc[...] = jnp.zeros_like(dv_acc)

        q, gate, dg_out = q_ref[...], g_ref[...], dog_ref[...]
        d_o = (dg_out * _silu(gate)).astype(BF16)
        dg_ref[...] = (dg_out * att_ref[...] * _dsilu(gate)).astype(BF16)
        total = _row_to_col(tot_ref[0, 0])
        t_idx = qb * tq + _iota((tq, 1), 0)
        ri, ci = _iota((QB, QB), 0), _iota((QB, QB), 1)
        lower_incl = (ri <= ci).astype(BF16)
        lower_excl = (ri < ci).astype(BF16)

        def step(kg, carry, masked, diag=False):
            run, erun, dq = carry
            kbs = [kg * nsub + sub for sub in range(nsub)]
            rows = [_sb_rows(kb) for kb in kbs]
            r0s = [sub * QB if diag else 0 for sub in range(nsub)]
            qks = [_dot_nt(q[r0:], k_ref[r, :]) for r, r0 in zip(rows, r0s)]
            dprobs = [_dot_nt(d_o[r0:], v_ref[r, :]) for r, r0 in zip(rows, r0s)]
            scores = [_sb_scores(qk, t_idx[r0:], kb, masked) for qk, kb, r0 in zip(qks, kbs, r0s)]
            sums = [_dot_split(lk, lower_incl) for _, _, lk in scores]
            probs, es = [], []
            for (valid, lsz, lk), part, dprob, r0 in zip(scores, sums, dprobs, r0s):
                a = jnp.exp(lsz + (total[r0:] - run[r0:] - part))
                a = jnp.where(valid, a, 0.0) if masked else a
                probs.append(a.astype(BF16))
                es.append(a * dprob)
                run = _add_rows(run, r0, jnp.sum(lk, axis=1, keepdims=True))
            esums = [_dot_split(e, lower_excl) for e in es]
            for a, r, r0 in zip(probs, rows, r0s):
                dv_acc[r, :] += _dot_tn(a, d_o[r0:])
            dzs = []
            for (valid, lsz, _), e, part, r0 in zip(scores, es, esums, r0s):
                sig = jnp.exp(lsz)
                dz = e * (1.0 - sig) - sig * (erun[r0:] + part)
                dz = jnp.where(valid, dz, 0.0) if masked else dz
                dzs.append((dz * scale).astype(BF16))
                erun = _add_rows(erun, r0, jnp.sum(e, axis=1, keepdims=True))
            for dz, r, r0 in zip(dzs, rows, r0s):
                dk_acc[r, :] += _dot_tn(dz, q[r0:])
                dq = _add_rows(dq, r0, _dot(dz, k_ref[r, :]))
            return run, erun, dq

        zero = jnp.zeros((tq, 1), F32)
        carry = step(0, (zero, zero, jnp.zeros((tq, HD), F32)), True)
        carry = lax.fori_loop(1, qb, lambda kg, cr: step(kg, cr, False), carry)
        _, _, dq = lax.fori_loop(0, jnp.minimum(qb, 1), lambda _, cr: step(qb, cr, True, diag=True), carry)
        dq_ref[...] = dq

        @pl.when(qb == nq - 1)
        def _():
            dk_ref[...] = dk_acc[...]
            dv_ref[...] = dv_acc[...].astype(BF16)

    full = pl.BlockSpec((lp, HD), lambda h, i: (0, h))
    blk = pl.BlockSpec((tq, HD), lambda h, i: (i, h))
    return pl.pallas_call(
        body, name=name, grid=(nh, nq),
        in_specs=[blk, full, full, pl.BlockSpec((tq, HD), lambda h, i: (i, gb + h)), blk,
                  pl.BlockSpec((1, 1, 1, tq), lambda h, i: (h, i, 0, 0)), blk],
        out_specs=(blk, full, full, blk),
        out_shape=(jax.ShapeDtypeStruct((lp, width), F32), jax.ShapeDtypeStruct((lp, width), F32),
                   jax.ShapeDtypeStruct((lp, width), BF16), jax.ShapeDtypeStruct((lp, width), BF16)),
        scratch_shapes=[pltpu.VMEM((lp, HD), F32), pltpu.VMEM((lp, HD), F32)],
        compiler_params=_cp("parallel", "arbitrary"))(qn, kn, vv, proj, att, tot, dog)


def _loss_head(h, target, name):
    lp, d = h.shape
    tm = _tile(PAD, 128)
    nt = lp // tm
    npad = PAD // tm

    def body(h_ref, t_ref, l_ref, dh_ref, dhb_ref):
        i = pl.program_id(0)
        err = jnp.where(i >= npad, h_ref[...] - t_ref[...], 0.0)
        dh = err * (1.0 / d)
        dh_ref[...] = dh
        dhb_ref[...] = dh.astype(BF16)

        @pl.when(i == 0)
        def _():
            l_ref[...] = jnp.zeros_like(l_ref)

        l_ref[...] += (0.5 / d) * jnp.sum(err * err, keepdims=True)

    row = pl.BlockSpec((tm, d), lambda i: (i, 0))
    return pl.pallas_call(
        body, name=name, grid=(nt,),
        in_specs=[row, pl.BlockSpec((tm, d), lambda i: (jnp.maximum(i - npad, 0), 0))],
        out_specs=(pl.BlockSpec((1, 1), lambda i: (0, 0)), row, row),
        out_shape=(jax.ShapeDtypeStruct((1, 1), F32), jax.ShapeDtypeStruct((lp, d), F32),
                   jax.ShapeDtypeStruct((lp, d), BF16)),
        compiler_params=_cp("arbitrary"))(h, target)


def _adamw(parts, w, m, v, name, mine=None):
    r, c = w.shape
    tr = _tile(r, max(8, (1 << 18) // c // 8 * 8))
    nparts = parts.shape[0]

    def body(*refs):
        p_ref, w_ref, m_ref, v_ref = refs[:4]
        g_ref, d_ref, mo_ref, vo_ref = refs[-4:]
        g = p_ref[0].astype(F32)
        if mine is not None:
            g = refs[4][...].astype(F32) + g
        for k in range(1, nparts):
            g = g + p_ref[k].astype(F32)
        mn = ADAM_B1 * m_ref[...] + (1.0 - ADAM_B1) * g
        vn = ADAM_B2 * v_ref[...] + (1.0 - ADAM_B2) * jnp.square(g)
        m_hat = mn / (1.0 - ADAM_B1 ** ADAM_STEP)
        v_hat = vn / (1.0 - ADAM_B2 ** ADAM_STEP)
        g_ref[...] = g
        d_ref[...] = -ADAM_LR * (m_hat / (jnp.sqrt(v_hat) + ADAM_EPS) + ADAM_WD * w_ref[...])
        mo_ref[...] = mn
        vo_ref[...] = vn

    blk = pl.BlockSpec((tr, c), lambda i: (i, 0))
    shp = jax.ShapeDtypeStruct((r, c), F32)
    extra = [] if mine is None else [mine]
    return pl.pallas_call(
        body, name=name, grid=(r // tr,),
        in_specs=[pl.BlockSpec((nparts, tr, c), lambda i: (0, i, 0)), blk, blk, blk] + [blk] * len(extra),
        out_specs=(blk, blk, blk, blk), out_shape=(shp, shp, shp, shp), compiler_params=_cp("parallel"))(parts, w, m, v, *extra)


def _unshard_cols(g):
    return jnp.transpose(g, (1, 0, 2)).reshape(g.shape[1], NDEV * g.shape[2])


def _shard_cols(a):
    r, c = a.shape
    return jnp.transpose(a.reshape(r, NDEV, c // NDEV), (1, 0, 2))


def kernel(x, meta_tokens, dn_norm_w, dn_w_in, dn_conv_w, dn_a_log, dn_dt_bias, dn_out_norm_w, dn_w_out, sb_norm_w, sb_w_in, sb_q_norm_w, sb_k_norm_w, sb_w_out, loss_target, m_meta_tokens, m_dn_norm_w, m_dn_w_in, m_dn_conv_w, m_dn_a_log, m_dn_dt_bias, m_dn_out_norm_w, m_dn_w_out, m_sb_norm_w, m_sb_w_in, m_sb_q_norm_w, m_sb_k_norm_w, m_sb_w_out, v_meta_tokens, v_dn_norm_w, v_dn_w_in, v_dn_conv_w, v_dn_a_log, v_dn_dt_bias, v_dn_out_norm_w, v_dn_w_out, v_sb_norm_w, v_sb_w_in, v_sb_q_norm_w, v_sb_k_norm_w, v_sb_w_out):
    seq, d = x.shape[1], x.shape[2]
    lp = PAD + seq
    key_w = d
    val_w = 2 * d
    hv = val_w // HD
    conv_w_cols = 2 * key_w + val_w
    main_w = conv_w_cols + val_w
    sb_w = d
    nc = lp // CH
    hk = key_w // HD

    (g_meta, g_sbn, g_conv) = _exchange([meta_tokens, sb_norm_w, dn_conv_w[0]], True, "gather_vectors")
    w_in0_mine = dn_w_in[0].astype(BF16)
    st_a, tok_a = _split_start([w_in0_mine], [lax.empty((NDEV,) + w_in0_mine.shape, BF16)], _plan_gather_chips, 4,
                               "gather_w_in0_chips_start", after=g_meta)
    meta = _unshard_cols(g_meta) + tok_a[:1, :1]
    sbn_w = _unshard_cols(g_sbn)
    conv_w = _unshard_cols(g_conv)
    h0 = jnp.concatenate([jnp.zeros((INERT, d), F32), meta, x[0]], axis=0)
    hn0 = _rms_fwd(h0, dn_norm_w, "dn_norm")
    lands_a, srcs_a = _split_wait(st_a, hn0, "gather_w_in0_chips_wait")
    st_a2, tok_a2 = _split_start([], lands_a, _plan_gather_forward, 3, "gather_w_in0_forward_start")
    st_b, tok_b = _exchange_start([dn_w_out[0].astype(BF16), sb_w_in[0].astype(BF16), sb_w_out[0].astype(BF16)],
                                  True, "gather_w_rest_start", after=tok_a2)
    lands_a, _ = _split_wait(st_a2, tok_b, "gather_w_in0_forward_wait")
    w_in0 = _unshard_cols(_with_own(lands_a[0], srcs_a[0]))
    main_cols, gate_cols = (0, main_w), (main_w, 2 * hv)

    proj0 = _matmul(hn0, w_in0, mode="nn", out_dtype=F32, tm=1056, tn=512, tk=4096, name="dn_in_proj", b_cols=main_cols)
    gl0 = _matmul(hn0, w_in0, mode="nn", out_dtype=F32, tm=1056, tn=512, tk=4096, name="dn_gate_proj", b_cols=gate_cols)
    act0 = _conv_fwd(proj0, conv_w, conv_w_cols, "dn_conv")
    a_log2 = jnp.concatenate([jnp.zeros_like(dn_a_log), dn_a_log], axis=1)
    dt_bias2 = jnp.concatenate([jnp.zeros_like(dn_dt_bias), dn_dt_bias], axis=1)
    bg = _gates_fwd(gl0, a_log2, dt_bias2, "dn_gates")
    pack = lambda t: jnp.transpose(t.reshape(nc, CH, hk, 2), (2, 0, 3, 1))
    gates = jnp.concatenate([pack(bg[:, :hv]), pack(bg[:, hv:]), jnp.zeros((hk, nc, 4, CH), F32)], axis=2)
    u0, w0, qd0, kd0, p0, t0 = _delta_local(act0, gates, key_w, "dn_delta_local")
    o0, vn0, states = _delta_scan(u0, w0, qd0, kd0, p0, gates, "dn_delta_scan")
    o0g = _outnorm_fwd(o0, proj0, conv_w_cols, dn_out_norm_w, "dn_out_norm")
    lands_b, srcs_b = _exchange_wait(st_b, o0g, "gather_w_rest_wait")
    g_out0, g_in1, g_out1 = [_with_own(l, s) for l, s in zip(lands_b, srcs_b)]
    w_out0 = g_out0.reshape(val_w, d)
    w_out1 = g_out1.reshape(sb_w, d)
    h1 = _matmul(o0g, w_out0, mode="nn", out_dtype=F32, tm=1056, tn=512, tk=4096, name="dn_out_proj", add=h0)
    hn1 = _rms_fwd(h1, sbn_w, "sb_norm")
    proj1 = _matmul(hn1, g_in1, mode="nn", out_dtype=F32, tm=1056, tn=512, tk=4096, name="sb_in_proj", sharded=True)
    qn1, kn1, vv1 = _qknorm_fwd(proj1, sb_q_norm_w, sb_k_norm_w, sb_w, "sb_qk_norm")
    att1, o1g, tot1 = _sb_fwd(qn1, kn1, vv1, proj1, 3 * sb_w, "sb_attn")
    h2 = _matmul(o1g, w_out1, mode="nn", out_dtype=F32, tm=1056, tn=512, tk=4096, name="sb_out_proj", add=h1)
    loss_part, dh2, dh2b = _loss_head(h2, loss_target[0], "loss_head")
    loss = lax.psum(loss_part[0, 0], ("x", "y", "c"))

    p_out1 = _matmul(o1g, dh2b, mode="tn", out_dtype=BF16, tm=1024, tn=512, tk=lp, name="sb_out_wgrad")
    do1g = _matmul(dh2b, w_out1, mode="nt", out_dtype=F32, tm=1056, tn=512, tk=4096, name="sb_out_dgrad")
    dqn1, dkn1, dv1, dgate1 = _sb_bwd(qn1, kn1, vv1, proj1, 3 * sb_w, att1, tot1, do1g, "sb_attn_bwd")
    dq1, dk1, d_qw, d_kw = _qknorm_bwd(proj1, sb_q_norm_w, sb_k_norm_w, dqn1, dkn1, sb_w, "sb_qk_norm_bwd")
    dproj1 = jnp.concatenate([dq1, dk1, dv1, dgate1], axis=1)
    p_in1 = _matmul(hn1, dproj1, mode="tn", out_dtype=BF16, tm=1024, tn=512, tk=lp, name="sb_in_wgrad", sharded=True)
    st_s1, tok_s1 = _exchange_start([p_out1.reshape(NDEV, sb_w // NDEV, d), p_in1], False, "scatter_sb_start")
    dhn1 = _matmul(dproj1, g_in1, mode="nt", out_dtype=F32, tm=1056, tn=512, tk=4096, name="sb_in_dgrad", after=tok_s1,
                   sharded=True)
    dh1, dh1b, d_sbn = _rms_bwd(h1, sbn_w, dhn1, dh2, "sb_norm_bwd")

    p_out0 = _matmul(o0g, dh1b, mode="tn", out_dtype=BF16, tm=1024, tn=512, tk=lp, name="dn_out_wgrad")
    st_s2, tok_s2 = _exchange_start([p_out0.reshape(NDEV, val_w // NDEV, d)], False, "scatter_dn_out_start")
    do0g = _matmul(dh1b, w_out0, mode="nt", out_dtype=F32, tm=1056, tn=512, tk=4096, name="dn_out_dgrad", after=tok_s2)
    do0, dz0, d_onw = _outnorm_bwd(o0, proj0, conv_w_cols, dn_out_norm_w, do0g, "dn_out_norm_bwd")
    dvn0, dw0, dqd0, dkd0, dp0, sd0 = _delta_scan_bwd(do0, w0, qd0, kd0, p0, vn0, states, gates, "dn_delta_scan_bwd")
    sd_rows = jnp.transpose(jnp.transpose(sd0[..., 0], (0, 2, 1)).reshape(hk, 2, nc), (0, 2, 1))
    gates_b = jnp.concatenate([gates[:, :, :4], jnp.broadcast_to(sd_rows[..., None], (hk, nc, 2, CH)),
                               jnp.zeros((hk, nc, 2, CH), F32)], axis=2)
    dq_act, dk_act, dv_act, dgates = _delta_local_bwd(act0, gates_b, t0, dvn0, dw0, dqd0, dkd0, dp0, key_w,
                                                      "dn_delta_local_bwd")
    unpack = lambda t: jnp.transpose(t, (1, 3, 0, 2)).reshape(lp, hv)
    dbg = jnp.concatenate([unpack(dgates[:, :, 0:2]), unpack(dgates[:, :, 2:4])], axis=1)
    dgl0, d_alog2, d_dtb2 = _gates_bwd(gl0, a_log2, dt_bias2, dbg, "dn_gates_bwd")
    d_alog, d_dtb = d_alog2[:, hv:], d_dtb2[:, hv:]
    dproj0, dcw_q = _conv_bwd(proj0, 0, conv_w[:, :key_w], dq_act, "dn_conv_bwd_q", into=dz0)
    dproj0, dcw_k = _conv_bwd(proj0, key_w, conv_w[:, key_w:2 * key_w], dk_act, "dn_conv_bwd_k", into=dproj0)
    dproj0, dcw_v = _conv_bwd(proj0, 2 * key_w, conv_w[:, 2 * key_w:], dv_act, "dn_conv_bwd_v", into=dproj0)
    p_in0_main = _matmul(hn0, dproj0, mode="tn", out_dtype=BF16, tm=1024, tn=512, tk=lp, name="dn_in_wgrad")
    p_in0_gate = _matmul(hn0, dgl0, mode="tn", out_dtype=BF16, tm=1024, tn=512, tk=lp, name="dn_gate_wgrad")
    p_in0 = _shard_cols(jnp.concatenate([p_in0_main, p_in0_gate], axis=1))
    st_c, tok_c = _split_start([p_in0], [lax.empty((4,) + p_in0.shape[1:], BF16)], _plan_scatter_core, 4,
                               "scatter_dn_in_core_start")
    dhn0 = _matmul(dgl0, w_in0, mode="nt", out_dtype=F32, tm=1056, tn=512, tk=4096, name="dn_gate_dgrad", after=tok_c,
                   b_cols=gate_cols)
    (zone_c,), (p_in0,) = _split_wait(st_c, dhn0, "scatter_dn_in_core_wait")
    sums_in0 = _chip_sums(p_in0, zone_c, "dn_in_chip_sums")
    st_s3, tok_s3 = _split_start([sums_in0], [lax.empty((3,) + p_in0.shape[1:], BF16)], _plan_scatter_chips, 3,
                                 "scatter_dn_in_chips_start")
    dhn0 = _matmul(dproj0, w_in0, mode="nt", out_dtype=F32, tm=1056, tn=512, tk=4096, name="dn_in_dgrad", add=dhn0,
                   after=tok_s3, b_cols=main_cols)
    dh0, _, d_dnn = _rms_bwd(h0, dn_norm_w, dhn0, dh1, "dn_norm_bwd")
    grad_x = dh0[PAD:][None]

    p_conv = _shard_cols(jnp.concatenate([dcw_q, dcw_k, dcw_v], axis=1))
    (r_meta, r_sbn, r_conv) = _exchange([_shard_cols(dh0[INERT:PAD]), _shard_cols(d_sbn), p_conv], False, "scatter_vector_grads")
    small = jnp.concatenate([d_dnn, d_alog, d_dtb, d_onw, d_qw, d_kw], axis=1)
    (r_small,) = _exchange([small], True, "gather_replicated_grads")
    outs = {}
    outs["meta_tokens"] = _adamw(r_meta, meta_tokens, m_meta_tokens, v_meta_tokens, "adamw_meta")
    outs["dn_conv_w"] = _adamw(r_conv, dn_conv_w[0], m_dn_conv_w[0], v_dn_conv_w[0], "adamw_dn_conv")
    outs["sb_norm_w"] = _adamw(r_sbn, sb_norm_w, m_sb_norm_w, v_sb_norm_w, "adamw_sb_norm")
    me = _own_index()
    own = lambda src: lax.dynamic_index_in_dim(src, me, 0, keepdims=False)
    (r_out1, r_in1), (s_out1, s_in1) = _exchange_wait(st_s1, r_small, "scatter_sb_wait")
    outs["sb_w_in"] = _adamw(r_in1, sb_w_in[0], m_sb_w_in[0], v_sb_w_in[0], "adamw_sb_w_in", mine=own(s_in1))
    outs["sb_w_out"] = _adamw(r_out1, sb_w_out[0], m_sb_w_out[0], v_sb_w_out[0], "adamw_sb_w_out", mine=own(s_out1))
    (r_out0,), (s_out0,) = _exchange_wait(st_s2, outs["sb_w_in"][1], "scatter_dn_out_wait")
    outs["dn_w_out"] = _adamw(r_out0, dn_w_out[0], m_dn_w_out[0], v_dn_w_out[0], "adamw_dn_w_out", mine=own(s_out0))
    cat = lambda *a: jnp.concatenate(a, axis=1)
    rep = _adamw(r_small, cat(dn_norm_w, dn_a_log, dn_dt_bias, dn_out_norm_w, sb_q_norm_w, sb_k_norm_w),
                 cat(m_dn_norm_w, m_dn_a_log, m_dn_dt_bias, m_dn_out_norm_w, m_sb_q_norm_w, m_sb_k_norm_w),
                 cat(v_dn_norm_w, v_dn_a_log, v_dn_dt_bias, v_dn_out_norm_w, v_sb_q_norm_w, v_sb_k_norm_w),
                 "adamw_replicated")
    off = 0
    for nm, wd in (("dn_norm_w", d), ("dn_a_log", hv), ("dn_dt_bias", hv), ("dn_out_norm_w", HD),
                   ("sb_q_norm_w", HD), ("sb_k_norm_w", HD)):
        outs[nm] = tuple(t[:, off:off + wd] for t in rep)
        off += wd
    behind = jnp.broadcast_to(outs["dn_w_out"][1][0, 0] + rep[1][0, 0] + outs["sb_w_out"][1][0, 0] + outs["meta_tokens"][1][0, 0]
                              + outs["dn_conv_w"][1][0, 0] + outs["sb_norm_w"][1][0, 0], (8, 128))
    (r_in0,), (sums_in0,) = _split_wait(st_s3, behind, "scatter_dn_in_chips_wait")
    outs["dn_w_in"] = _adamw(r_in0, dn_w_in[0], m_dn_w_in[0], v_dn_w_in[0], "adamw_dn_w_in", mine=sums_in0[0])
    lead =("dn_w_in", "dn_conv_w", "dn_w_out", "sb_w_in", "sb_w_out")
    order = ("meta_tokens", "dn_norm_w", "dn_w_in", "dn_conv_w", "dn_a_log", "dn_dt_bias", "dn_out_norm_w", "dn_w_out",
             "sb_norm_w", "sb_w_in", "sb_q_norm_w", "sb_k_norm_w", "sb_w_out")
    fix = lambda nm, t: t[None] if nm in lead else t
    result = [loss, grad_x]
    for kind in range(4):
        result += [fix(nm, outs[nm][kind]) for nm in order]
    return tuple(result)
```
